```python
import jax, jax.numpy as jnp
from jax import lax
import numpy as np

D_MODEL = 1024
BATCH = 8
SEQ = 4096
DEPTH = 2
DEC_BATCH = 8
DEC_SEQ = 32
PAST_LEN = 2048

CHUNK = 64
QBLOCK = 128
NORM_EPS = 1e-6
MLA_HEADS = 8
Q_LORA = 384
KV_LORA = 256
NOPE_DIM = 64
ROPE_DIM = 32
V_HEAD = 64
ROPE_THETA = 10000.0
SM_SCALE = (NOPE_DIM + ROPE_DIM) ** -0.5
RWKV_HEADS = 4
RWKV_HEAD = 64
RWKV_DIM = RWKV_HEADS * RWKV_HEAD
W_LORA = 64
A_LORA = 64
G_LORA = 128
RWKV_IN = 3 * RWKV_DIM + W_LORA + A_LORA + G_LORA
GN_EPS = 64e-5
CONV_DIM = 256
CONV_WIDTH = 31
LN_EPS = 1e-5
N_BRANCH = 3
MLA_IN = Q_LORA + KV_LORA + ROPE_DIM
IN_COLS = MLA_IN + RWKV_IN + 2 * CONV_DIM + N_BRANCH * D_MODEL
N_MOD = 6
N_GROUPS = 4
EXPERTS_PER_GROUP = 4
N_EXPERTS = N_GROUPS * EXPERTS_PER_GROUP
TOP_K = 2
D_EXPERT = 256

kernel_name = "hybrid_mla_rwkv7_conformer_hmoe_stream_step"


def rmsnorm(x, g):
    xf = x.astype(jnp.float32)
    y = xf * lax.rsqrt(jnp.mean(xf * xf, axis=-1, keepdims=True) + NORM_EPS)
    return (y * g.astype(jnp.float32)).astype(x.dtype)


def layernorm(x, g, b):
    xf = x.astype(jnp.float32)
    mu = jnp.mean(xf, axis=-1, keepdims=True)
    var = jnp.mean(jnp.square(xf - mu), axis=-1, keepdims=True)
    y = (xf - mu) * lax.rsqrt(var + LN_EPS) * g.astype(jnp.float32) + b.astype(jnp.float32)
    return y.astype(x.dtype)


def rope(x, pos):
    half = ROPE_DIM // 2
    inv = ROPE_THETA ** (-jnp.arange(half, dtype=jnp.float32) / half)
    ang = pos.astype(jnp.float32)[:, None] * inv[None, :]
    shape = (1, pos.shape[0]) + (1,) * (x.ndim - 3) + (half,)
    cos, sin = jnp.cos(ang).reshape(shape), jnp.sin(ang).reshape(shape)
    xf = x.astype(jnp.float32)
    x1, x2 = xf[..., :half], xf[..., half:]
    return jnp.concatenate([x1 * cos - x2 * sin, x1 * sin + x2 * cos], axis=-1).astype(x.dtype)


def chunk_causal_attend(q_lat, q_rope, q_pos, k_lat, k_rope, k_pos):
    B, T, H, C = q_lat.shape
    qb = min(QBLOCK, T)
    nb = T // qb
    k_chunk = k_pos // CHUNK

    def block(args):
        ql, qr, qp = args
        s = (jnp.einsum('bqhc,bkc->bhqk', ql, k_lat).astype(jnp.float32)
             + jnp.einsum('bqhr,bkr->bhqk', qr, k_rope).astype(jnp.float32)) * SM_SCALE
        visible = k_chunk[None, :] <= (qp // CHUNK)[:, None]
        s = jnp.where(visible[None, None], s, -jnp.inf)
        p = jax.nn.softmax(s, axis=-1).astype(k_lat.dtype)
        return jnp.einsum('bhqk,bkc->bqhc', p, k_lat)

    split = lambda t: jnp.moveaxis(t.reshape((B, nb, qb) + t.shape[2:]), 1, 0)
    out = lax.map(block, (split(q_lat), split(q_rope), q_pos.reshape(nb, qb)))
    return jnp.moveaxis(out, 0, 1).reshape(B, T, H, C)


def mla_branch(p_q, p_kv, p_kr, cache_lat, cache_rope, lp):
    B, T, _ = p_q.shape
    P = cache_lat.shape[1]
    pos = P + jnp.arange(T)
    q = (rmsnorm(p_q, lp['q_norm_g']) @ lp['q_up']).reshape(B, T, MLA_HEADS, NOPE_DIM + ROPE_DIM)
    q_nope, q_rope = q[..., :NOPE_DIM], rope(q[..., NOPE_DIM:], pos)
    lat = rmsnorm(p_kv, lp['kv_norm_g'])
    kr = rope(p_kr, pos)
    k_lat = jnp.concatenate([cache_lat.astype(lat.dtype), lat], axis=1)
    k_rope = jnp.concatenate([cache_rope.astype(kr.dtype), kr], axis=1)
    q_lat = jnp.einsum('bthn,chn->bthc', q_nope, lp['w_uk'])
    o_lat = chunk_causal_attend(q_lat, q_rope, pos, k_lat, k_rope, jnp.arange(P + T))
    o = jnp.einsum('bthc,chv->bthv', o_lat, lp['w_uv']).reshape(B, T, MLA_HEADS * V_HEAD)
    return o @ lp['proj_a'], lat, kr


def wkv_scan(S0, r, w, k, v, kk, a):
    def step(S, inp):
        r_t, w_t, k_t, v_t, kk_t, a_t = inp
        sa = jnp.einsum('bhij,bhj->bhi', S, -kk_t)
        S = (S * w_t[:, :, None, :] + sa[..., None] * (kk_t * a_t)[:, :, None, :]
             + v_t[..., None] * k_t[:, :, None, :])
        return S, jnp.einsum('bhij,bhj->bhi', S, r_t)
    xs = tuple(jnp.moveaxis(t, 1, 0) for t in (r, w, k, v, kk, a))
    S, ys = lax.scan(step, S0, xs)
    return S, jnp.moveaxis(ys, 0, 1)


def rwkv_branch(pr, shift_st, wkv_st, lp):
    B, T, _ = pr.shape
    f32 = jnp.float32
    prev = jnp.concatenate([shift_st[:, None, :].astype(pr.dtype), pr[:, :-1]], axis=1)
    xs = pr + (prev - pr) * lp['rwkv_mu']
    o1 = RWKV_DIM
    r, k, v, wl, al, gl = jnp.split(xs, [o1, 2 * o1, 3 * o1, 3 * o1 + W_LORA, 3 * o1 + W_LORA + A_LORA], axis=-1)
    heads = lambda t: t.astype(f32).reshape(B, T, RWKV_HEADS, RWKV_HEAD)
    z = (lp['rwkv_w0'] + jnp.tanh(wl) @ lp['rwkv_w2']).astype(f32)
    decay = jnp.exp(-jnp.exp(-jax.nn.softplus(-z) - 0.5))
    a = jax.nn.sigmoid((lp['rwkv_a0'] + al @ lp['rwkv_a2']).astype(f32))
    g = (jax.nn.sigmoid(gl) @ lp['rwkv_g2']).astype(f32)
    kf = k.astype(f32)
    kk = heads(kf * lp['rwkv_k_k'])
    kk = kk * lax.rsqrt(jnp.sum(kk * kk, axis=-1, keepdims=True) + 1e-12)
    kf = kf * (1.0 + (a - 1.0) * lp['rwkv_k_a'])
    rh, kh, vh, ah, wh = heads(r), heads(kf), heads(v), heads(a), heads(decay)
    S, y = wkv_scan(wkv_st.astype(f32), rh, wh, kh, vh, kk, ah)
    mu = jnp.mean(y, axis=-1, keepdims=True)
    var = jnp.mean(jnp.square(y - mu), axis=-1, keepdims=True)
    y = ((y - mu) * lax.rsqrt(var + GN_EPS)).reshape(B, T, RWKV_DIM)
    y = y * lp['rwkv_ln_g'].astype(f32) + lp['rwkv_ln_b'].astype(f32)
    bonus = jnp.sum(rh * kh * lp['rwkv_r_k'].astype(f32), axis=-1, keepdims=True) * vh
    y = (y + bonus.reshape(B, T, RWKV_DIM)) * g
    return y.astype(pr.dtype) @ lp['proj_b'], pr[:, -1], S.astype(pr.dtype)


def conv_branch(pc, conv_st, lp):
    glu = pc[..., :CONV_DIM] * jax.nn.sigmoid(pc[..., CONV_DIM:])
    buf = jnp.concatenate([conv_st.astype(glu.dtype), glu], axis=1)
    y = lax.conv_general_dilated(buf, lp['conv_dw'][:, None, :], window_strides=(1,), padding='VALID',
                                 dimension_numbers=('NWC', 'WIO', 'NWC'), feature_group_count=CONV_DIM)
    y = jax.nn.silu(layernorm(y + lp['conv_dw_b'], lp['conv_ln_g'], lp['conv_ln_b']))
    return y @ lp['proj_c'], buf[:, -(CONV_WIDTH - 1):]


def hier_moe(h, lp):
    f32 = jnp.float32
    B, T, _ = h.shape
    g_prob = jax.nn.softmax((h @ lp['router_group_w'] + lp['router_group_b']).astype(f32), axis=-1)
    g_val, g_idx = lax.top_k(g_prob, 1)
    e_logits = (h @ lp['router_expert_w'] + lp['router_expert_b']).astype(f32)
    e_logits = e_logits.reshape(B, T, N_GROUPS, EXPERTS_PER_GROUP)
    in_group = jnp.einsum('btg,btge->bte', jax.nn.one_hot(g_idx[..., 0], N_GROUPS, dtype=f32), e_logits)
    e_val, e_idx = lax.top_k(in_group, TOP_K)
    weights = g_val * jax.nn.softmax(e_val, axis=-1)
    expert_id = g_idx * EXPERTS_PER_GROUP + e_idx
    combine = jnp.einsum('btk,btke->bte', weights, jax.nn.one_hot(expert_id, N_EXPERTS, dtype=f32)).astype(h.dtype)
    gate = jnp.einsum('btd,edf->btef', h, lp['moe_w_gate'])
    up = jnp.einsum('btd,edf->btef', h, lp['moe_w_up'])
    act = jax.nn.silu(gate) * up * combine[..., None]
    return jnp.einsum('btef,efd->btd', act, lp['moe_w_down'])


def trunk_layer(x, c, cache_lat, cache_rope, shift_st, wkv_st, conv_st, lp):
    B, T, D = x.shape
    mod = (jax.nn.silu(c) @ lp['ada_w'] + lp['ada_b']).reshape(B, N_MOD, D)
    sh1, sc1, gt1, sh2, sc2, gt2 = [mod[:, i, None, :] for i in range(N_MOD)]
    h = rmsnorm(x, lp['norm1_g']) * (1 + sc1) + sh1
    p = h @ lp['w_in']
    cuts = [Q_LORA, Q_LORA + KV_LORA, MLA_IN, MLA_IN + RWKV_IN, MLA_IN + RWKV_IN + 2 * CONV_DIM]
    p_q, p_kv, p_kr, p_rw, p_cv, p_gate = jnp.split(p, cuts, axis=-1)
    o_a, lat, kr = mla_branch(p_q, p_kv, p_kr, cache_lat, cache_rope, lp)
    o_b, shift_new, wkv_new = rwkv_branch(p_rw, shift_st, wkv_st, lp)
    o_c, conv_new = conv_branch(p_cv, conv_st, lp)
    gates = jax.nn.sigmoid(p_gate).reshape(B, T, N_BRANCH, D)
    merged = gates[:, :, 0] * o_a + gates[:, :, 1] * o_b + gates[:, :, 2] * o_c
    x = x + gt1 * (merged @ lp['w_out'])
    h2 = rmsnorm(x, lp['norm2_g']) * (1 + sc2) + sh2
    x = x + gt2 * hier_moe(h2, lp)
    return x, (lat, kr, shift_new, wkv_new, conv_new)


def setup_inputs(seed: int = 0) -> dict:
    key = jax.random.key(seed)
    ks = iter(jax.random.split(key, 64))
    nrm = lambda shape, std: std * jax.random.normal(next(ks), shape, jnp.float32)
    L, D = DEPTH, D_MODEL
    return {
        'x_prompt': nrm((BATCH, SEQ, D), 1.0),
        'x_sample': nrm((DEC_BATCH, DEC_SEQ, D), 1.0),
        'c_prompt': nrm((BATCH, D), 1.0),
        'c_sample': nrm((DEC_BATCH, D), 1.0),
        'cache_kv_latent': nrm((L, DEC_BATCH, PAST_LEN, KV_LORA), 1.0),
        'cache_k_rope': nrm((L, DEC_BATCH, PAST_LEN, ROPE_DIM), 1.0),
        'state_rwkv_shift': nrm((L, DEC_BATCH, RWKV_IN), 1.0),
        'state_rwkv_wkv': nrm((L, DEC_BATCH, RWKV_HEADS, RWKV_HEAD, RWKV_HEAD), 0.5),
        'state_conv': nrm((L, DEC_BATCH, CONV_WIDTH - 1, CONV_DIM), 0.5),
        'ada_w': nrm((L, D, N_MOD * D), 0.5 * D ** -0.5),
        'ada_b': nrm((L, N_MOD * D), 0.02),
        'norm1_g': 1.0 + nrm((L, D), 0.05),
        'norm2_g': 1.0 + nrm((L, D), 0.05),
        'w_in': nrm((L, D, IN_COLS), D ** -0.5),
        'q_norm_g': 1.0 + nrm((L, Q_LORA), 0.05),
        'q_up': nrm((L, Q_LORA, MLA_HEADS * (NOPE_DIM + ROPE_DIM)), Q_LORA ** -0.5),
        'kv_norm_g': 1.0 + nrm((L, KV_LORA), 0.05),
        'w_uk': nrm((L, KV_LORA, MLA_HEADS, NOPE_DIM), KV_LORA ** -0.5),
        'w_uv': nrm((L, KV_LORA, MLA_HEADS, V_HEAD), KV_LORA ** -0.5),
        'proj_a': nrm((L, MLA_HEADS * V_HEAD, D), (MLA_HEADS * V_HEAD) ** -0.5),
        'rwkv_mu': jax.random.uniform(next(ks), (L, RWKV_IN), jnp.float32),
        'rwkv_w0': nrm((L, RWKV_DIM), 1.0),
        'rwkv_w2': nrm((L, W_LORA, RWKV_DIM), 0.5 * W_LORA ** -0.5),
        'rwkv_a0': nrm((L, RWKV_DIM), 0.5),
        'rwkv_a2': nrm((L, A_LORA, RWKV_DIM), 0.5 * A_LORA ** -0.5),
        'rwkv_g2': nrm((L, G_LORA, RWKV_DIM), G_LORA ** -0.5),
        'rwkv_k_k': 0.85 + nrm((L, RWKV_DIM), 0.1),
        'rwkv_k_a': 1.0 + nrm((L, RWKV_DIM), 0.1),
        'rwkv_r_k': nrm((L, RWKV_HEADS, RWKV_HEAD), 0.1),
        'rwkv_ln_g': 1.0 + nrm((L, RWKV_DIM), 0.05),
        'rwkv_ln_b': nrm((L, RWKV_DIM), 0.02),
        'proj_b': nrm((L, RWKV_DIM, D), RWKV_DIM ** -0.5),
        'conv_dw': nrm((L, CONV_WIDTH, CONV_DIM), CONV_WIDTH ** -0.5),
        'conv_dw_b': nrm((L, CONV_DIM), 0.02),
        'conv_ln_g': 1.0 + nrm((L, CONV_DIM), 0.05),
        'conv_ln_b': nrm((L, CONV_DIM), 0.02),
        'proj_c': nrm((L, CONV_DIM, D), CONV_DIM ** -0.5),
        'w_out': nrm((L, D, D), D ** -0.5),
        'router_group_w': nrm((L, D, N_GROUPS), D ** -0.5),
        'router_group_b': nrm((L, N_GROUPS), 0.01),
        'router_expert_w': nrm((L, D, N_EXPERTS), D ** -0.5),
        'router_expert_b': nrm((L, N_EXPERTS), 0.01),
        'moe_w_gate': nrm((L, N_EXPERTS, D, D_EXPERT), D ** -0.5),
        'moe_w_up': nrm((L, N_EXPERTS, D, D_EXPERT), D ** -0.5),
        'moe_w_down': nrm((L, N_EXPERTS, D_EXPERT, D), D_EXPERT ** -0.5),
        'final_g': 1.0 + nrm((D,), 0.05),
    }


def reference(x_prompt, x_sample, c_prompt, c_sample, cache_kv_latent, cache_k_rope, state_rwkv_shift,
              state_rwkv_wkv, state_conv, ada_w, ada_b, norm1_g, norm2_g, w_in, q_norm_g, q_up, kv_norm_g,
              w_uk, w_uv, proj_a, rwkv_mu, rwkv_w0, rwkv_w2, rwkv_a0, rwkv_a2, rwkv_g2, rwkv_k_k, rwkv_k_a,
              rwkv_r_k, rwkv_ln_g, rwkv_ln_b, proj_b, conv_dw, conv_dw_b, conv_ln_g, conv_ln_b, proj_c, w_out,
              router_group_w, router_group_b, router_expert_w, router_expert_b, moe_w_gate, moe_w_up,
              moe_w_down, final_g):
    bp = x_prompt.shape[0]
    dt = x_prompt.dtype
    lat0 = jnp.zeros((bp, 0, KV_LORA), dt)
    rope0 = jnp.zeros((bp, 0, ROPE_DIM), dt)
    shift0 = jnp.zeros((bp, RWKV_IN), dt)
    wkv0 = jnp.zeros((bp, RWKV_HEADS, RWKV_HEAD, RWKV_HEAD), dt)
    conv0 = jnp.zeros((bp, CONV_WIDTH - 1, CONV_DIM), dt)
    hp, hs = x_prompt, x_sample
    new_p, new_s = [], []
    for l in range(DEPTH):
        lp = dict(ada_w=ada_w[l], ada_b=ada_b[l], norm1_g=norm1_g[l], norm2_g=norm2_g[l], w_in=w_in[l],
                  q_norm_g=q_norm_g[l], q_up=q_up[l], kv_norm_g=kv_norm_g[l], w_uk=w_uk[l], w_uv=w_uv[l],
                  proj_a=proj_a[l], rwkv_mu=rwkv_mu[l], rwkv_w0=rwkv_w0[l], rwkv_w2=rwkv_w2[l],
                  rwkv_a0=rwkv_a0[l], rwkv_a2=rwkv_a2[l], rwkv_g2=rwkv_g2[l], rwkv_k_k=rwkv_k_k[l],
                  rwkv_k_a=rwkv_k_a[l], rwkv_r_k=rwkv_r_k[l], rwkv_ln_g=rwkv_ln_g[l], rwkv_ln_b=rwkv_ln_b[l],
                  proj_b=proj_b[l], conv_dw=conv_dw[l], conv_dw_b=conv_dw_b[l], conv_ln_g=conv_ln_g[l],
                  conv_ln_b=conv_ln_b[l], proj_c=proj_c[l], w_out=w_out[l], router_group_w=router_group_w[l],
                  router_group_b=router_group_b[l], router_expert_w=router_expert_w[l],
                  router_expert_b=router_expert_b[l], moe_w_gate=moe_w_gate[l], moe_w_up=moe_w_up[l],
                  moe_w_down=moe_w_down[l])
        hp, st_p = trunk_layer(hp, c_prompt, lat0, rope0, shift0, wkv0, conv0, lp)
        hs, st_s = trunk_layer(hs, c_sample, cache_kv_latent[l], cache_k_rope[l], state_rwkv_shift[l],
                               state_rwkv_wkv[l], state_conv[l], lp)
        new_p.append(st_p)
        new_s.append(st_s)
    y_prompt = rmsnorm(hp, final_g)
    y_sample = rmsnorm(hs, final_g)
    stack = lambda states, i: jnp.stack([s[i] for s in states], axis=0)
    p_lat, p_rope, p_shift, p_wkv, p_conv = [stack(new_p, i) for i in range(5)]
    s_lat, s_rope, s_shift, s_wkv, s_conv = [stack(new_s, i) for i in range(5)]
    return (y_prompt, y_sample, p_lat, p_rope, p_shift, p_wkv, p_conv, s_lat, s_rope, s_shift, s_wkv, s_conv)
```

```python
import functools

import numpy as np
import jax
import jax.numpy as jnp
from jax import lax
from jax.experimental import pallas as pl
from jax.experimental.pallas import tpu as pltpu

F32 = jnp.float32
BF16 = jnp.bfloat16

D_MODEL = 1024
CHUNK = 64
NORM_EPS = 1e-6
MLA_HEADS = 8
Q_LORA = 384
KV_LORA = 256
NOPE_DIM = 64
ROPE_DIM = 32
V_HEAD = 64
ROPE_THETA = 10000.0
SM_SCALE = (NOPE_DIM + ROPE_DIM) ** -0.5
RWKV_HEADS = 4
RWKV_HEAD = 64
RWKV_DIM = RWKV_HEADS * RWKV_HEAD
W_LORA = 64
A_LORA = 64
G_LORA = 128
RWKV_IN = 3 * RWKV_DIM + W_LORA + A_LORA + G_LORA
GN_EPS = 64e-5
CONV_DIM = 256
CONV_WIDTH = 31
LN_EPS = 1e-5
MLA_IN = Q_LORA + KV_LORA + ROPE_DIM
N_MOD = 6
N_GROUPS = 4
EXPERTS_PER_GROUP = 4
N_EXPERTS = 16
D_EXPERT = 256
WKV_CHUNK = 64
CONV_HALO = 32
ROUTER_LANES = 128
VMEM_LIMIT = 56 * 1024 * 1024


def _cp(sem):
    return pltpu.CompilerParams(dimension_semantics=sem, vmem_limit_bytes=VMEM_LIMIT)


def _full(a):
    nd = a.ndim
    return pl.BlockSpec(a.shape, lambda *_: (0,) * nd)


def _dot(a, b, dims=(((1,), (0,)), ((), ()))):
    return lax.dot_general(a.astype(BF16), b.astype(BF16), dims, preferred_element_type=F32)


def _split(a):
    hi = a.astype(BF16)
    lo = (a - hi.astype(F32)).astype(BF16)
    return hi, lo


def _dot_hl(a, b, dims=(((1,), (0,)), ((), ()))):
    hi, lo = _split(a)
    bb = b.astype(BF16)
    return (lax.dot_general(hi, bb, dims, preferred_element_type=F32)
            + lax.dot_general(lo, bb, dims, preferred_element_type=F32))


def _dot3(a, b, dims=(((1,), (0,)), ((), ()))):
    ah, al = _split(a)
    bh, bl = _split(b)
    d = lambda x, y: lax.dot_general(x, y, dims, preferred_element_type=F32)
    return d(ah, bh) + d(ah, bl) + d(al, bh)


def _sigmoid(x):
    return 1.0 / (1.0 + jnp.exp(-x))


def _rms(x, g):
    return x * lax.rsqrt(jnp.mean(x * x, axis=-1, keepdims=True) + NORM_EPS) * g


def _ada_kernel(c_ref, w_ref, b_ref, o_ref):
    c = c_ref[...]
    o_ref[0] = _dot(c * _sigmoid(c), w_ref[0]) + b_ref[0]


def _ada_mod(c_all, ada_w, ada_b):
    L, D, C = ada_w.shape
    R = c_all.shape[0]
    nb = C // D
    return pl.pallas_call(
        _ada_kernel,
        grid=(L, nb),
        in_specs=[pl.BlockSpec((R, D), lambda l, j: (0, 0)),
                  pl.BlockSpec((1, D, D), lambda l, j: (l, 0, j)),
                  pl.BlockSpec((1, 1, D), lambda l, j: (l, 0, j))],
        out_specs=pl.BlockSpec((1, R, D), lambda l, j: (l, 0, j)),
        out_shape=jax.ShapeDtypeStruct((L, R, C), F32),
        compiler_params=_cp(("parallel", "parallel")),
        name="ada_mod",
    )(c_all, ada_w, ada_b.reshape(L, 1, C))


def _in_proj_kernel(x_ref, mod_ref, g1_ref, wq_ref, wkv_ref, wkr_ref, wrw_ref, wcv_ref, qg_ref, wqu_ref,
                    wuk_ref, kvg_ref, cos_ref, sin_ref,
                    lat_ref, kr_ref, k512_ref, qlat_ref, qrope_ref, prw_ref, glu_ref):
    x = x_ref[0]
    mod = mod_ref[0]
    h = _rms(x, g1_ref[...]) * (1.0 + mod[1:2]) + mod[0:1]
    hb = h.astype(BF16)
    cos = cos_ref[...]
    sin = sin_ref[...]
    qn = _rms(_dot(hb, wq_ref[...]), qg_ref[...]).astype(BF16)
    q = _dot(qn, wqu_ref[...])
    qrope_ref[0] = ((q[:, 512:768] * cos + q[:, 768:1024] * sin) * SM_SCALE).astype(BF16)
    qnope = q[:, 0:512].astype(BF16)
    for hd in range(MLA_HEADS):
        pair = qnope[:, 128 * (hd // 2):128 * (hd // 2) + 128]
        qlat_ref[0, hd] = (_dot(pair, wuk_ref[hd]) * SM_SCALE).astype(BF16)
    lat = _rms(_dot(hb, wkv_ref[...]), kvg_ref[...])
    lat_ref[0] = lat
    pkr = _dot(hb, wkr_ref[...])
    krt = pkr[:, 0:256] * cos + pkr[:, 256:512] * sin
    kr_ref[0] = krt[:, 0:ROPE_DIM]
    k512_ref[0, :, 0:256] = lat.astype(BF16)
    k512_ref[0, :, 256:512] = krt.astype(BF16)
    prw_ref[0] = _dot(hb, wrw_ref[...])
    pcv = _dot(hb, wcv_ref[...])
    glu_ref[0] = pcv[:, 0:CONV_DIM] * _sigmoid(pcv[:, CONV_DIM:2 * CONV_DIM])


def _in_proj(x, mod, lw, cos8, sin8, tm):
    B, T, D = x.shape
    nT = T // tm
    row = lambda w: pl.BlockSpec((1, tm, w), lambda b, i: (b, i, 0))
    args = (lw['g1'], lw['wq'], lw['wkv'], lw['wkr'], lw['wrw'], lw['wcv'], lw['qg'], lw['wqu'], lw['wukp'],
            lw['kvg'])
    return pl.pallas_call(
        _in_proj_kernel,
        grid=(B, nT),
        in_specs=[row(D), pl.BlockSpec((1, N_MOD, D), lambda b, i: (b, 0, 0))] + [_full(a) for a in args]
                 + [pl.BlockSpec((tm, 256), lambda b, i: (i, 0))] * 2,
        out_specs=[row(KV_LORA), row(ROPE_DIM), row(512),
                   pl.BlockSpec((1, MLA_HEADS, tm, KV_LORA), lambda b, i: (b, 0, i, 0)),
                   row(256), row(RWKV_IN), row(CONV_DIM)],
        out_shape=[jax.ShapeDtypeStruct((B, T, KV_LORA), F32),
                   jax.ShapeDtypeStruct((B, T, ROPE_DIM), F32),
                   jax.ShapeDtypeStruct((B, T, 512), BF16),
                   jax.ShapeDtypeStruct((B, MLA_HEADS, T, KV_LORA), BF16),
                   jax.ShapeDtypeStruct((B, T, 256), BF16),
                   jax.ShapeDtypeStruct((B, T, RWKV_IN), F32),
                   jax.ShapeDtypeStruct((B, T, CONV_DIM), F32)],
        compiler_params=_cp(("parallel", "parallel")),
        name="in_proj",
    )(x, mod, *args, cos8, sin8)


def _attn_kernel(qi_ref, ki_ref, last_ref, qlat_ref, qrope_ref, k_ref, wuv_ref, pa_ref, o_ref,
                 qcat, m_s, l_s, acc, *, tq, tk, past, n_keys):
    p = pl.program_id(1)
    qi = qi_ref[p]
    ki = ki_ref[p]
    R = MLA_HEADS * tq

    @pl.when(ki == 0)
    def _():
        qcat[:, 0:256] = qlat_ref[0].reshape(R, 256)
        qr = qrope_ref[0].astype(F32)
        lane = lax.broadcasted_iota(jnp.int32, (tq, 256), 1)
        for hd in range(MLA_HEADS):
            qcat[hd * tq:(hd + 1) * tq, 256:512] = jnp.where((lane >> 5) == hd, qr, 0.0).astype(BF16)
        m_s[...] = jnp.full((R, 1), -jnp.inf, F32)
        l_s[...] = jnp.zeros((R, 1), F32)
        acc[...] = jnp.zeros((R, 256), F32)

    k = k_ref[0]
    s = lax.dot_general(qcat[...], k, (((1,), (1,)), ((), ())), preferred_element_type=F32)
    t = lax.broadcasted_iota(jnp.int32, (R, 1), 0) & (tq - 1)
    limit = jnp.minimum((((past + qi * tq + t) >> 6) + 1) << 6, n_keys)
    kpos = ki * tk + lax.broadcasted_iota(jnp.int32, (1, tk), 1)
    s = jnp.where(kpos < limit, s, -jnp.inf)
    m_prev = m_s[...]
    m_new = jnp.maximum(m_prev, jnp.max(s, axis=1, keepdims=True))
    alpha = jnp.exp(m_prev - m_new)
    pe = jnp.exp(s - m_new)
    l_s[...] = alpha * l_s[...] + jnp.sum(pe, axis=1, keepdims=True)
    acc[...] = alpha * acc[...] + _dot(pe, k[:, 0:256])
    m_s[...] = m_new

    @pl.when(last_ref[p] == 1)
    def _():
        o = acc[...] / l_s[...]
        pairs = []
        for pi in range(MLA_HEADS // 2):
            o0 = o[(2 * pi) * tq:(2 * pi + 1) * tq]
            o1 = o[(2 * pi + 1) * tq:(2 * pi + 2) * tq]
            pairs.append(_dot(o0, wuv_ref[2 * pi]) + _dot(o1, wuv_ref[2 * pi + 1]))
        o_ref[0] = _dot(jnp.concatenate(pairs, axis=1), pa_ref[...])


def _attn_tiles(T, past):
    n_keys = past + T
    tq = min(128, T)
    if n_keys <= 2304:
        tk = -(-n_keys // 128) * 128
    else:
        tk = 512
    s_pad = -(-n_keys // tk) * tk
    return tq, tk, s_pad, n_keys


def _attention(qlat, qrope, k512, wuvp, proj_a, past):
    B, H, T, _ = qlat.shape
    tq, tk, s_pad, n_keys = _attn_tiles(T, past)
    assert k512.shape[1] == s_pad and tq & (tq - 1) == 0 and past % CHUNK == 0
    qi, ki, last = [], [], []
    for i in range(T // tq):
        lim = min(((past + (i + 1) * tq - 1) // CHUNK + 1) * CHUNK, n_keys)
        nk = -(-lim // tk)
        for j in range(nk):
            qi.append(i), ki.append(j), last.append(int(j == nk - 1))
    npairs = len(qi)
    tabs = [jnp.asarray(np.asarray(a, np.int32)) for a in (qi, ki, last)]
    R = H * tq
    grid_spec = pltpu.PrefetchScalarGridSpec(
        num_scalar_prefetch=3,
        grid=(B, npairs),
        in_specs=[pl.BlockSpec((1, H, tq, 256), lambda b, p, qi, ki, la: (b, 0, qi[p], 0)),
                  pl.BlockSpec((1, tq, 256), lambda b, p, qi, ki, la: (b, qi[p], 0)),
                  pl.BlockSpec((1, tk, 512), lambda b, p, qi, ki, la: (b, ki[p], 0)),
                  pl.BlockSpec(wuvp.shape, lambda b, p, qi, ki, la: (0, 0, 0)),
                  pl.BlockSpec(proj_a.shape, lambda b, p, qi, ki, la: (0, 0))],
        out_specs=pl.BlockSpec((1, tq, D_MODEL), lambda b, p, qi, ki, la: (b, qi[p], 0)),
        scratch_shapes=[pltpu.VMEM((R, 512), BF16), pltpu.VMEM((R, 1), F32), pltpu.VMEM((R, 1), F32),
                        pltpu.VMEM((R, 256), F32)])
    return pl.pallas_call(
        functools.partial(_attn_kernel, tq=tq, tk=tk, past=past, n_keys=n_keys),
        grid_spec=grid_spec,
        out_shape=jax.ShapeDtypeStruct((B, T, D_MODEL), F32),
        compiler_params=_cp(("parallel", "arbitrary")),
        name="attention",
    )(*tabs, qlat, qrope, k512, wuvp, proj_a)


def _head_sum(x, ones_bd):
    return _dot_hl(x, ones_bd)


def _rwkv_prep_kernel(prw_ref, prev_ref, mu_ref, w0_ref, w2_ref, a0_ref, a2_ref, g2_ref, kk_ref, ka_ref, rk_ref,
                      ones_ref, r_ref, k_ref, b_ref, kkn_ref, v_ref, lw_ref, g_ref, bonus_ref, *, tm):
    pr = prw_ref[0]
    row = lax.broadcasted_iota(jnp.int32, (tm, 1), 0)
    prev = jnp.where(row == 0, prev_ref[0, 0], pltpu.roll(pr, 1, 0))
    xs = pr + (prev - pr) * mu_ref[...]
    r = xs[:, 0:256]
    k = xs[:, 256:512]
    v = xs[:, 512:768]
    wa = xs[:, 768:896]
    gl = xs[:, 896:1024]
    z = w0_ref[...] + _dot(jnp.tanh(wa), w2_ref[...])
    nz = -z
    softplus = jnp.maximum(nz, 0.0) + jnp.log(1.0 + jnp.exp(-jnp.abs(nz)))
    lw_ref[0] = -jnp.exp(-softplus - 0.5)
    a = _sigmoid(a0_ref[...] + _dot(wa, a2_ref[...]))
    g_ref[0] = _dot(_sigmoid(gl), g2_ref[...])
    ones_bd = ones_ref[...]
    kk = k * kk_ref[...]
    kk = kk * lax.rsqrt(_head_sum(kk * kk, ones_bd) + 1e-12)
    kf = k * (1.0 + (a - 1.0) * ka_ref[...])
    r_ref[0] = r
    k_ref[0] = kf
    v_ref[0] = v
    kkn_ref[0] = kk
    b_ref[0] = kk * a
    bonus_ref[0] = _head_sum(r * kf * rk_ref[...], ones_bd) * v


def _rwkv_prep(prw, prev_rows, lw, tm):
    B, T, _ = prw.shape
    nT = T // tm
    args = (lw['mu'], lw['w0'], lw['w2p'], lw['a0'], lw['a2p'], lw['g2'], lw['k_k'], lw['k_a'], lw['r_k'],
            lw['ones_bd'])
    row = pl.BlockSpec((1, tm, 256), lambda b, i: (b, i, 0))
    return pl.pallas_call(
        functools.partial(_rwkv_prep_kernel, tm=tm),
        grid=(B, nT),
        in_specs=[pl.BlockSpec((1, tm, RWKV_IN), lambda b, i: (b, i, 0)),
                  pl.BlockSpec((1, 1, 1, RWKV_IN), lambda b, i: (b, i, 0, 0))] + [_full(a) for a in args],
        out_specs=[row] * 8,
        out_shape=[jax.ShapeDtypeStruct((B, T, 256), F32)] * 8,
        compiler_params=_cp(("parallel", "parallel")),
        name="rwkv_prep",
    )(prw, prev_rows, *args)


def _wkv_kernel(r_ref, k_ref, b_ref, kk_ref, v_ref, lw_ref, s0_ref, y_ref, sout_ref, S, *, nb):
    L = WKV_CHUNK
    W = RWKV_DIM

    @pl.when(pl.program_id(0) == 0)
    def _():
        S[...] = s0_ref[...]

    ri = lax.broadcasted_iota(jnp.int32, (W, W), 0)
    ci = lax.broadcasted_iota(jnp.int32, (W, W), 1)
    same = (ri >> 6) == (ci >> 6)
    strict = same & ((ci & 63) < (ri & 63))
    incl = same & ((ci & 63) <= (ri & 63))
    eye = (ri == ci).astype(F32)
    tri = (lax.broadcasted_iota(jnp.int32, (L, L), 1) <= lax.broadcasted_iota(jnp.int32, (L, L), 0)).astype(BF16)
    lane_head = lax.broadcasted_iota(jnp.int32, (L, W), 1) >> 6
    nt = (((1,), (1,)), ((), ()))
    tn = (((0,), (0,)), ((), ()))

    def stack(x):
        return jnp.concatenate([jnp.where(lane_head == hd, x, 0.0) for hd in range(RWKV_HEADS)], axis=0)

    def collapse(z):
        return z[0:L] + z[L:2 * L] + z[2 * L:3 * L] + z[3 * L:4 * L]

    def body(bi, carry):
        r = r_ref[bi]
        kf = k_ref[bi]
        bb = b_ref[bi]
        kk = kk_ref[bi]
        v = v_ref[bi]
        ll = lw_ref[bi]
        s_prev = S[bi]
        ll_hi, ll_lo = _split(ll)
        c = (jnp.dot(tri, ll_hi, preferred_element_type=F32)
             + jnp.dot(tri, ll_lo, preferred_element_type=F32))
        c_last = c[L - 1:L, :]
        e_neg = jnp.exp(-c)
        e_end = jnp.exp(c_last - c)
        rt = stack(r * jnp.exp(c))
        kkt = stack(kk * jnp.exp(c - ll))
        bt = stack(bb * e_neg)
        kt = stack(kf * e_neg)
        a_m = jnp.where(strict, _dot(kkt, bt, nt), 0.0)
        b_m = jnp.where(strict, _dot(kkt, kt, nt), 0.0)
        m1 = jnp.where(incl, _dot(rt, bt, nt), 0.0)
        m2 = jnp.where(incl, _dot(rt, kt, nt), 0.0)
        t_m = eye - a_m
        a_p = a_m
        for _ in range(5):
            a_p = _dot(a_p, a_p)
            t_m = t_m + _dot(t_m, a_p)
        vs = stack(v)
        w1 = collapse(_dot(t_m, kkt))
        w2 = collapse(_dot(t_m, _dot(b_m, vs)))
        u = -(_dot(w1, s_prev, nt) + w2)
        y = _dot(collapse(rt), s_prev, nt) + collapse(_dot(m1, stack(u)) + _dot(m2, vs))
        y_ref[bi] = y
        upd = _dot(u, bb * e_end, tn) + _dot(v, kf * e_end, tn)
        S[bi] = s_prev * jnp.exp(c_last) + jnp.where(same, upd, 0.0)
        return carry

    lax.fori_loop(0, nb, body, 0)
    sout_ref[...] = S[...]


def _wkv(r, kf, bb, kk, v, lw, s0):
    B, T, W = r.shape
    L = WKV_CHUNK
    assert T % L == 0
    blk = pl.BlockSpec((B, L, W), lambda c: (0, c, 0))
    st = pl.BlockSpec((B, W, W), lambda c: (0, 0, 0))
    return pl.pallas_call(
        functools.partial(_wkv_kernel, nb=B),
        grid=(T // L,),
        in_specs=[blk] * 6 + [st],
        out_specs=[blk, st],
        out_shape=[jax.ShapeDtypeStruct((B, T, W), F32), jax.ShapeDtypeStruct((B, W, W), F32)],
        scratch_shapes=[pltpu.VMEM((B, W, W), F32)],
        compiler_params=_cp(("arbitrary",)),
        name="wkv",
    )(r, kf, bb, kk, v, lw, s0)


def _conv_kernel(glu_ref, st_ref, dw_ref, dwb_ref, lng_ref, lnb_ref, y_ref, new_ref, ext, *, tm):
    @pl.when(pl.program_id(1) == 0)
    def _():
        ext[0:CONV_HALO, :] = st_ref[0]
    ext[CONV_HALO:CONV_HALO + tm, :] = glu_ref[0]
    off = CONV_HALO - (CONV_WIDTH - 1)
    acc = jnp.zeros((tm, CONV_DIM), F32)
    for w in range(CONV_WIDTH):
        acc = acc + ext[off + w:off + w + tm, :] * dw_ref[w:w + 1, :]
    y = acc + dwb_ref[...]
    mu = jnp.mean(y, axis=-1, keepdims=True)
    var = jnp.mean(jnp.square(y - mu), axis=-1, keepdims=True)
    y = (y - mu) * lax.rsqrt(var + LN_EPS) * lng_ref[...] + lnb_ref[...]
    y_ref[0] = y * _sigmoid(y)
    tail = ext[tm:tm + CONV_HALO, :]
    new_ref[0] = tail
    ext[0:CONV_HALO, :] = tail


def _conv(glu, st_pad, lw, tm):
    B, T, C = glu.shape
    args = (lw['conv_dw'], lw['conv_dw_b'], lw['conv_ln_g'], lw['conv_ln_b'])
    return pl.pallas_call(
        functools.partial(_conv_kernel, tm=tm),
        grid=(B, T // tm),
        in_specs=[pl.BlockSpec((1, tm, C), lambda b, i: (b, i, 0)),
                  pl.BlockSpec((1, CONV_HALO, C), lambda b, i: (b, 0, 0))] + [_full(a) for a in args],
        out_specs=[pl.BlockSpec((1, tm, C), lambda b, i: (b, i, 0)),
                   pl.BlockSpec((1, CONV_HALO, C), lambda b, i: (b, 0, 0))],
        out_shape=[jax.ShapeDtypeStruct((B, T, C), F32), jax.ShapeDtypeStruct((B, CONV_HALO, C), F32)],
        scratch_shapes=[pltpu.VMEM((CONV_HALO + tm, C), F32)],
        compiler_params=_cp(("parallel", "arbitrary")),
        name="conv",
    )(glu, st_pad, *args)


def _merge_kernel(x_ref, mod_ref, g1_ref, wg_ref, oa_ref, y_ref, bonus_ref, g_ref, lng_ref, lnb_ref, ones_ref,
                  pb_ref, yc_ref, pc_ref, wo_ref, g2_ref, wr_ref, br_ref, x1_ref, h2_ref, comb_ref):
    x = x_ref[0]
    mod = mod_ref[0]
    hb = (_rms(x, g1_ref[...]) * (1.0 + mod[1:2]) + mod[0:1]).astype(BF16)
    ones_bd = ones_ref[...]
    y = y_ref[0]
    mu = _head_sum(y, ones_bd) * (1.0 / RWKV_HEAD)
    yc = y - mu
    var = _head_sum(yc * yc, ones_bd) * (1.0 / RWKV_HEAD)
    yb = (yc * lax.rsqrt(var + GN_EPS) * lng_ref[...] + lnb_ref[...] + bonus_ref[0]) * g_ref[0]
    D = D_MODEL
    merged = _sigmoid(_dot(hb, wg_ref[:, 0:D])) * oa_ref[0]
    merged = merged + _sigmoid(_dot(hb, wg_ref[:, D:2 * D])) * _dot(yb, pb_ref[...])
    merged = merged + _sigmoid(_dot(hb, wg_ref[:, 2 * D:3 * D])) * _dot(yc_ref[0], pc_ref[...])
    x1 = x + mod[2:3] * _dot(merged, wo_ref[...])
    x1_ref[0] = x1
    h2 = _rms(x1, g2_ref[...]) * (1.0 + mod[4:5]) + mod[3:4]
    h2_ref[0] = h2.astype(BF16)
    logits = _dot3(h2, wr_ref[...]) + br_ref[...]
    lane = lax.broadcasted_iota(jnp.int32, logits.shape, 1)
    neg = -jnp.inf
    is_g = (lane >= N_EXPERTS) & (lane < N_EXPERTS + N_GROUPS)
    gl = jnp.where(is_g, logits, neg)
    gmax = jnp.max(gl, axis=1, keepdims=True)
    gsum = jnp.sum(jnp.exp(gl - gmax), axis=1, keepdims=True)
    g_val = 1.0 / gsum
    g_idx = jnp.min(jnp.where(is_g & (gl == gmax), lane, 4 * ROUTER_LANES), axis=1, keepdims=True) - N_EXPERTS
    in_grp = (lane >= g_idx * EXPERTS_PER_GROUP) & (lane < (g_idx + 1) * EXPERTS_PER_GROUP)
    el = jnp.where(in_grp, logits, neg)
    v1 = jnp.max(el, axis=1, keepdims=True)
    i1 = jnp.min(jnp.where(in_grp & (el == v1), lane, 4 * ROUTER_LANES), axis=1, keepdims=True)
    el2 = jnp.where(lane == i1, neg, el)
    v2 = jnp.max(el2, axis=1, keepdims=True)
    i2 = jnp.min(jnp.where(in_grp & (lane != i1) & (el2 == v2), lane, 4 * ROUTER_LANES), axis=1, keepdims=True)
    e2 = jnp.exp(v2 - v1)
    w1 = g_val / (1.0 + e2)
    w2 = g_val * e2 / (1.0 + e2)
    comb_ref[0] = jnp.where(lane == i1, w1, 0.0) + jnp.where(lane == i2, w2, 0.0)


def _merge(x, mod, oa, y, bonus, g, yc, lw, tm):
    B, T, D = x.shape
    row = lambda w: pl.BlockSpec((1, tm, w), lambda b, i: (b, i, 0))
    modspec = pl.BlockSpec((1, N_MOD, D), lambda b, i: (b, 0, 0))
    ins = [(x, row(D)), (mod, modspec), (lw['g1'], None), (lw['wg'], None), (oa, row(D)), (y, row(256)),
           (bonus, row(256)), (g, row(256)), (lw['ln_g'], None), (lw['ln_b'], None), (lw['ones_bd'], None),
           (lw['proj_b'], None), (yc, row(256)), (lw['proj_c'], None), (lw['w_out'], None), (lw['g2n'], None),
           (lw['wr'], None), (lw['br'], None)]
    return pl.pallas_call(
        _merge_kernel,
        grid=(B, T // tm),
        in_specs=[s if s is not None else _full(a) for a, s in ins],
        out_specs=[row(D), row(D), row(ROUTER_LANES)],
        out_shape=[jax.ShapeDtypeStruct((B, T, D), F32), jax.ShapeDtypeStruct((B, T, D), BF16),
                   jax.ShapeDtypeStruct((B, T, ROUTER_LANES), F32)],
        compiler_params=_cp(("parallel", "parallel")),
        name="merge",
    )(*[a for a, _ in ins])


def _moe_kernel(h2_ref, comb_ref, x1_ref, gt_ref, wgu_ref, wd_ref, fg_ref, o_ref, acc, *, final):
    e = pl.program_id(2)

    @pl.when(e == 0)
    def _():
        acc[...] = jnp.zeros_like(acc)

    gu = _dot(h2_ref[0], wgu_ref[0])
    gate = gu[:, 0:D_EXPERT]
    up = gu[:, D_EXPERT:2 * D_EXPERT]
    comb = comb_ref[0]
    lane = lax.broadcasted_iota(jnp.int32, comb.shape, 1)
    cw = jnp.sum(jnp.where(lane == e, comb, 0.0), axis=1, keepdims=True)
    act = gate * _sigmoid(gate) * up * cw
    acc[...] += _dot(act, wd_ref[0])

    @pl.when(e == N_EXPERTS - 1)
    def _():
        x2 = x1_ref[0] + gt_ref[0] * acc[...]
        o_ref[0] = _rms(x2, fg_ref[...]) if final else x2


def _moe(h2, comb, x1, gt, lw, final_g, tm, final):
    B, T, D = x1.shape
    gt_rows = gt.shape[1]
    gt_spec = (pl.BlockSpec((1, 1, D), lambda b, i, e: (b, 0, 0)) if gt_rows == 1
               else pl.BlockSpec((1, tm, D), lambda b, i, e: (b, i, 0)))
    row = lambda w: pl.BlockSpec((1, tm, w), lambda b, i, e: (b, i, 0))
    return pl.pallas_call(
        functools.partial(_moe_kernel, final=final),
        grid=(B, T // tm, N_EXPERTS),
        in_specs=[row(D), row(ROUTER_LANES), row(D), gt_spec,
                  pl.BlockSpec((1, D, 2 * D_EXPERT), lambda b, i, e: (e, 0, 0)),
                  pl.BlockSpec((1, D_EXPERT, D), lambda b, i, e: (e, 0, 0)),
                  pl.BlockSpec((1, D), lambda b, i, e: (0, 0))],
        out_specs=row(D),
        out_shape=jax.ShapeDtypeStruct((B, T, D), F32),
        scratch_shapes=[pltpu.VMEM((tm, D), F32)],
        compiler_params=_cp(("parallel", "parallel", "arbitrary")),
        name="moe",
    )(h2, comb, x1, gt, lw['wgu'], lw['wd'], final_g)


def _layer_weights(l, w):
    D = D_MODEL
    w_in = w['w_in'][l]
    c_q, c_kv, c_kr = Q_LORA, Q_LORA + KV_LORA, MLA_IN
    c_rw, c_cv = MLA_IN + RWKV_IN, MLA_IN + RWKV_IN + 2 * CONV_DIM
    half = ROPE_DIM // 2
    swap = np.concatenate([np.arange(half, ROPE_DIM), np.arange(half)])
    wkr = w_in[:, c_kv:c_kr]
    q_up = w['q_up'][l].reshape(Q_LORA, MLA_HEADS, NOPE_DIM + ROPE_DIM)
    q_rope = q_up[:, :, NOPE_DIM:]
    wqu = jnp.concatenate([q_up[:, :, :NOPE_DIM].reshape(Q_LORA, -1), q_rope.reshape(Q_LORA, -1),
                           q_rope[:, :, swap].reshape(Q_LORA, -1)], axis=1)
    uk_t = jnp.transpose(w['w_uk'][l], (1, 2, 0))
    uv_t = jnp.transpose(w['w_uv'][l], (1, 0, 2))
    wukp = jnp.zeros((MLA_HEADS, 2 * NOPE_DIM, KV_LORA), F32)
    wuvp = jnp.zeros((MLA_HEADS, KV_LORA, 2 * V_HEAD), F32)
    for hd in range(MLA_HEADS):
        o = (hd % 2) * NOPE_DIM
        wukp = wukp.at[hd, o:o + NOPE_DIM, :].set(uk_t[hd])
        wuvp = wuvp.at[hd, :, o:o + V_HEAD].set(uv_t[hd])
    zl = jnp.zeros((W_LORA, RWKV_DIM), F32)
    head = np.arange(RWKV_DIM) // RWKV_HEAD
    wr = jnp.zeros((D, ROUTER_LANES), F32)
    wr = wr.at[:, 0:N_EXPERTS].set(w['router_expert_w'][l]).at[:, N_EXPERTS:N_EXPERTS + N_GROUPS].set(
        w['router_group_w'][l])
    br = jnp.zeros((1, ROUTER_LANES), F32)
    br = br.at[0, 0:N_EXPERTS].set(w['router_expert_b'][l]).at[0, N_EXPERTS:N_EXPERTS + N_GROUPS].set(
        w['router_group_b'][l])
    r2 = lambda a: a.reshape(1, -1)
    return dict(
        g1=r2(w['norm1_g'][l]), g2n=r2(w['norm2_g'][l]),
        wq=w_in[:, :c_q].astype(BF16), wkv=w_in[:, c_q:c_kv].astype(BF16),
        wkr=jnp.concatenate([jnp.tile(wkr, (1, 8)), jnp.tile(wkr[:, swap], (1, 8))], axis=1).astype(BF16),
        wrw=w_in[:, c_kr:c_rw].astype(BF16), wcv=w_in[:, c_rw:c_cv].astype(BF16), wg=w_in[:, c_cv:].astype(BF16),
        qg=r2(w['q_norm_g'][l]), wqu=wqu.astype(BF16), wukp=wukp.astype(BF16), kvg=r2(w['kv_norm_g'][l]),
        wuvp=wuvp.astype(BF16), proj_a=w['proj_a'][l].astype(BF16),
        mu=r2(w['rwkv_mu'][l]), w0=r2(w['rwkv_w0'][l]), a0=r2(w['rwkv_a0'][l]),
        w2p=jnp.concatenate([w['rwkv_w2'][l], zl], axis=0).astype(BF16),
        a2p=jnp.concatenate([zl, w['rwkv_a2'][l]], axis=0).astype(BF16),
        g2=w['rwkv_g2'][l].astype(BF16), k_k=r2(w['rwkv_k_k'][l]), k_a=r2(w['rwkv_k_a'][l]),
        r_k=r2(w['rwkv_r_k'][l]), ln_g=r2(w['rwkv_ln_g'][l]), ln_b=r2(w['rwkv_ln_b'][l]),
        ones_bd=jnp.asarray((head[:, None] == head[None, :]).astype(np.float32)).astype(BF16),
        proj_b=w['proj_b'][l].astype(BF16),
        conv_dw=w['conv_dw'][l], conv_dw_b=r2(w['conv_dw_b'][l]), conv_ln_g=r2(w['conv_ln_g'][l]),
        conv_ln_b=r2(w['conv_ln_b'][l]), proj_c=w['proj_c'][l].astype(BF16),
        w_out=w['w_out'][l].astype(BF16), wr=wr, br=br,
        wgu=jnp.concatenate([w['moe_w_gate'][l], w['moe_w_up'][l]], axis=2).astype(BF16),
        wd=w['moe_w_down'][l].astype(BF16),
    )


def _rope_tables(past, T):
    half = ROPE_DIM // 2
    inv = ROPE_THETA ** (-jnp.arange(half, dtype=F32) / half)
    ang = (past + jnp.arange(T)).astype(F32)[:, None] * inv[None, :]
    cos, sin = jnp.cos(ang), jnp.sin(ang)
    return (jnp.tile(jnp.concatenate([cos, cos], axis=1), (1, 8)),
            jnp.tile(jnp.concatenate([-sin, sin], axis=1), (1, 8)))


def _row_tile(T, cap):
    return min(T, cap)


def _trunk_layer(x, mod, cache_lat, cache_rope, shift_st, wkv_st, conv_st, lw, final_g, final):
    B, T, D = x.shape
    past = 0 if cache_lat is None else cache_lat.shape[1]
    cos8, sin8 = _rope_tables(past, T)
    tm = _row_tile(T, 256)
    lat, kr, k512, qlat, qrope, prw, glu = _in_proj(x, mod, lw, cos8, sin8, tm)

    _, _, s_pad, n_keys = _attn_tiles(T, past)
    keys = k512
    if past:
        old = jnp.concatenate([cache_lat, jnp.tile(cache_rope, (1, 1, 8))], axis=2).astype(BF16)
        keys = jnp.concatenate([old, k512], axis=1)
    if s_pad != n_keys:
        keys = jnp.pad(keys, ((0, 0), (0, s_pad - n_keys), (0, 0)))
    o_a = _attention(qlat, qrope, keys, lw['wuvp'], lw['proj_a'], past)

    nT = T // tm
    first = jnp.zeros((B, 1, RWKV_IN), F32) if shift_st is None else shift_st[:, None, :]
    prev_rows = jnp.concatenate([first, prw[:, tm - 1:T - 1:tm, :]], axis=1).reshape(B, nT, 1, RWKV_IN)
    r, kf, bb, kk, v, lwd, g, bonus = _rwkv_prep(prw, prev_rows, lw, tm)
    Tp = -(-T // WKV_CHUNK) * WKV_CHUNK
    scan_in = [r, kf, bb, kk, v, lwd]
    if Tp != T:
        scan_in = [jnp.pad(a, ((0, 0), (0, Tp - T), (0, 0))) for a in scan_in]
    W = RWKV_DIM
    s0 = jnp.zeros((B, W, W), F32)
    if wkv_st is not None:
        for hd in range(RWKV_HEADS):
            o = hd * RWKV_HEAD
            s0 = s0.at[:, o:o + RWKV_HEAD, o:o + RWKV_HEAD].set(wkv_st[:, hd])
    y, s_bd = _wkv(*scan_in, s0)
    y = y[:, :T]
    wkv_new = jnp.stack([s_bd[:, hd * RWKV_HEAD:(hd + 1) * RWKV_HEAD, hd * RWKV_HEAD:(hd + 1) * RWKV_HEAD]
                         for hd in range(RWKV_HEADS)], axis=1)

    pad_rows = CONV_HALO - (CONV_WIDTH - 1)
    st = jnp.zeros((B, CONV_WIDTH - 1, CONV_DIM), F32) if conv_st is None else conv_st
    yc, conv_tail = _conv(glu, jnp.pad(st, ((0, 0), (pad_rows, 0), (0, 0))), lw, tm)
    conv_new = conv_tail[:, pad_rows:]

    x1, h2, comb = _merge(x, mod, o_a, y, bonus, g, yc, lw, tm)
    if T >= 1024:
        x2 = _moe(h2, comb, x1, mod[:, 5:6], lw, final_g, 1024, final)
    else:
        gt = jnp.broadcast_to(mod[:, 5:6], (B, T, D)).reshape(1, B * T, D)
        x2 = _moe(h2.reshape(1, B * T, D), comb.reshape(1, B * T, ROUTER_LANES), x1.reshape(1, B * T, D), gt, lw,
                  final_g, B * T, final).reshape(B, T, D)
    return x2, (lat, kr, prw[:, -1], wkv_new, conv_new)


def kernel(x_prompt, x_sample, c_prompt, c_sample, cache_kv_latent, cache_k_rope, state_rwkv_shift, state_rwkv_wkv, state_conv, ada_w, ada_b, norm1_g, norm2_g, w_in, q_norm_g, q_up, kv_norm_g, w_uk, w_uv, proj_a, rwkv_mu, rwkv_w0, rwkv_w2, rwkv_a0, rwkv_a2, rwkv_g2, rwkv_k_k, rwkv_k_a, rwkv_r_k, rwkv_ln_g, rwkv_ln_b, proj_b, conv_dw, conv_dw_b, conv_ln_g, conv_ln_b, proj_c, w_out, router_group_w, router_group_b, router_expert_w, router_expert_b, moe_w_gate, moe_w_up, moe_w_down, final_g):
    w = dict(norm1_g=norm1_g, norm2_g=norm2_g, w_in=w_in, q_norm_g=q_norm_g, q_up=q_up, kv_norm_g=kv_norm_g,
             w_uk=w_uk, w_uv=w_uv, proj_a=proj_a, rwkv_mu=rwkv_mu, rwkv_w0=rwkv_w0, rwkv_w2=rwkv_w2,
             rwkv_a0=rwkv_a0, rwkv_a2=rwkv_a2, rwkv_g2=rwkv_g2, rwkv_k_k=rwkv_k_k, rwkv_k_a=rwkv_k_a,
             rwkv_r_k=rwkv_r_k.reshape(rwkv_r_k.shape[0], -1), rwkv_ln_g=rwkv_ln_g, rwkv_ln_b=rwkv_ln_b,
             proj_b=proj_b, conv_dw=conv_dw, conv_dw_b=conv_dw_b, conv_ln_g=conv_ln_g, conv_ln_b=conv_ln_b,
             proj_c=proj_c, w_out=w_out, router_group_w=router_group_w, router_group_b=router_group_b,
             router_expert_w=router_expert_w, router_expert_b=router_expert_b, moe_w_gate=moe_w_gate,
             moe_w_up=moe_w_up, moe_w_down=moe_w_down)
    depth = ada_w.shape[0]
    bp = x_prompt.shape[0]
    D = D_MODEL
    mod_all = _ada_mod(jnp.concatenate([c_prompt, c_sample], axis=0), ada_w, ada_b)
    fg = final_g.reshape(1, D)
    hp, hs = x_prompt, x_sample
    new_p, new_s = [], []
    for l in range(depth):
        lw = _layer_weights(l, w)
        mod_p = mod_all[l, :bp].reshape(bp, N_MOD, D)
        mod_s = mod_all[l, bp:].reshape(-1, N_MOD, D)
        final = l == depth - 1
        hp, st_p = _trunk_layer(hp, mod_p, None, None, None, None, None, lw, fg, final)
        hs, st_s = _trunk_layer(hs, mod_s, cache_kv_latent[l], cache_k_rope[l], state_rwkv_shift[l],
                                state_rwkv_wkv[l], state_conv[l], lw, fg, final)
        new_p.append(st_p)
        new_s.append(st_s)
    stack = lambda states, i: jnp.stack([s[i] for s in states], axis=0)
    return ((hp, hs) + tuple(stack(new_p, i) for i in range(5)) + tuple(stack(new_s, i) for i in range(5)))
```

```python
import functools

import numpy as np
import jax
import jax.numpy as jnp
from jax import lax
from jax.experimental import pallas as pl
from jax.experimental.pallas import tpu as pltpu

F32 = jnp.float32
BF16 = jnp.bfloat16

D_MODEL = 1024
CHUNK = 64
NORM_EPS = 1e-6
MLA_HEADS = 8
Q_LORA = 384
KV_LORA = 256
NOPE_DIM = 64
ROPE_DIM = 32
V_HEAD = 64
ROPE_THETA = 10000.0
SM_SCALE = (NOPE_DIM + ROPE_DIM) ** -0.5
RWKV_HEADS = 4
RWKV_HEAD = 64
RWKV_DIM = RWKV_HEADS * RWKV_HEAD
W_LORA = 64
A_LORA = 64
G_LORA = 128
RWKV_IN = 3 * RWKV_DIM + W_LORA + A_LORA + G_LORA
GN_EPS = 64e-5
CONV_DIM = 256
CONV_WIDTH = 31
LN_EPS = 1e-5
MLA_IN = Q_LORA + KV_LORA + ROPE_DIM
N_MOD = 6
N_GROUPS = 4
EXPERTS_PER_GROUP = 4
N_EXPERTS = 16
D_EXPERT = 256
WKV_CHUNK = 64
CONV_HALO = 32
ROUTER_LANES = 128
PAIR_KEYS = 4 * 256
VMEM_LIMIT = 56 * 1024 * 1024


def _cp(sem):
    return pltpu.CompilerParams(dimension_semantics=sem, vmem_limit_bytes=VMEM_LIMIT)


def _full(a):
    nd = a.ndim
    return pl.BlockSpec(a.shape, lambda *_: (0,) * nd)


def _dot(a, b, dims=(((1,), (0,)), ((), ()))):
    return lax.dot_general(a.astype(BF16), b.astype(BF16), dims, preferred_element_type=F32)


def _split(a):
    hi = a.astype(BF16)
    lo = (a - hi.astype(F32)).astype(BF16)
    return hi, lo


def _dot_hl(a, b, dims=(((1,), (0,)), ((), ()))):
    hi, lo = _split(a)
    bb = b.astype(BF16)
    return (lax.dot_general(hi, bb, dims, preferred_element_type=F32)
            + lax.dot_general(lo, bb, dims, preferred_element_type=F32))


def _dot3(a, b, dims=(((1,), (0,)), ((), ()))):
    ah, al = _split(a)
    bh, bl = _split(b)
    d = lambda x, y: lax.dot_general(x, y, dims, preferred_element_type=F32)
    return d(ah, bh) + d(ah, bl) + d(al, bh)


def _sigmoid(x):
    return 1.0 / (1.0 + jnp.exp(-x))


def _rms(x, g):
    return x * lax.rsqrt(jnp.mean(x * x, axis=-1, keepdims=True) + NORM_EPS) * g


def _ada_kernel(c_ref, w_ref, b_ref, o_ref):
    c = c_ref[...]
    o_ref[0] = _dot(c * _sigmoid(c), w_ref[0]) + b_ref[0]


def _ada_mod(c_all, ada_w, ada_b):
    L, D, C = ada_w.shape
    R = c_all.shape[0]
    nb = C // D
    return pl.pallas_call(
        _ada_kernel,
        grid=(L, nb),
        in_specs=[pl.BlockSpec((R, D), lambda l, j: (0, 0)),
                  pl.BlockSpec((1, D, D), lambda l, j: (l, 0, j)),
                  pl.BlockSpec((1, 1, D), lambda l, j: (l, 0, j))],
        out_specs=pl.BlockSpec((1, R, D), lambda l, j: (l, 0, j)),
        out_shape=jax.ShapeDtypeStruct((L, R, C), F32),
        compiler_params=_cp(("parallel", "parallel")),
        name="ada_mod",
    )(c_all, ada_w, ada_b.reshape(L, 1, C))


_NT = (((1,), (1,)), ((), ()))


def _pair_keys(lat, rope8, wukq):
    lane = lax.broadcasted_iota(jnp.int32, rope8.shape, 1) & 127
    rope_slot = jnp.where((lane >= NOPE_DIM) & (lane < NOPE_DIM + ROPE_DIM), rope8, 0.0)
    return (_dot(lat, wukq) + jnp.concatenate([rope_slot] * (MLA_HEADS // 2), axis=1)).astype(BF16)


def _in_proj_kernel(x_ref, mod_ref, g1_ref, wq_ref, wkv_ref, wkvt_ref, wkr_ref, wrw_ref, wcv_ref, qg_ref, wqnt_ref,
                    wqrt_ref, wqrst_ref, wukq_ref, kvg_ref, kvgc_ref, cos_ref, sin_ref, cost_ref, sint_ref,
                    lat_ref, kr_ref, kq_ref, latt_ref, qt_ref, prw_ref, glu_ref):
    x = x_ref[0]
    mod = mod_ref[0]
    tm = x.shape[0]
    h = _rms(x, g1_ref[...]) * (1.0 + mod[1:2]) + mod[0:1]
    hb = h.astype(BF16)
    qn = _rms(_dot(hb, wq_ref[...]), qg_ref[...]).astype(BF16)
    qnt = _dot(wqnt_ref[...], qn, _NT)
    qrt = _dot(wqrt_ref[...], qn, _NT)
    qrst = _dot(wqrst_ref[...], qn, _NT)
    qrope = qrt * cost_ref[...] + qrst * sint_ref[...]
    zpad = jnp.zeros((128 - NOPE_DIM - ROPE_DIM, tm), F32)
    for hd in range(MLA_HEADS):
        qh = jnp.concatenate([qnt[NOPE_DIM * hd:NOPE_DIM * (hd + 1)], qrope[ROPE_DIM * hd:ROPE_DIM * (hd + 1)],
                              zpad], axis=0)
        qt_ref[0, hd] = (qh * SM_SCALE).astype(BF16)
    lat = _rms(_dot(hb, wkv_ref[...]), kvg_ref[...])
    lat_ref[0] = lat
    pkvt = _dot(wkvt_ref[...], hb, _NT)
    latt = pkvt * lax.rsqrt(jnp.mean(pkvt * pkvt, axis=0, keepdims=True) + NORM_EPS) * kvgc_ref[...]
    latt_ref[0] = latt.astype(BF16)
    pkr = _dot(hb, wkr_ref[...])
    krt = pkr[:, 0:256] * cos_ref[...] + pkr[:, 256:512] * sin_ref[...]
    kr_ref[0] = krt[:, 0:ROPE_DIM]
    kq_ref[0] = _pair_keys(lat, krt, wukq_ref[...])
    prw_ref[0] = _dot(hb, wrw_ref[...])
    pcv = _dot(hb, wcv_ref[...])
    glu_ref[0] = pcv[:, 0:CONV_DIM] * _sigmoid(pcv[:, CONV_DIM:2 * CONV_DIM])


def _in_proj(x, mod, lw, tabs, tm):
    B, T, D = x.shape
    nT = T // tm
    cos8, sin8, cos8t, sin8t = tabs
    row = lambda w: pl.BlockSpec((1, tm, w), lambda b, i: (b, i, 0))
    col = pl.BlockSpec((1, 256, tm), lambda b, i: (b, 0, i))
    args = (lw['g1'], lw['wq'], lw['wkv'], lw['wkvt'], lw['wkr'], lw['wrw'], lw['wcv'], lw['qg'], lw['wqnt'],
            lw['wqrt'], lw['wqrst'], lw['wukq'], lw['kvg'], lw['kvgc'])
    return pl.pallas_call(
        _in_proj_kernel,
        grid=(B, nT),
        in_specs=[row(D), pl.BlockSpec((1, N_MOD, D), lambda b, i: (b, 0, 0))] + [_full(a) for a in args]
                 + [pl.BlockSpec((tm, 256), lambda b, i: (i, 0))] * 2
                 + [pl.BlockSpec((256, tm), lambda b, i: (0, i))] * 2,
        out_specs=[row(KV_LORA), row(ROPE_DIM), row(PAIR_KEYS), col,
                   pl.BlockSpec((1, MLA_HEADS, 128, tm), lambda b, i: (b, 0, 0, i)),
                   row(RWKV_IN), row(CONV_DIM)],
        out_shape=[jax.ShapeDtypeStruct((B, T, KV_LORA), F32),
                   jax.ShapeDtypeStruct((B, T, ROPE_DIM), F32),
                   jax.ShapeDtypeStruct((B, T, PAIR_KEYS), BF16),
                   jax.ShapeDtypeStruct((B, KV_LORA, T), BF16),
                   jax.ShapeDtypeStruct((B, MLA_HEADS, 128, T), BF16),
                   jax.ShapeDtypeStruct((B, T, RWKV_IN), F32),
                   jax.ShapeDtypeStruct((B, T, CONV_DIM), F32)],
        compiler_params=_cp(("parallel", "parallel")),
        name="in_proj",
    )(x, mod, *args, cos8, sin8, cos8t, sin8t)


def _key_up_kernel(lat_ref, rope8_ref, wukq_ref, kq_ref):
    kq_ref[0] = _pair_keys(lat_ref[0], rope8_ref[0], wukq_ref[...])


def _key_up(lat, rope8, wukq, tm):
    B, S, _ = lat.shape
    row = lambda w: pl.BlockSpec((1, tm, w), lambda b, i: (b, i, 0))
    return pl.pallas_call(
        _key_up_kernel,
        grid=(B, S // tm),
        in_specs=[row(KV_LORA), row(256), _full(wukq)],
        out_specs=row(PAIR_KEYS),
        out_shape=jax.ShapeDtypeStruct((B, S, PAIR_KEYS), BF16),
        compiler_params=_cp(("parallel", "parallel")),
        name="key_up",
    )(lat, rope8, wukq)


def _attn_kernel(qi_ref, ki_ref, last_ref, qt_ref, k_ref, vt_ref, wuvt_ref, pat_ref, o_ref,
                 qcat, m_s, l_s, acc, *, tq, tk, past, n_keys):
    p = pl.program_id(1)
    qi = qi_ref[p]
    ki = ki_ref[p]
    H = MLA_HEADS
    NP = H // 2

    @pl.when(ki == 0)
    def _():
        qcat[...] = jnp.zeros(qcat.shape, BF16)
        for pr in range(NP):
            qcat[pr, 0:128, 0:tq] = qt_ref[0, 2 * pr]
            qcat[pr, 128:256, tq:2 * tq] = qt_ref[0, 2 * pr + 1]
        m_s[...] = jnp.full(m_s.shape, -jnp.inf, F32)
        l_s[...] = jnp.zeros(l_s.shape, F32)
        acc[...] = jnp.zeros(acc.shape, F32)

    def step(bias):
        vt = vt_ref[0]
        scores = [jnp.dot(k_ref[0, :, 256 * pr:256 * (pr + 1)], qcat[pr], preferred_element_type=F32)
                  for pr in range(NP)]
        for pr in range(NP):
            s = scores[pr] if bias is None else scores[pr] + bias
            m_prev = m_s[pr]
            m_new = jnp.maximum(m_prev, jnp.max(s, axis=0, keepdims=True))
            alpha = jnp.exp(m_prev - m_new)
            pe = jnp.exp(s - m_new)
            l_s[pr] = alpha * l_s[pr] + jnp.sum(pe, axis=0, keepdims=True)
            acc[pr] = alpha * acc[pr] + jnp.dot(vt, pe.astype(BF16), preferred_element_type=F32)
            m_s[pr] = m_new

    is_last = last_ref[p] == 1

    @pl.when(jnp.logical_not(is_last))
    def _():
        step(None)

    @pl.when(is_last)
    def _():
        t = lax.broadcasted_iota(jnp.int32, (1, tq), 1)
        limit = jnp.minimum((((past + qi * tq + t) >> 6) + 1) << 6, n_keys)
        kpos = ki * tk + lax.broadcasted_iota(jnp.int32, (tk, 1), 0)
        bias = jnp.where(kpos < limit, 0.0, -jnp.inf)
        step(jnp.concatenate([bias, bias], axis=1))
        heads = []
        for pr in range(NP):
            o = acc[pr] / l_s[pr]
            heads += [_dot(wuvt_ref[2 * pr + j], o[:, j * tq:(j + 1) * tq]) for j in range(2)]
        out_t = _dot(pat_ref[...], jnp.concatenate(heads, axis=0))
        o_ref[0] = out_t.T


def _attn_tiles(T, past):
    n_keys = past + T
    tq = 128
    t_pad = -(-T // tq) * tq
    tk = -(-n_keys // 128) * 128 if n_keys <= 2304 else 512
    s_pad = -(-n_keys // tk) * tk
    return tq, t_pad, tk, s_pad, n_keys


def _attention(qt, keys, latt, wuvt, proj_at, past, T):
    B, H = qt.shape[:2]
    tq, t_pad, tk, s_pad, n_keys = _attn_tiles(T, past)
    assert keys.shape[1] == s_pad and latt.shape[2] == s_pad and qt.shape[3] == t_pad and past % CHUNK == 0
    qi, ki, last = [], [], []
    for i in range(t_pad // tq):
        lim = min(((past + (i + 1) * tq - 1) // CHUNK + 1) * CHUNK, n_keys)
        nk = -(-lim // tk)
        for j in range(nk):
            qi.append(i), ki.append(j), last.append(int(j == nk - 1))
    npairs = len(qi)
    tabs = [jnp.asarray(np.asarray(a, np.int32)) for a in (qi, ki, last)]
    grid_spec = pltpu.PrefetchScalarGridSpec(
        num_scalar_prefetch=3,
        grid=(B, npairs),
        in_specs=[pl.BlockSpec((1, H, 128, tq), lambda b, p, qi, ki, la: (b, 0, 0, qi[p])),
                  pl.BlockSpec((1, tk, PAIR_KEYS), lambda b, p, qi, ki, la: (b, ki[p], 0)),
                  pl.BlockSpec((1, KV_LORA, tk), lambda b, p, qi, ki, la: (b, 0, ki[p])),
                  pl.BlockSpec(wuvt.shape, lambda b, p, qi, ki, la: (0, 0, 0)),
                  pl.BlockSpec(proj_at.shape, lambda b, p, qi, ki, la: (0, 0))],
        out_specs=pl.BlockSpec((1, tq, D_MODEL), lambda b, p, qi, ki, la: (b, qi[p], 0)),
        scratch_shapes=[pltpu.VMEM((H // 2, 256, 2 * tq), BF16), pltpu.VMEM((H // 2, 1, 2 * tq), F32),
                        pltpu.VMEM((H // 2, 1, 2 * tq), F32), pltpu.VMEM((H // 2, KV_LORA, 2 * tq), F32)])
    return pl.pallas_call(
        functools.partial(_attn_kernel, tq=tq, tk=tk, past=past, n_keys=n_keys),
        grid_spec=grid_spec,
        out_shape=jax.ShapeDtypeStruct((B, t_pad, D_MODEL), F32),
        compiler_params=_cp(("parallel", "arbitrary")),
        name="attention",
    )(*tabs, qt, keys, latt, wuvt, proj_at)


def _head_sum(x, ones_bd):
    return _dot_hl(x, ones_bd)


def _rwkv_prep_kernel(prw_ref, prev_ref, mu_ref, w0_ref, w2_ref, a0_ref, a2_ref, g2_ref, kk_ref, ka_ref, rk_ref,
                      ones_ref, r_ref, k_ref, b_ref, kkn_ref, v_ref, lw_ref, g_ref, bonus_ref, *, tm):
    pr = prw_ref[0]
    row = lax.broadcasted_iota(jnp.int32, (tm, 1), 0)
    prev = jnp.where(row == 0, prev_ref[0, 0], pltpu.roll(pr, 1, 0))
    xs = pr + (prev - pr) * mu_ref[...]
    r = xs[:, 0:256]
    k = xs[:, 256:512]
    v = xs[:, 512:768]
    wa = xs[:, 768:896]
    gl = xs[:, 896:1024]
    z = w0_ref[...] + _dot(jnp.tanh(wa), w2_ref[...])
    nz = -z
    softplus = jnp.maximum(nz, 0.0) + jnp.log(1.0 + jnp.exp(-jnp.abs(nz)))
    lw_ref[0] = -jnp.exp(-softplus - 0.5)
    a = _sigmoid(a0_ref[...] + _dot(wa, a2_ref[...]))
    g_ref[0] = _dot(_sigmoid(gl), g2_ref[...])
    ones_bd = ones_ref[...]
    kk = k * kk_ref[...]
    kk = kk * lax.rsqrt(_head_sum(kk * kk, ones_bd) + 1e-12)
    kf = k * (1.0 + (a - 1.0) * ka_ref[...])
    r_ref[0] = r
    k_ref[0] = kf
    v_ref[0] = v
    kkn_ref[0] = kk
    b_ref[0] = kk * a
    bonus_ref[0] = _head_sum(r * kf * rk_ref[...], ones_bd) * v


def _rwkv_prep(prw, prev_rows, lw, tm):
    B, T, _ = prw.shape
    nT = T // tm
    args = (lw['mu'], lw['w0'], lw['w2p'], lw['a0'], lw['a2p'], lw['g2'], lw['k_k'], lw['k_a'], lw['r_k'],
            lw['ones_bd'])
    row = pl.BlockSpec((1, tm, 256), lambda b, i: (b, i, 0))
    return pl.pallas_call(
        functools.partial(_rwkv_prep_kernel, tm=tm),
        grid=(B, nT),
        in_specs=[pl.BlockSpec((1, tm, RWKV_IN), lambda b, i: (b, i, 0)),
                  pl.BlockSpec((1, 1, 1, RWKV_IN), lambda b, i: (b, i, 0, 0))] + [_full(a) for a in args],
        out_specs=[row] * 8,
        out_shape=[jax.ShapeDtypeStruct((B, T, 256), F32)] * 8,
        compiler_params=_cp(("parallel", "parallel")),
        name="rwkv_prep",
    )(prw, prev_rows, *args)


def _wkv_kernel(r_ref, k_ref, b_ref, kk_ref, v_ref, lw_ref, s0_ref, y_ref, sout_ref, S, *, nb, G):
    L = WKV_CHUNK
    W = RWKV_DIM

    @pl.when(pl.program_id(0) == 0)
    def _():
        S[...] = s0_ref[...]

    ri = lax.broadcasted_iota(jnp.int32, (W, W), 0)
    ci = lax.broadcasted_iota(jnp.int32, (W, W), 1)
    same = (ri >> 6) == (ci >> 6)
    strict = same & ((ci & 63) < (ri & 63))
    incl = same & ((ci & 63) <= (ri & 63))
    eye = (ri == ci).astype(F32)
    tri = (lax.broadcasted_iota(jnp.int32, (L, L), 1) <= lax.broadcasted_iota(jnp.int32, (L, L), 0)).astype(BF16)
    lane_head = lax.broadcasted_iota(jnp.int32, (L, W), 1) >> 6
    nt = (((1,), (1,)), ((), ()))
    tn = (((0,), (0,)), ((), ()))

    def stack(x):
        return jnp.concatenate([jnp.where(lane_head == hd, x, 0.0) for hd in range(RWKV_HEADS)], axis=0)

    def collapse(z):
        return z[0:L] + z[L:2 * L] + z[2 * L:3 * L] + z[3 * L:4 * L]

    def body(it, carry):
        ids = [it * G + g for g in range(G)]
        each = lambda f, *xs: [f(*a) for a in zip(*xs)]
        r, kf, bb, kk, v, ll, s_prev = [[ref[i] for i in ids] for ref in
                                        (r_ref, k_ref, b_ref, kk_ref, v_ref, lw_ref, S)]
        split = each(_split, ll)
        c = each(lambda hl: jnp.dot(tri, hl[0], preferred_element_type=F32)
                 + jnp.dot(tri, hl[1], preferred_element_type=F32), split)
        c_last = each(lambda x: x[L - 1:L, :], c)
        e_neg = each(lambda x: jnp.exp(-x), c)
        e_end = each(lambda x, y: jnp.exp(y - x), c, c_last)
        rt = each(lambda x, y: stack(x * jnp.exp(y)), r, c)
        kkt = each(lambda x, y, z: stack(x * jnp.exp(y - z)), kk, c, ll)
        bt = each(lambda x, y: stack(x * y), bb, e_neg)
        kt = each(lambda x, y: stack(x * y), kf, e_neg)
        a_m = each(lambda x, y: jnp.where(strict, _dot(x, y, nt), 0.0), kkt, bt)
        b_m = each(lambda x, y: jnp.where(strict, _dot(x, y, nt), 0.0), kkt, kt)
        m1 = each(lambda x, y: jnp.where(incl, _dot(x, y, nt), 0.0), rt, bt)
        m2 = each(lambda x, y: jnp.where(incl, _dot(x, y, nt), 0.0), rt, kt)
        t_m = each(lambda x: eye - x, a_m)
        a_p = a_m
        for _ in range(5):
            a_p = each(lambda x: _dot(x, x), a_p)
            t_m = each(lambda x, y: x + _dot(x, y), t_m, a_p)
        vs = each(stack, v)
        w1 = each(lambda x, y: collapse(_dot(x, y)), t_m, kkt)
        bv = each(_dot, b_m, vs)
        w2 = each(lambda x, y: collapse(_dot(x, y)), t_m, bv)
        u = each(lambda x, y, z: -(_dot(x, y, nt) + z), w1, s_prev, w2)
        y0 = each(lambda x, y: _dot(collapse(x), y, nt), rt, s_prev)
        y1 = each(lambda a, b_, c_, d: collapse(_dot(a, stack(b_)) + _dot(c_, d)), m1, u, m2, vs)
        upd = each(lambda a, b_, c_, d, e: _dot(a, b_ * e, tn) + _dot(c_, d * e, tn), u, bb, v, kf, e_end)
        for g, i in enumerate(ids):
            y_ref[i] = y0[g] + y1[g]
            S[i] = s_prev[g] * jnp.exp(c_last[g]) + jnp.where(same, upd[g], 0.0)
        return carry

    lax.fori_loop(0, nb // G, body, 0)
    sout_ref[...] = S[...]


def _wkv(r, kf, bb, kk, v, lw, s0):
    B, T, W = r.shape
    L = WKV_CHUNK
    assert T % L == 0
    blk = pl.BlockSpec((B, L, W), lambda c: (0, c, 0))
    st = pl.BlockSpec((B, W, W), lambda c: (0, 0, 0))
    return pl.pallas_call(
        functools.partial(_wkv_kernel, nb=B, G=4 if B % 4 == 0 else 1),
        grid=(T // L,),
        in_specs=[blk] * 6 + [st],
        out_specs=[blk, st],
        out_shape=[jax.ShapeDtypeStruct((B, T, W), F32), jax.ShapeDtypeStruct((B, W, W), F32)],
        scratch_shapes=[pltpu.VMEM((B, W, W), F32)],
        compiler_params=_cp(("arbitrary",)),
        name="wkv",
    )(r, kf, bb, kk, v, lw, s0)


def _conv_kernel(glu_ref, st_ref, dw_ref, dwb_ref, lng_ref, lnb_ref, y_ref, new_ref, ext, *, tm):
    @pl.when(pl.program_id(1) == 0)
    def _():
        ext[0:CONV_HALO, :] = st_ref[0]
    ext[CONV_HALO:CONV_HALO + tm, :] = glu_ref[0]
    off = CONV_HALO - (CONV_WIDTH - 1)
    acc = jnp.zeros((tm, CONV_DIM), F32)
    for w in range(CONV_WIDTH):
        acc = acc + ext[off + w:off + w + tm, :] * dw_ref[w:w + 1, :]
    y = acc + dwb_ref[...]
    mu = jnp.mean(y, axis=-1, keepdims=True)
    var = jnp.mean(jnp.square(y - mu), axis=-1, keepdims=True)
    y = (y - mu) * lax.rsqrt(var + LN_EPS) * lng_ref[...] + lnb_ref[...]
    y_ref[0] = y * _sigmoid(y)
    tail = ext[tm:tm + CONV_HALO, :]
    new_ref[0] = tail
    ext[0:CONV_HALO, :] = tail


def _conv(glu, st_pad, lw, tm):
    B, T, C = glu.shape
    args = (lw['conv_dw'], lw['conv_dw_b'], lw['conv_ln_g'], lw['conv_ln_b'])
    return pl.pallas_call(
        functools.partial(_conv_kernel, tm=tm),
        grid=(B, T // tm),
        in_specs=[pl.BlockSpec((1, tm, C), lambda b, i: (b, i, 0)),
                  pl.BlockSpec((1, CONV_HALO, C), lambda b, i: (b, 0, 0))] + [_full(a) for a in args],
        out_specs=[pl.BlockSpec((1, tm, C), lambda b, i: (b, i, 0)),
                   pl.BlockSpec((1, CONV_HALO, C), lambda b, i: (b, 0, 0))],
        out_shape=[jax.ShapeDtypeStruct((B, T, C), F32), jax.ShapeDtypeStruct((B, CONV_HALO, C), F32)],
        scratch_shapes=[pltpu.VMEM((CONV_HALO + tm, C), F32)],
        compiler_params=_cp(("parallel", "arbitrary")),
        name="conv",
    )(glu, st_pad, *args)


def _merge_kernel(x_ref, mod_ref, g1_ref, wg_ref, oa_ref, y_ref, bonus_ref, g_ref, lng_ref, lnb_ref, ones_ref,
                  pb_ref, yc_ref, pc_ref, wo_ref, g2_ref, wr_ref, br_ref, x1_ref, h2_ref, comb_ref):
    x = x_ref[0]
    mod = mod_ref[0]
    hb = (_rms(x, g1_ref[...]) * (1.0 + mod[1:2]) + mod[0:1]).astype(BF16)
    ones_bd = ones_ref[...]
    y = y_ref[0]
    mu = _head_sum(y, ones_bd) * (1.0 / RWKV_HEAD)
    yc = y - mu
    var = _head_sum(yc * yc, ones_bd) * (1.0 / RWKV_HEAD)
    yb = (yc * lax.rsqrt(var + GN_EPS) * lng_ref[...] + lnb_ref[...] + bonus_ref[0]) * g_ref[0]
    D = D_MODEL
    merged = _sigmoid(_dot(hb, wg_ref[:, 0:D])) * oa_ref[0]
    merged = merged + _sigmoid(_dot(hb, wg_ref[:, D:2 * D])) * _dot(yb, pb_ref[...])
    merged = merged + _sigmoid(_dot(hb, wg_ref[:, 2 * D:3 * D])) * _dot(yc_ref[0], pc_ref[...])
    x1 = x + mod[2:3] * _dot(merged, wo_ref[...])
    x1_ref[0] = x1
    h2 = _rms(x1, g2_ref[...]) * (1.0 + mod[4:5]) + mod[3:4]
    h2_ref[0] = h2.astype(BF16)
    logits = _dot3(h2, wr_ref[...]) + br_ref[...]
    lane = lax.broadcasted_iota(jnp.int32, logits.shape, 1)
    neg = -jnp.inf
    is_g = (lane >= N_EXPERTS) & (lane < N_EXPERTS + N_GROUPS)
    gl = jnp.where(is_g, logits, neg)
    gmax = jnp.max(gl, axis=1, keepdims=True)
    gsum = jnp.sum(jnp.exp(gl - gmax), axis=1, keepdims=True)
    g_val = 1.0 / gsum
    g_idx = jnp.min(jnp.where(is_g & (gl == gmax), lane, 4 * ROUTER_LANES), axis=1, keepdims=True) - N_EXPERTS
    in_grp = (lane >= g_idx * EXPERTS_PER_GROUP) & (lane < (g_idx + 1) * EXPERTS_PER_GROUP)
    el = jnp.where(in_grp, logits, neg)
    v1 = jnp.max(el, axis=1, keepdims=True)
    i1 = jnp.min(jnp.where(in_grp & (el == v1), lane, 4 * ROUTER_LANES), axis=1, keepdims=True)
    el2 = jnp.where(lane == i1, neg, el)
    v2 = jnp.max(el2, axis=1, keepdims=True)
    i2 = jnp.min(jnp.where(in_grp & (lane != i1) & (el2 == v2), lane, 4 * ROUTER_LANES), axis=1, keepdims=True)
    e2 = jnp.exp(v2 - v1)
    w1 = g_val / (1.0 + e2)
    w2 = g_val * e2 / (1.0 + e2)
    comb_ref[0] = jnp.where(lane == i1, w1, 0.0) + jnp.where(lane == i2, w2, 0.0)


def _merge(x, mod, oa, y, bonus, g, yc, lw, tm):
    B, T, D = x.shape
    row = lambda w: pl.BlockSpec((1, tm, w), lambda b, i: (b, i, 0))
    modspec = pl.BlockSpec((1, N_MOD, D), lambda b, i: (b, 0, 0))
    ins = [(x, row(D)), (mod, modspec), (lw['g1'], None), (lw['wg'], None), (oa, row(D)), (y, row(256)),
           (bonus, row(256)), (g, row(256)), (lw['ln_g'], None), (lw['ln_b'], None), (lw['ones_bd'], None),
           (lw['proj_b'], None), (yc, row(256)), (lw['proj_c'], None), (lw['w_out'], None), (lw['g2n'], None),
           (lw['wr'], None), (lw['br'], None)]
    return pl.pallas_call(
        _merge_kernel,
        grid=(B, T // tm),
        in_specs=[s if s is not None else _full(a) for a, s in ins],
        out_specs=[row(D), row(D), row(ROUTER_LANES)],
        out_shape=[jax.ShapeDtypeStruct((B, T, D), F32), jax.ShapeDtypeStruct((B, T, D), BF16),
                   jax.ShapeDtypeStruct((B, T, ROUTER_LANES), F32)],
        compiler_params=_cp(("parallel", "parallel")),
        name="merge",
    )(*[a for a, _ in ins])


def _moe_kernel(h2_ref, comb_ref, x1_ref, gt_ref, wgu_ref, wd_ref, fg_ref, o_ref, acc, *, final):
    e = pl.program_id(2)

    @pl.when(e == 0)
    def _():
        acc[...] = jnp.zeros_like(acc)

    gu = _dot(h2_ref[0], wgu_ref[0])
    gate = gu[:, 0:D_EXPERT]
    up = gu[:, D_EXPERT:2 * D_EXPERT]
    comb = comb_ref[0]
    lane = lax.broadcasted_iota(jnp.int32, comb.shape, 1)
    cw = jnp.sum(jnp.where(lane == e, comb, 0.0), axis=1, keepdims=True)
    act = gate * _sigmoid(gate) * up * cw
    acc[...] += _dot(act, wd_ref[0])

    @pl.when(e == N_EXPERTS - 1)
    def _():
        x2 = x1_ref[0] + gt_ref[0] * acc[...]
        o_ref[0] = _rms(x2, fg_ref[...]) if final else x2


def _moe(h2, comb, x1, gt, lw, final_g, tm, final):
    B, T, D = x1.shape
    gt_rows = gt.shape[1]
    gt_spec = (pl.BlockSpec((1, 1, D), lambda b, i, e: (b, 0, 0)) if gt_rows == 1
               else pl.BlockSpec((1, tm, D), lambda b, i, e: (b, i, 0)))
    row = lambda w: pl.BlockSpec((1, tm, w), lambda b, i, e: (b, i, 0))
    return pl.pallas_call(
        functools.partial(_moe_kernel, final=final),
        grid=(B, T // tm, N_EXPERTS),
        in_specs=[row(D), row(ROUTER_LANES), row(D), gt_spec,
                  pl.BlockSpec((1, D, 2 * D_EXPERT), lambda b, i, e: (e, 0, 0)),
                  pl.BlockSpec((1, D_EXPERT, D), lambda b, i, e: (e, 0, 0)),
                  pl.BlockSpec((1, D), lambda b, i, e: (0, 0))],
        out_specs=row(D),
        out_shape=jax.ShapeDtypeStruct((B, T, D), F32),
        scratch_shapes=[pltpu.VMEM((tm, D), F32)],
        compiler_params=_cp(("parallel", "parallel", "arbitrary")),
        name="moe",
    )(h2, comb, x1, gt, lw['wgu'], lw['wd'], final_g)


def _layer_weights(l, w):
    D = D_MODEL
    w_in = w['w_in'][l]
    c_q, c_kv, c_kr = Q_LORA, Q_LORA + KV_LORA, MLA_IN
    c_rw, c_cv = MLA_IN + RWKV_IN, MLA_IN + RWKV_IN + 2 * CONV_DIM
    half = ROPE_DIM // 2
    swap = np.concatenate([np.arange(half, ROPE_DIM), np.arange(half)])
    wkr = w_in[:, c_kv:c_kr]
    q_up = w['q_up'][l].reshape(Q_LORA, MLA_HEADS, NOPE_DIM + ROPE_DIM)
    q_rope = q_up[:, :, NOPE_DIM:]
    wukq = jnp.zeros((KV_LORA, PAIR_KEYS), F32)
    for hd in range(MLA_HEADS):
        wukq = wukq.at[:, 128 * hd:128 * hd + NOPE_DIM].set(w['w_uk'][l][:, hd, :])
    zl = jnp.zeros((W_LORA, RWKV_DIM), F32)
    head = np.arange(RWKV_DIM) // RWKV_HEAD
    wr = jnp.zeros((D, ROUTER_LANES), F32)
    wr = wr.at[:, 0:N_EXPERTS].set(w['router_expert_w'][l]).at[:, N_EXPERTS:N_EXPERTS + N_GROUPS].set(
        w['router_group_w'][l])
    br = jnp.zeros((1, ROUTER_LANES), F32)
    br = br.at[0, 0:N_EXPERTS].set(w['router_expert_b'][l]).at[0, N_EXPERTS:N_EXPERTS + N_GROUPS].set(
        w['router_group_b'][l])
    r2 = lambda a: a.reshape(1, -1)
    return dict(
        g1=r2(w['norm1_g'][l]), g2n=r2(w['norm2_g'][l]),
        wq=w_in[:, :c_q].astype(BF16), wkv=w_in[:, c_q:c_kv].astype(BF16),
        wkr=jnp.concatenate([jnp.tile(wkr, (1, 8)), jnp.tile(wkr[:, swap], (1, 8))], axis=1).astype(BF16),
        wrw=w_in[:, c_kr:c_rw].astype(BF16), wcv=w_in[:, c_rw:c_cv].astype(BF16), wg=w_in[:, c_cv:].astype(BF16),
        wkvt=w_in[:, c_q:c_kv].T.astype(BF16), kvgc=w['kv_norm_g'][l].reshape(-1, 1),
        qg=r2(w['q_norm_g'][l]), wqnt=q_up[:, :, :NOPE_DIM].reshape(Q_LORA, -1).T.astype(BF16),
        wukq=wukq.astype(BF16), kvg=r2(w['kv_norm_g'][l]),
        wqrt=q_rope.reshape(Q_LORA, -1).T.astype(BF16), wqrst=q_rope[:, :, swap].reshape(Q_LORA, -1).T.astype(BF16),
        wuvt=jnp.transpose(w['w_uv'][l], (1, 2, 0)).astype(BF16), proj_at=w['proj_a'][l].T.astype(BF16),
        mu=r2(w['rwkv_mu'][l]), w0=r2(w['rwkv_w0'][l]), a0=r2(w['rwkv_a0'][l]),
        w2p=jnp.concatenate([w['rwkv_w2'][l], zl], axis=0).astype(BF16),
        a2p=jnp.concatenate([zl, w['rwkv_a2'][l]], axis=0).astype(BF16),
        g2=w['rwkv_g2'][l].astype(BF16), k_k=r2(w['rwkv_k_k'][l]), k_a=r2(w['rwkv_k_a'][l]),
        r_k=r2(w['rwkv_r_k'][l]), ln_g=r2(w['rwkv_ln_g'][l]), ln_b=r2(w['rwkv_ln_b'][l]),
        ones_bd=jnp.asarray((head[:, None] == head[None, :]).astype(np.float32)).astype(BF16),
        proj_b=w['proj_b'][l].astype(BF16),
        conv_dw=w['conv_dw'][l], conv_dw_b=r2(w['conv_dw_b'][l]), conv_ln_g=r2(w['conv_ln_g'][l]),
        conv_ln_b=r2(w['conv_ln_b'][l]), proj_c=w['proj_c'][l].astype(BF16),
        w_out=w['w_out'][l].astype(BF16), wr=wr, br=br,
        wgu=jnp.concatenate([w['moe_w_gate'][l], w['moe_w_up'][l]], axis=2).astype(BF16),
        wd=w['moe_w_down'][l].astype(BF16),
    )


def _rope_tables(past, T):
    half = ROPE_DIM // 2
    inv = ROPE_THETA ** (-jnp.arange(half, dtype=F32) / half)
    ang = (past + jnp.arange(T)).astype(F32)[:, None] * inv[None, :]
    cos, sin = jnp.cos(ang), jnp.sin(ang)
    cos8 = jnp.tile(jnp.concatenate([cos, cos], axis=1), (1, 8))
    sin8 = jnp.tile(jnp.concatenate([-sin, sin], axis=1), (1, 8))
    return cos8, sin8, cos8.T, sin8.T


def _row_tile(T, cap):
    return min(T, cap)


def _trunk_layer(x, mod, cache_lat, cache_rope, shift_st, wkv_st, conv_st, lw, final_g, final):
    B, T, D = x.shape
    past = 0 if cache_lat is None else cache_lat.shape[1]
    tm = _row_tile(T, 256)
    lat, kr, keys, latt, qt, prw, glu = _in_proj(x, mod, lw, _rope_tables(past, T), tm)

    _, t_pad, _, s_pad, n_keys = _attn_tiles(T, past)
    if past:
        old = _key_up(cache_lat, jnp.tile(cache_rope, (1, 1, 8)), lw['wukq'], _row_tile(past, 512))
        keys = jnp.concatenate([old, keys], axis=1)
        latt = jnp.concatenate([jnp.swapaxes(cache_lat, 1, 2).astype(BF16), latt], axis=2)
    if s_pad != n_keys:
        keys = jnp.pad(keys, ((0, 0), (0, s_pad - n_keys), (0, 0)))
        latt = jnp.pad(latt, ((0, 0), (0, 0), (0, s_pad - n_keys)))
    if t_pad != T:
        qt = jnp.pad(qt, ((0, 0), (0, 0), (0, 0), (0, t_pad - T)))
    o_a = _attention(qt, keys, latt, lw['wuvt'], lw['proj_at'], past, T)[:, :T]

    nT = T // tm
    first = jnp.zeros((B, 1, RWKV_IN), F32) if shift_st is None else shift_st[:, None, :]
    prev_rows = jnp.concatenate([first, prw[:, tm - 1:T - 1:tm, :]], axis=1).reshape(B, nT, 1, RWKV_IN)
    r, kf, bb, kk, v, lwd, g, bonus = _rwkv_prep(prw, prev_rows, lw, tm)
    Tp = -(-T // WKV_CHUNK) * WKV_CHUNK
    scan_in = [r, kf, bb, kk, v, lwd]
    if Tp != T:
        scan_in = [jnp.pad(a, ((0, 0), (0, Tp - T), (0, 0))) for a in scan_in]
    W = RWKV_DIM
    s0 = jnp.zeros((B, W, W), F32)
    if wkv_st is not None:
        for hd in range(RWKV_HEADS):
            o = hd * RWKV_HEAD
            s0 = s0.at[:, o:o + RWKV_HEAD, o:o + RWKV_HEAD].set(wkv_st[:, hd])
    y, s_bd = _wkv(*scan_in, s0)
    y = y[:, :T]
    wkv_new = jnp.stack([s_bd[:, hd * RWKV_HEAD:(hd + 1) * RWKV_HEAD, hd * RWKV_HEAD:(hd + 1) * RWKV_HEAD]
                         for hd in range(RWKV_HEADS)], axis=1)

    pad_rows = CONV_HALO - (CONV_WIDTH - 1)
    st = jnp.zeros((B, CONV_WIDTH - 1, CONV_DIM), F32) if conv_st is None else conv_st
    yc, conv_tail = _conv(glu, jnp.pad(st, ((0, 0), (pad_rows, 0), (0, 0))), lw, tm)
    conv_new = conv_tail[:, pad_rows:]

    x1, h2, comb = _merge(x, mod, o_a, y, bonus, g, yc, lw, tm)
    if T >= 1024:
        x2 = _moe(h2, comb, x1, mod[:, 5:6], lw, final_g, 1024, final)
    else:
        gt = jnp.broadcast_to(mod[:, 5:6], (B, T, D)).reshape(1, B * T, D)
        x2 = _moe(h2.reshape(1, B * T, D), comb.reshape(1, B * T, ROUTER_LANES), x1.reshape(1, B * T, D), gt, lw,
                  final_g, B * T, final).reshape(B, T, D)
    return x2, (lat, kr, prw[:, -1], wkv_new, conv_new)


def kernel(x_prompt, x_sample, c_prompt, c_sample, cache_kv_latent, cache_k_rope, state_rwkv_shift, state_rwkv_wkv, state_conv, ada_w, ada_b, norm1_g, norm2_g, w_in, q_norm_g, q_up, kv_norm_g, w_uk, w_uv, proj_a, rwkv_mu, rwkv_w0, rwkv_w2, rwkv_a0, rwkv_a2, rwkv_g2, rwkv_k_k, rwkv_k_a, rwkv_r_k, rwkv_ln_g, rwkv_ln_b, proj_b, conv_dw, conv_dw_b, conv_ln_g, conv_ln_b, proj_c, w_out, router_group_w, router_group_b, router_expert_w, router_expert_b, moe_w_gate, moe_w_up, moe_w_down, final_g):
    w = dict(norm1_g=norm1_g, norm2_g=norm2_g, w_in=w_in, q_norm_g=q_norm_g, q_up=q_up, kv_norm_g=kv_norm_g,
             w_uk=w_uk, w_uv=w_uv, proj_a=proj_a, rwkv_mu=rwkv_mu, rwkv_w0=rwkv_w0, rwkv_w2=rwkv_w2,
             rwkv_a0=rwkv_a0, rwkv_a2=rwkv_a2, rwkv_g2=rwkv_g2, rwkv_k_k=rwkv_k_k, rwkv_k_a=rwkv_k_a,
             rwkv_r_k=rwkv_r_k.reshape(rwkv_r_k.shape[0], -1), rwkv_ln_g=rwkv_ln_g, rwkv_ln_b=rwkv_ln_b,
             proj_b=proj_b, conv_dw=conv_dw, conv_dw_b=conv_dw_b, conv_ln_g=conv_ln_g, conv_ln_b=conv_ln_b,
             proj_c=proj_c, w_out=w_out, router_group_w=router_group_w, router_group_b=router_group_b,
             router_expert_w=router_expert_w, router_expert_b=router_expert_b, moe_w_gate=moe_w_gate,
             moe_w_up=moe_w_up, moe_w_down=moe_w_down)
    depth = ada_w.shape[0]
    bp = x_prompt.shape[0]
    D = D_MODEL
    mod_all = _ada_mod(jnp.concatenate([c_prompt, c_sample], axis=0), ada_w, ada_b)
    fg = final_g.reshape(1, D)
    hp, hs = x_prompt, x_sample
    new_p, new_s = [], []
    for l in range(depth):
        lw = _layer_weights(l, w)
        mod_p = mod_all[l, :bp].reshape(bp, N_MOD, D)
        mod_s = mod_all[l, bp:].reshape(-1, N_MOD, D)
        final = l == depth - 1
        hp, st_p = _trunk_layer(hp, mod_p, None, None, None, None, None, lw, fg, final)
        hs, st_s = _trunk_layer(hs, mod_s, cache_kv_latent[l], cache_k_rope[l], state_rwkv_shift[l],
                                state_rwkv_wkv[l], state_conv[l], lw, fg, final)
        new_p.append(st_p)
        new_s.append(st_s)
    stack = lambda states, i: jnp.stack([s[i] for s in states], axis=0)
    return ((hp, hs) + tuple(stack(new_p, i) for i in range(5)) + tuple(stack(new_s, i) for i in range(5)))
```

```python
import functools

import numpy as np
import jax
import jax.numpy as jnp
from jax import lax
from jax.experimental import pallas as pl
from jax.experimental.pallas import tpu as pltpu

F32 = jnp.float32
BF16 = jnp.bfloat16

D_MODEL = 1024
CHUNK = 64
NORM_EPS = 1e-6
MLA_HEADS = 8
Q_LORA = 384
KV_LORA = 256
NOPE_DIM = 64
ROPE_DIM = 32
V_HEAD = 64
ROPE_THETA = 10000.0
SM_SCALE = (NOPE_DIM + ROPE_DIM) ** -0.5
LOG2_E = 1.4426950408889634
RWKV_HEADS = 4
RWKV_HEAD = 64
RWKV_DIM = RWKV_HEADS * RWKV_HEAD
W_LORA = 64
A_LORA = 64
G_LORA = 128
RWKV_IN = 3 * RWKV_DIM + W_LORA + A_LORA + G_LORA
GN_EPS = 64e-5
CONV_DIM = 256
CONV_WIDTH = 31
LN_EPS = 1e-5
MLA_IN = Q_LORA + KV_LORA + ROPE_DIM
N_MOD = 6
N_GROUPS = 4
EXPERTS_PER_GROUP = 4
N_EXPERTS = 16
D_EXPERT = 256
WKV_CHUNK = 64
CONV_HALO = 32
ROUTER_LANES = 128
PAIR_KEYS = 4 * 256
VMEM_LIMIT = 56 * 1024 * 1024


def _cp(sem):
    return pltpu.CompilerParams(dimension_semantics=sem, vmem_limit_bytes=VMEM_LIMIT)


def _full(a):
    nd = a.ndim
    return pl.BlockSpec(a.shape, lambda *_: (0,) * nd)


def _dot(a, b, dims=(((1,), (0,)), ((), ()))):
    return lax.dot_general(a.astype(BF16), b.astype(BF16), dims, preferred_element_type=F32)


def _split(a):
    hi = a.astype(BF16)
    lo = (a - hi.astype(F32)).astype(BF16)
    return hi, lo


def _dot_hl(a, b, dims=(((1,), (0,)), ((), ()))):
    hi, lo = _split(a)
    bb = b.astype(BF16)
    return (lax.dot_general(hi, bb, dims, preferred_element_type=F32)
            + lax.dot_general(lo, bb, dims, preferred_element_type=F32))


def _dot3(a, b, dims=(((1,), (0,)), ((), ()))):
    ah, al = _split(a)
    bh, bl = _split(b)
    d = lambda x, y: lax.dot_general(x, y, dims, preferred_element_type=F32)
    return d(ah, bh) + d(ah, bl) + d(al, bh)


def _sigmoid(x):
    return 1.0 / (1.0 + jnp.exp(-x))


def _rms(x, g):
    return x * lax.rsqrt(jnp.mean(x * x, axis=-1, keepdims=True) + NORM_EPS) * g


def _ada_kernel(c_ref, w_ref, b_ref, o_ref):
    c = c_ref[...]
    o_ref[0] = _dot(c * _sigmoid(c), w_ref[0]) + b_ref[0]


def _ada_mod(c_all, ada_w, ada_b):
    L, D, C = ada_w.shape
    R = c_all.shape[0]
    nb = C // D
    return pl.pallas_call(
        _ada_kernel,
        grid=(L, nb),
        in_specs=[pl.BlockSpec((R, D), lambda l, j: (0, 0)),
                  pl.BlockSpec((1, D, D), lambda l, j: (l, 0, j)),
                  pl.BlockSpec((1, 1, D), lambda l, j: (l, 0, j))],
        out_specs=pl.BlockSpec((1, R, D), lambda l, j: (l, 0, j)),
        out_shape=jax.ShapeDtypeStruct((L, R, C), F32),
        compiler_params=_cp(("parallel", "parallel")),
        name="ada_mod",
    )(c_all, ada_w, ada_b.reshape(L, 1, C))


_NT = (((1,), (1,)), ((), ()))


def _pair_keys(lat, rope8, wukq):
    lane = lax.broadcasted_iota(jnp.int32, rope8.shape, 1) & 127
    rope_slot = jnp.where((lane >= NOPE_DIM) & (lane < NOPE_DIM + ROPE_DIM), rope8, 0.0)
    return (_dot(lat, wukq) + jnp.concatenate([rope_slot] * (MLA_HEADS // 2), axis=1)).astype(BF16)


def _in_proj_kernel(x_ref, mod_ref, g1_ref, wq_ref, wkv_ref, wkvt_ref, wkr_ref, wrw_ref, wcv_ref, qg_ref, wqnt_ref,
                    wqrt_ref, wqrst_ref, wukq_ref, kvg_ref, kvgc_ref, cos_ref, sin_ref, cost_ref, sint_ref,
                    lat_ref, kr_ref, kq_ref, latt_ref, qt_ref, prw_ref, glu_ref):
    x = x_ref[0]
    mod = mod_ref[0]
    tm = x.shape[0]
    h = _rms(x, g1_ref[...]) * (1.0 + mod[1:2]) + mod[0:1]
    hb = h.astype(BF16)
    qn = _rms(_dot(hb, wq_ref[...]), qg_ref[...]).astype(BF16)
    qnt = _dot(wqnt_ref[...], qn, _NT)
    qrt = _dot(wqrt_ref[...], qn, _NT)
    qrst = _dot(wqrst_ref[...], qn, _NT)
    qrope = qrt * cost_ref[...] + qrst * sint_ref[...]
    zpad = jnp.zeros((128 - NOPE_DIM - ROPE_DIM, tm), F32)
    for hd in range(MLA_HEADS):
        qh = jnp.concatenate([qnt[NOPE_DIM * hd:NOPE_DIM * (hd + 1)], qrope[ROPE_DIM * hd:ROPE_DIM * (hd + 1)],
                              zpad], axis=0)
        qt_ref[0, hd] = (qh * (SM_SCALE * LOG2_E)).astype(BF16)
    lat = _rms(_dot(hb, wkv_ref[...]), kvg_ref[...])
    lat_ref[0] = lat
    pkvt = _dot(wkvt_ref[...], hb, _NT)
    latt = pkvt * lax.rsqrt(jnp.mean(pkvt * pkvt, axis=0, keepdims=True) + NORM_EPS) * kvgc_ref[...]
    latt_ref[0] = latt.astype(BF16)
    pkr = _dot(hb, wkr_ref[...])
    krt = pkr[:, 0:256] * cos_ref[...] + pkr[:, 256:512] * sin_ref[...]
    kr_ref[0] = krt[:, 0:ROPE_DIM]
    kq_ref[0] = _pair_keys(lat, krt, wukq_ref[...])
    prw_ref[0] = _dot(hb, wrw_ref[...])
    pcv = _dot(hb, wcv_ref[...])
    glu_ref[0] = pcv[:, 0:CONV_DIM] * _sigmoid(pcv[:, CONV_DIM:2 * CONV_DIM])


def _in_proj(x, mod, lw, tabs, tm):
    B, T, D = x.shape
    nT = T // tm
    cos8, sin8, cos8t, sin8t = tabs
    row = lambda w: pl.BlockSpec((1, tm, w), lambda b, i: (b, i, 0))
    col = pl.BlockSpec((1, 256, tm), lambda b, i: (b, 0, i))
    args = (lw['g1'], lw['wq'], lw['wkv'], lw['wkvt'], lw['wkr'], lw['wrw'], lw['wcv'], lw['qg'], lw['wqnt'],
            lw['wqrt'], lw['wqrst'], lw['wukq'], lw['kvg'], lw['kvgc'])
    return pl.pallas_call(
        _in_proj_kernel,
        grid=(B, nT),
        in_specs=[row(D), pl.BlockSpec((1, N_MOD, D), lambda b, i: (b, 0, 0))] + [_full(a) for a in args]
                 + [pl.BlockSpec((tm, 256), lambda b, i: (i, 0))] * 2
                 + [pl.BlockSpec((256, tm), lambda b, i: (0, i))] * 2,
        out_specs=[row(KV_LORA), row(ROPE_DIM), row(PAIR_KEYS), col,
                   pl.BlockSpec((1, MLA_HEADS, 128, tm), lambda b, i: (b, 0, 0, i)),
                   row(RWKV_IN), row(CONV_DIM)],
        out_shape=[jax.ShapeDtypeStruct((B, T, KV_LORA), F32),
                   jax.ShapeDtypeStruct((B, T, ROPE_DIM), F32),
                   jax.ShapeDtypeStruct((B, T, PAIR_KEYS), BF16),
                   jax.ShapeDtypeStruct((B, KV_LORA, T), BF16),
                   jax.ShapeDtypeStruct((B, MLA_HEADS, 128, T), BF16),
                   jax.ShapeDtypeStruct((B, T, RWKV_IN), F32),
                   jax.ShapeDtypeStruct((B, T, CONV_DIM), F32)],
        compiler_params=_cp(("parallel", "parallel")),
        name="in_proj",
    )(x, mod, *args, cos8, sin8, cos8t, sin8t)


def _key_up_kernel(lat_ref, rope8_ref, wukq_ref, kq_ref):
    kq_ref[0] = _pair_keys(lat_ref[0], rope8_ref[0], wukq_ref[...])


def _key_up(lat, rope8, wukq, tm):
    B, S, _ = lat.shape
    row = lambda w: pl.BlockSpec((1, tm, w), lambda b, i: (b, i, 0))
    return pl.pallas_call(
        _key_up_kernel,
        grid=(B, S // tm),
        in_specs=[row(KV_LORA), row(256), _full(wukq)],
        out_specs=row(PAIR_KEYS),
        out_shape=jax.ShapeDtypeStruct((B, S, PAIR_KEYS), BF16),
        compiler_params=_cp(("parallel", "parallel")),
        name="key_up",
    )(lat, rope8, wukq)


def _attn_kernel(qi_ref, ki_ref, last_ref, qt_ref, k_ref, vt_ref, wuvt_ref, pat_ref, o_ref,
                 qcat, m_s, l_s, acc, *, tq, tk, past, n_keys):
    p = pl.program_id(1)
    qi = qi_ref[p]
    ki = ki_ref[p]
    H = MLA_HEADS
    NP = H // 2
    nh = tq // 128
    NC = NP * nh

    @pl.when(ki == 0)
    def _():
        qcat[...] = jnp.zeros(qcat.shape, BF16)
        for pr in range(NP):
            for hf in range(nh):
                pos = slice(128 * hf, 128 * (hf + 1))
                qcat[pr * nh + hf, 0:128, 0:128] = qt_ref[0, 2 * pr, :, pos]
                qcat[pr * nh + hf, 128:256, 128:256] = qt_ref[0, 2 * pr + 1, :, pos]
        m_s[...] = jnp.full(m_s.shape, -jnp.inf, F32)
        l_s[...] = jnp.zeros(l_s.shape, F32)
        acc[...] = jnp.zeros(acc.shape, F32)

    def step(bias):
        vt = vt_ref[0]
        scores = {}

        def qk(ch):
            pr = ch // nh
            scores[ch] = jnp.dot(k_ref[0, :, 256 * pr:256 * (pr + 1)], qcat[ch], preferred_element_type=F32)

        qk(0)
        qk(1)
        for ch in range(NC):
            if ch + 2 < NC:
                qk(ch + 2)
            s = scores.pop(ch)
            if bias is not None:
                s = s + bias[ch % nh]
            m_prev = m_s[ch]
            m_new = jnp.maximum(m_prev, jnp.max(s, axis=0, keepdims=True))
            alpha = jnp.exp2(m_prev - m_new)
            pe = jnp.exp2(s - m_new)
            l_s[ch] = alpha * l_s[ch] + jnp.sum(pe, axis=0, keepdims=True)
            acc[ch] = alpha * acc[ch] + jnp.dot(vt, pe.astype(BF16), preferred_element_type=F32)
            m_s[ch] = m_new

    is_last = last_ref[p] == 1

    @pl.when(jnp.logical_not(is_last))
    def _():
        step(None)

    @pl.when(is_last)
    def _():
        kpos = ki * tk + lax.broadcasted_iota(jnp.int32, (tk, 1), 0)
        biases = []
        for hf in range(nh):
            t = 128 * hf + lax.broadcasted_iota(jnp.int32, (1, 128), 1)
            limit = jnp.minimum((((past + qi * tq + t) >> 6) + 1) << 6, n_keys)
            bias = jnp.where(kpos < limit, 0.0, -jnp.inf)
            biases.append(jnp.concatenate([bias, bias], axis=1))
        step(biases)
        heads = []
        for pr in range(NP):
            o = [acc[pr * nh + hf] / l_s[pr * nh + hf] for hf in range(nh)]
            for j in range(2):
                oh = jnp.concatenate([x[:, 128 * j:128 * (j + 1)] for x in o], axis=1)
                heads.append(_dot(wuvt_ref[2 * pr + j], oh))
        out_t = _dot(pat_ref[...], jnp.concatenate(heads, axis=0))
        o_ref[0] = out_t.T


def _attn_tiles(T, past):
    n_keys = past + T
    tq = 256 if T % 256 == 0 else 128
    t_pad = -(-T // tq) * tq
    tk = -(-n_keys // 128) * 128 if n_keys <= 2304 else 512
    s_pad = -(-n_keys // tk) * tk
    return tq, t_pad, tk, s_pad, n_keys


def _attention(qt, keys, latt, wuvt, proj_at, past, T):
    B, H = qt.shape[:2]
    tq, t_pad, tk, s_pad, n_keys = _attn_tiles(T, past)
    assert keys.shape[1] == s_pad and latt.shape[2] == s_pad and qt.shape[3] == t_pad and past % CHUNK == 0
    qi, ki, last = [], [], []
    for i in range(t_pad // tq):
        lim = min(((past + (i + 1) * tq - 1) // CHUNK + 1) * CHUNK, n_keys)
        nk = -(-lim // tk)
        for j in range(nk):
            qi.append(i), ki.append(j), last.append(int(j == nk - 1))
    npairs = len(qi)
    tabs = [jnp.asarray(np.asarray(a, np.int32)) for a in (qi, ki, last)]
    nc = (H // 2) * (tq // 128)
    grid_spec = pltpu.PrefetchScalarGridSpec(
        num_scalar_prefetch=3,
        grid=(B, npairs),
        in_specs=[pl.BlockSpec((1, H, 128, tq), lambda b, p, qi, ki, la: (b, 0, 0, qi[p])),
                  pl.BlockSpec((1, tk, PAIR_KEYS), lambda b, p, qi, ki, la: (b, ki[p], 0)),
                  pl.BlockSpec((1, KV_LORA, tk), lambda b, p, qi, ki, la: (b, 0, ki[p])),
                  pl.BlockSpec(wuvt.shape, lambda b, p, qi, ki, la: (0, 0, 0)),
                  pl.BlockSpec(proj_at.shape, lambda b, p, qi, ki, la: (0, 0))],
        out_specs=pl.BlockSpec((1, tq, D_MODEL), lambda b, p, qi, ki, la: (b, qi[p], 0)),
        scratch_shapes=[pltpu.VMEM((nc, 256, 256), BF16), pltpu.VMEM((nc, 1, 256), F32),
                        pltpu.VMEM((nc, 1, 256), F32), pltpu.VMEM((nc, KV_LORA, 256), F32)])
    return pl.pallas_call(
        functools.partial(_attn_kernel, tq=tq, tk=tk, past=past, n_keys=n_keys),
        grid_spec=grid_spec,
        out_shape=jax.ShapeDtypeStruct((B, t_pad, D_MODEL), F32),
        compiler_params=_cp(("parallel", "arbitrary")),
        name="attention",
    )(*tabs, qt, keys, latt, wuvt, proj_at)


def _head_sum(x, ones_bd):
    return _dot_hl(x, ones_bd)


def _rwkv_prep_kernel(prw_ref, prev_ref, mu_ref, w0_ref, w2_ref, a0_ref, a2_ref, g2_ref, kk_ref, ka_ref, rk_ref,
                      ones_ref, r_ref, k_ref, b_ref, kkn_ref, v_ref, lw_ref, g_ref, bonus_ref, *, tm):
    pr = prw_ref[0]
    row = lax.broadcasted_iota(jnp.int32, (tm, 1), 0)
    prev = jnp.where(row == 0, prev_ref[0, 0], pltpu.roll(pr, 1, 0))
    xs = pr + (prev - pr) * mu_ref[...]
    r = xs[:, 0:256]
    k = xs[:, 256:512]
    v = xs[:, 512:768]
    wa = xs[:, 768:896]
    gl = xs[:, 896:1024]
    z = w0_ref[...] + _dot(jnp.tanh(wa), w2_ref[...])
    nz = -z
    softplus = jnp.maximum(nz, 0.0) + jnp.log(1.0 + jnp.exp(-jnp.abs(nz)))
    lw_ref[0] = -jnp.exp(-softplus - 0.5)
    a = _sigmoid(a0_ref[...] + _dot(wa, a2_ref[...]))
    g_ref[0] = _dot(_sigmoid(gl), g2_ref[...])
    ones_bd = ones_ref[...]
    kk = k * kk_ref[...]
    kk = kk * lax.rsqrt(_head_sum(kk * kk, ones_bd) + 1e-12)
    kf = k * (1.0 + (a - 1.0) * ka_ref[...])
    r_ref[0] = r
    k_ref[0] = kf
    v_ref[0] = v
    kkn_ref[0] = kk
    b_ref[0] = kk * a
    bonus_ref[0] = _head_sum(r * kf * rk_ref[...], ones_bd) * v


def _rwkv_prep(prw, prev_rows, lw, tm):
    B, T, _ = prw.shape
    nT = T // tm
    args = (lw['mu'], lw['w0'], lw['w2p'], lw['a0'], lw['a2p'], lw['g2'], lw['k_k'], lw['k_a'], lw['r_k'],
            lw['ones_bd'])
    row = pl.BlockSpec((1, tm, 256), lambda b, i: (b, i, 0))
    return pl.pallas_call(
        functools.partial(_rwkv_prep_kernel, tm=tm),
        grid=(B, nT),
        in_specs=[pl.BlockSpec((1, tm, RWKV_IN), lambda b, i: (b, i, 0)),
                  pl.BlockSpec((1, 1, 1, RWKV_IN), lambda b, i: (b, i, 0, 0))] + [_full(a) for a in args],
        out_specs=[row] * 8,
        out_shape=[jax.ShapeDtypeStruct((B, T, 256), F32)] * 8,
        compiler_params=_cp(("parallel", "parallel")),
        name="rwkv_prep",
    )(prw, prev_rows, *args)


def _wkv_kernel(r_ref, k_ref, b_ref, kk_ref, v_ref, lw_ref, s0_ref, y_ref, sout_ref, S, *, nb, G):
    L = WKV_CHUNK
    W = RWKV_DIM

    @pl.when(pl.program_id(0) == 0)
    def _():
        S[...] = s0_ref[...]

    ri = lax.broadcasted_iota(jnp.int32, (W, W), 0)
    ci = lax.broadcasted_iota(jnp.int32, (W, W), 1)
    same = (ri >> 6) == (ci >> 6)
    strict = same & ((ci & 63) < (ri & 63))
    incl = same & ((ci & 63) <= (ri & 63))
    eye = (ri == ci).astype(F32)
    tri = (lax.broadcasted_iota(jnp.int32, (L, L), 1) <= lax.broadcasted_iota(jnp.int32, (L, L), 0)).astype(BF16)
    lane_head = lax.broadcasted_iota(jnp.int32, (L, W), 1) >> 6
    nt = (((1,), (1,)), ((), ()))
    tn = (((0,), (0,)), ((), ()))

    def stack(x):
        return jnp.concatenate([jnp.where(lane_head == hd, x, 0.0) for hd in range(RWKV_HEADS)], axis=0)

    def collapse(z):
        return z[0:L] + z[L:2 * L] + z[2 * L:3 * L] + z[3 * L:4 * L]

    def body(it, carry):
        ids = [it * G + g for g in range(G)]
        each = lambda f, *xs: [f(*a) for a in zip(*xs)]
        r, kf, bb, kk, v, ll, s_prev = [[ref[i] for i in ids] for ref in
                                        (r_ref, k_ref, b_ref, kk_ref, v_ref, lw_ref, S)]
        split = each(_split, ll)
        c = each(lambda hl: jnp.dot(tri, hl[0], preferred_element_type=F32)
                 + jnp.dot(tri, hl[1], preferred_element_type=F32), split)
        c_last = each(lambda x: x[L - 1:L, :], c)
        e_neg = each(lambda x: jnp.exp(-x), c)
        e_end = each(lambda x, y: jnp.exp(y - x), c, c_last)
        rt = each(lambda x, y: stack(x * jnp.exp(y)), r, c)
        kkt = each(lambda x, y, z: stack(x * jnp.exp(y - z)), kk, c, ll)
        bt = each(lambda x, y: stack(x * y), bb, e_neg)
        kt = each(lambda x, y: stack(x * y), kf, e_neg)
        a_m = each(lambda x, y: jnp.where(strict, _dot(x, y, nt), 0.0), kkt, bt)
        b_m = each(lambda x, y: jnp.where(strict, _dot(x, y, nt), 0.0), kkt, kt)
        m1 = each(lambda x, y: jnp.where(incl, _dot(x, y, nt), 0.0), rt, bt)
        m2 = each(lambda x, y: jnp.where(incl, _dot(x, y, nt), 0.0), rt, kt)
        t_m = each(lambda x: eye - x, a_m)
        a_p = a_m
        for _ in range(5):
            a_p = each(lambda x: _dot(x, x), a_p)
            t_m = each(lambda x, y: x + _dot(x, y), t_m, a_p)
        vs = each(stack, v)
        w1 = each(lambda x, y: collapse(_dot(x, y)), t_m, kkt)
        bv = each(_dot, b_m, vs)
        w2 = each(lambda x, y: collapse(_dot(x, y)), t_m, bv)
        u = each(lambda x, y, z: -(_dot(x, y, nt) + z), w1, s_prev, w2)
        y0 = each(lambda x, y: _dot(collapse(x), y, nt), rt, s_prev)
        y1 = each(lambda a, b_, c_, d: collapse(_dot(a, stack(b_)) + _dot(c_, d)), m1, u, m2, vs)
        upd = each(lambda a, b_, c_, d, e: _dot(a, b_ * e, tn) + _dot(c_, d * e, tn), u, bb, v, kf, e_end)
        for g, i in enumerate(ids):
            y_ref[i] = y0[g] + y1[g]
            S[i] = s_prev[g] * jnp.exp(c_last[g]) + jnp.where(same, upd[g], 0.0)
        return carry

    lax.fori_loop(0, nb // G, body, 0)
    sout_ref[...] = S[...]


def _wkv(r, kf, bb, kk, v, lw, s0):
    B, T, W = r.shape
    L = WKV_CHUNK
    assert T % L == 0
    blk = pl.BlockSpec((B, L, W), lambda c: (0, c, 0))
    st = pl.BlockSpec((B, W, W), lambda c: (0, 0, 0))
    return pl.pallas_call(
        functools.partial(_wkv_kernel, nb=B, G=4 if B % 4 == 0 else 1),
        grid=(T // L,),
        in_specs=[blk] * 6 + [st],
        out_specs=[blk, st],
        out_shape=[jax.ShapeDtypeStruct((B, T, W), F32), jax.ShapeDtypeStruct((B, W, W), F32)],
        scratch_shapes=[pltpu.VMEM((B, W, W), F32)],
        compiler_params=_cp(("arbitrary",)),
        name="wkv",
    )(r, kf, bb, kk, v, lw, s0)


def _conv_kernel(glu_ref, st_ref, dw_ref, dwb_ref, lng_ref, lnb_ref, y_ref, new_ref, ext, *, tm):
    @pl.when(pl.program_id(1) == 0)
    def _():
        ext[0:CONV_HALO, :] = st_ref[0]
    ext[CONV_HALO:CONV_HALO + tm, :] = glu_ref[0]
    off = CONV_HALO - (CONV_WIDTH - 1)
    acc = jnp.zeros((tm, CONV_DIM), F32)
    for w in range(CONV_WIDTH):
        acc = acc + ext[off + w:off + w + tm, :] * dw_ref[w:w + 1, :]
    y = acc + dwb_ref[...]
    mu = jnp.mean(y, axis=-1, keepdims=True)
    var = jnp.mean(jnp.square(y - mu), axis=-1, keepdims=True)
    y = (y - mu) * lax.rsqrt(var + LN_EPS) * lng_ref[...] + lnb_ref[...]
    y_ref[0] = y * _sigmoid(y)
    tail = ext[tm:tm + CONV_HALO, :]
    new_ref[0] = tail
    ext[0:CONV_HALO, :] = tail


def _conv(glu, st_pad, lw, tm):
    B, T, C = glu.shape
    args = (lw['conv_dw'], lw['conv_dw_b'], lw['conv_ln_g'], lw['conv_ln_b'])
    return pl.pallas_call(
        functools.partial(_conv_kernel, tm=tm),
        grid=(B, T // tm),
        in_specs=[pl.BlockSpec((1, tm, C), lambda b, i: (b, i, 0)),
                  pl.BlockSpec((1, CONV_HALO, C), lambda b, i: (b, 0, 0))] + [_full(a) for a in args],
        out_specs=[pl.BlockSpec((1, tm, C), lambda b, i: (b, i, 0)),
                   pl.BlockSpec((1, CONV_HALO, C), lambda b, i: (b, 0, 0))],
        out_shape=[jax.ShapeDtypeStruct((B, T, C), F32), jax.ShapeDtypeStruct((B, CONV_HALO, C), F32)],
        scratch_shapes=[pltpu.VMEM((CONV_HALO + tm, C), F32)],
        compiler_params=_cp(("parallel", "arbitrary")),
        name="conv",
    )(glu, st_pad, *args)


def _merge_kernel(x_ref, mod_ref, g1_ref, wg_ref, oa_ref, y_ref, bonus_ref, g_ref, lng_ref, lnb_ref, ones_ref,
                  pb_ref, yc_ref, pc_ref, wo_ref, g2_ref, wr_ref, br_ref, x1_ref, h2_ref, comb_ref):
    x = x_ref[0]
    mod = mod_ref[0]
    hb = (_rms(x, g1_ref[...]) * (1.0 + mod[1:2]) + mod[0:1]).astype(BF16)
    ones_bd = ones_ref[...]
    y = y_ref[0]
    mu = _head_sum(y, ones_bd) * (1.0 / RWKV_HEAD)
    yc = y - mu
    var = _head_sum(yc * yc, ones_bd) * (1.0 / RWKV_HEAD)
    yb = (yc * lax.rsqrt(var + GN_EPS) * lng_ref[...] + lnb_ref[...] + bonus_ref[0]) * g_ref[0]
    D = D_MODEL
    merged = _sigmoid(_dot(hb, wg_ref[:, 0:D])) * oa_ref[0]
    merged = merged + _sigmoid(_dot(hb, wg_ref[:, D:2 * D])) * _dot(yb, pb_ref[...])
    merged = merged + _sigmoid(_dot(hb, wg_ref[:, 2 * D:3 * D])) * _dot(yc_ref[0], pc_ref[...])
    x1 = x + mod[2:3] * _dot(merged, wo_ref[...])
    x1_ref[0] = x1
    h2 = _rms(x1, g2_ref[...]) * (1.0 + mod[4:5]) + mod[3:4]
    h2_ref[0] = h2.astype(BF16)
    logits = _dot3(h2, wr_ref[...]) + br_ref[...]
    lane = lax.broadcasted_iota(jnp.int32, logits.shape, 1)
    neg = -jnp.inf
    is_g = (lane >= N_EXPERTS) & (lane < N_EXPERTS + N_GROUPS)
    gl = jnp.where(is_g, logits, neg)
    gmax = jnp.max(gl, axis=1, keepdims=True)
    gsum = jnp.sum(jnp.exp(gl - gmax), axis=1, keepdims=True)
    g_val = 1.0 / gsum
    g_idx = jnp.min(jnp.where(is_g & (gl == gmax), lane, 4 * ROUTER_LANES), axis=1, keepdims=True) - N_EXPERTS
    in_grp = (lane >= g_idx * EXPERTS_PER_GROUP) & (lane < (g_idx + 1) * EXPERTS_PER_GROUP)
    el = jnp.where(in_grp, logits, neg)
    v1 = jnp.max(el, axis=1, keepdims=True)
    i1 = jnp.min(jnp.where(in_grp & (el == v1), lane, 4 * ROUTER_LANES), axis=1, keepdims=True)
    el2 = jnp.where(lane == i1, neg, el)
    v2 = jnp.max(el2, axis=1, keepdims=True)
    i2 = jnp.min(jnp.where(in_grp & (lane != i1) & (el2 == v2), lane, 4 * ROUTER_LANES), axis=1, keepdims=True)
    e2 = jnp.exp(v2 - v1)
    w1 = g_val / (1.0 + e2)
    w2 = g_val * e2 / (1.0 + e2)
    comb_ref[0] = jnp.where(lane == i1, w1, 0.0) + jnp.where(lane == i2, w2, 0.0)


def _merge(x, mod, oa, y, bonus, g, yc, lw, tm):
    B, T, D = x.shape
    row = lambda w: pl.BlockSpec((1, tm, w), lambda b, i: (b, i, 0))
    modspec = pl.BlockSpec((1, N_MOD, D), lambda b, i: (b, 0, 0))
    ins = [(x, row(D)), (mod, modspec), (lw['g1'], None), (lw['wg'], None), (oa, row(D)), (y, row(256)),
           (bonus, row(256)), (g, row(256)), (lw['ln_g'], None), (lw['ln_b'], None), (lw['ones_bd'], None),
           (lw['proj_b'], None), (yc, row(256)), (lw['proj_c'], None), (lw['w_out'], None), (lw['g2n'], None),
           (lw['wr'], None), (lw['br'], None)]
    return pl.pallas_call(
        _merge_kernel,
        grid=(B, T // tm),
        in_specs=[s if s is not None else _full(a) for a, s in ins],
        out_specs=[row(D), row(D), row(ROUTER_LANES)],
        out_shape=[jax.ShapeDtypeStruct((B, T, D), F32), jax.ShapeDtypeStruct((B, T, D), BF16),
                   jax.ShapeDtypeStruct((B, T, ROUTER_LANES), F32)],
        compiler_params=_cp(("parallel", "parallel")),
        name="merge",
    )(*[a for a, _ in ins])


def _moe_kernel(h2_ref, comb_ref, x1_ref, gt_ref, wgu_ref, wd_ref, fg_ref, o_ref, acc, *, final, rb):
    s = pl.program_id(2)
    tm = h2_ref.shape[1]
    nrb = tm // rb
    DE = D_EXPERT

    @pl.when(s == 0)
    def _():
        acc[...] = jnp.zeros_like(acc)

    wgu = wgu_ref[0]
    wd = wd_ref[0]
    gus = {}

    def up(i):
        gus[i] = _dot(h2_ref[0, i * rb:(i + 1) * rb, :], wgu)

    up(0)
    for i in range(nrb):
        if i + 1 < nrb:
            up(i + 1)
        gu = gus.pop(i)
        rows = slice(i * rb, (i + 1) * rb)
        comb = comb_ref[0, rows, :]
        lane = lax.broadcasted_iota(jnp.int32, comb.shape, 1)
        cws = [jnp.sum(jnp.where(lane == 2 * s + j, comb, 0.0), axis=1, keepdims=True) for j in range(2)]
        cw = jnp.concatenate([jnp.broadcast_to(c, (rb, DE)) for c in cws], axis=1)
        gate = gu[:, 0:2 * DE]
        act = gate * _sigmoid(gate) * gu[:, 2 * DE:4 * DE] * cw
        acc[rows, :] += _dot(act, wd)

    @pl.when(s == N_EXPERTS // 2 - 1)
    def _():
        x2 = x1_ref[0] + gt_ref[0] * acc[...]
        o_ref[0] = _rms(x2, fg_ref[...]) if final else x2


def _moe(h2, comb, x1, gt, lw, final_g, tm, final):
    B, T, D = x1.shape
    gt_rows = gt.shape[1]
    gt_spec = (pl.BlockSpec((1, 1, D), lambda b, i, e: (b, 0, 0)) if gt_rows == 1
               else pl.BlockSpec((1, tm, D), lambda b, i, e: (b, i, 0)))
    row = lambda w: pl.BlockSpec((1, tm, w), lambda b, i, e: (b, i, 0))
    return pl.pallas_call(
        functools.partial(_moe_kernel, final=final, rb=min(tm, 256)),
        grid=(B, T // tm, N_EXPERTS // 2),
        in_specs=[row(D), row(ROUTER_LANES), row(D), gt_spec,
                  pl.BlockSpec((1, D, 4 * D_EXPERT), lambda b, i, e: (e, 0, 0)),
                  pl.BlockSpec((1, 2 * D_EXPERT, D), lambda b, i, e: (e, 0, 0)),
                  pl.BlockSpec((1, D), lambda b, i, e: (0, 0))],
        out_specs=row(D),
        out_shape=jax.ShapeDtypeStruct((B, T, D), F32),
        scratch_shapes=[pltpu.VMEM((tm, D), F32)],
        compiler_params=_cp(("parallel", "parallel", "arbitrary")),
        name="moe",
    )(h2, comb, x1, gt, lw['wgu'], lw['wd'], final_g)


def _layer_weights(l, w):
    D = D_MODEL
    w_in = w['w_in'][l]
    c_q, c_kv, c_kr = Q_LORA, Q_LORA + KV_LORA, MLA_IN
    c_rw, c_cv = MLA_IN + RWKV_IN, MLA_IN + RWKV_IN + 2 * CONV_DIM
    half = ROPE_DIM // 2
    swap = np.concatenate([np.arange(half, ROPE_DIM), np.arange(half)])
    wkr = w_in[:, c_kv:c_kr]
    q_up = w['q_up'][l].reshape(Q_LORA, MLA_HEADS, NOPE_DIM + ROPE_DIM)
    q_rope = q_up[:, :, NOPE_DIM:]
    wukq = jnp.zeros((KV_LORA, PAIR_KEYS), F32)
    for hd in range(MLA_HEADS):
        wukq = wukq.at[:, 128 * hd:128 * hd + NOPE_DIM].set(w['w_uk'][l][:, hd, :])
    zl = jnp.zeros((W_LORA, RWKV_DIM), F32)
    head = np.arange(RWKV_DIM) // RWKV_HEAD
    wr = jnp.zeros((D, ROUTER_LANES), F32)
    wr = wr.at[:, 0:N_EXPERTS].set(w['router_expert_w'][l]).at[:, N_EXPERTS:N_EXPERTS + N_GROUPS].set(
        w['router_group_w'][l])
    br = jnp.zeros((1, ROUTER_LANES), F32)
    br = br.at[0, 0:N_EXPERTS].set(w['router_expert_b'][l]).at[0, N_EXPERTS:N_EXPERTS + N_GROUPS].set(
        w['router_group_b'][l])
    r2 = lambda a: a.reshape(1, -1)
    pair_cols = lambda a: jnp.concatenate([a[0::2], a[1::2]], axis=2)
    return dict(
        g1=r2(w['norm1_g'][l]), g2n=r2(w['norm2_g'][l]),
        wq=w_in[:, :c_q].astype(BF16), wkv=w_in[:, c_q:c_kv].astype(BF16),
        wkr=jnp.concatenate([jnp.tile(wkr, (1, 8)), jnp.tile(wkr[:, swap], (1, 8))], axis=1).astype(BF16),
        wrw=w_in[:, c_kr:c_rw].astype(BF16), wcv=w_in[:, c_rw:c_cv].astype(BF16), wg=w_in[:, c_cv:].astype(BF16),
        wkvt=w_in[:, c_q:c_kv].T.astype(BF16), kvgc=w['kv_norm_g'][l].reshape(-1, 1),
        qg=r2(w['q_norm_g'][l]), wqnt=q_up[:, :, :NOPE_DIM].reshape(Q_LORA, -1).T.astype(BF16),
        wukq=wukq.astype(BF16), kvg=r2(w['kv_norm_g'][l]),
        wqrt=q_rope.reshape(Q_LORA, -1).T.astype(BF16), wqrst=q_rope[:, :, swap].reshape(Q_LORA, -1).T.astype(BF16),
        wuvt=jnp.transpose(w['w_uv'][l], (1, 2, 0)).astype(BF16), proj_at=w['proj_a'][l].T.astype(BF16),
        mu=r2(w['rwkv_mu'][l]), w0=r2(w['rwkv_w0'][l]), a0=r2(w['rwkv_a0'][l]),
        w2p=jnp.concatenate([w['rwkv_w2'][l], zl], axis=0).astype(BF16),
        a2p=jnp.concatenate([zl, w['rwkv_a2'][l]], axis=0).astype(BF16),
        g2=w['rwkv_g2'][l].astype(BF16), k_k=r2(w['rwkv_k_k'][l]), k_a=r2(w['rwkv_k_a'][l]),
        r_k=r2(w['rwkv_r_k'][l]), ln_g=r2(w['rwkv_ln_g'][l]), ln_b=r2(w['rwkv_ln_b'][l]),
        ones_bd=jnp.asarray((head[:, None] == head[None, :]).astype(np.float32)).astype(BF16),
        proj_b=w['proj_b'][l].astype(BF16),
        conv_dw=w['conv_dw'][l], conv_dw_b=r2(w['conv_dw_b'][l]), conv_ln_g=r2(w['conv_ln_g'][l]),
        conv_ln_b=r2(w['conv_ln_b'][l]), proj_c=w['proj_c'][l].astype(BF16),
        w_out=w['w_out'][l].astype(BF16), wr=wr, br=br,
        wgu=jnp.concatenate([pair_cols(w['moe_w_gate'][l]), pair_cols(w['moe_w_up'][l])], axis=2).astype(BF16),
        wd=w['moe_w_down'][l].reshape(N_EXPERTS // 2, 2 * D_EXPERT, D).astype(BF16),
    )


def _rope_tables(past, T):
    half = ROPE_DIM // 2
    inv = ROPE_THETA ** (-jnp.arange(half, dtype=F32) / half)
    ang = (past + jnp.arange(T)).astype(F32)[:, None] * inv[None, :]
    cos, sin = jnp.cos(ang), jnp.sin(ang)
    cos8 = jnp.tile(jnp.concatenate([cos, cos], axis=1), (1, 8))
    sin8 = jnp.tile(jnp.concatenate([-sin, sin], axis=1), (1, 8))
    return cos8, sin8, cos8.T, sin8.T


def _row_tile(T, cap):
    return min(T, cap)


def _trunk_layer(x, mod, cache_lat, cache_rope, shift_st, wkv_st, conv_st, lw, final_g, final):
    B, T, D = x.shape
    past = 0 if cache_lat is None else cache_lat.shape[1]
    tm = _row_tile(T, 256)
    lat, kr, keys, latt, qt, prw, glu = _in_proj(x, mod, lw, _rope_tables(past, T), tm)

    _, t_pad, _, s_pad, n_keys = _attn_tiles(T, past)
    if past:
        old = _key_up(cache_lat, jnp.tile(cache_rope, (1, 1, 8)), lw['wukq'], _row_tile(past, 512))
        keys = jnp.concatenate([old, keys], axis=1)
        latt = jnp.concatenate([jnp.swapaxes(cache_lat, 1, 2).astype(BF16), latt], axis=2)
    if s_pad != n_keys:
        keys = jnp.pad(keys, ((0, 0), (0, s_pad - n_keys), (0, 0)))
        latt = jnp.pad(latt, ((0, 0), (0, 0), (0, s_pad - n_keys)))
    if t_pad != T:
        qt = jnp.pad(qt, ((0, 0), (0, 0), (0, 0), (0, t_pad - T)))
    o_a = _attention(qt, keys, latt, lw['wuvt'], lw['proj_at'], past, T)[:, :T]

    nT = T // tm
    first = jnp.zeros((B, 1, RWKV_IN), F32) if shift_st is None else shift_st[:, None, :]
    prev_rows = jnp.concatenate([first, prw[:, tm - 1:T - 1:tm, :]], axis=1).reshape(B, nT, 1, RWKV_IN)
    r, kf, bb, kk, v, lwd, g, bonus = _rwkv_prep(prw, prev_rows, lw, tm)
    Tp = -(-T // WKV_CHUNK) * WKV_CHUNK
    scan_in = [r, kf, bb, kk, v, lwd]
    if Tp != T:
        scan_in = [jnp.pad(a, ((0, 0), (0, Tp - T), (0, 0))) for a in scan_in]
    W = RWKV_DIM
    s0 = jnp.zeros((B, W, W), F32)
    if wkv_st is not None:
        for hd in range(RWKV_HEADS):
            o = hd * RWKV_HEAD
            s0 = s0.at[:, o:o + RWKV_HEAD, o:o + RWKV_HEAD].set(wkv_st[:, hd])
    y, s_bd = _wkv(*scan_in, s0)
    y = y[:, :T]
    wkv_new = jnp.stack([s_bd[:, hd * RWKV_HEAD:(hd + 1) * RWKV_HEAD, hd * RWKV_HEAD:(hd + 1) * RWKV_HEAD]
                         for hd in range(RWKV_HEADS)], axis=1)

    pad_rows = CONV_HALO - (CONV_WIDTH - 1)
    st = jnp.zeros((B, CONV_WIDTH - 1, CONV_DIM), F32) if conv_st is None else conv_st
    yc, conv_tail = _conv(glu, jnp.pad(st, ((0, 0), (pad_rows, 0), (0, 0))), lw, tm)
    conv_new = conv_tail[:, pad_rows:]

    x1, h2, comb = _merge(x, mod, o_a, y, bonus, g, yc, lw, tm)
    if T >= 1024:
        x2 = _moe(h2, comb, x1, mod[:, 5:6], lw, final_g, 1024, final)
    else:
        gt = jnp.broadcast_to(mod[:, 5:6], (B, T, D)).reshape(1, B * T, D)
        x2 = _moe(h2.reshape(1, B * T, D), comb.reshape(1, B * T, ROUTER_LANES), x1.reshape(1, B * T, D), gt, lw,
                  final_g, B * T, final).reshape(B, T, D)
    return x2, (lat, kr, prw[:, -1], wkv_new, conv_new)


def kernel(x_prompt, x_sample, c_prompt, c_sample, cache_kv_latent, cache_k_rope, state_rwkv_shift, state_rwkv_wkv, state_conv, ada_w, ada_b, norm1_g, norm2_g, w_in, q_norm_g, q_up, kv_norm_g, w_uk, w_uv, proj_a, rwkv_mu, rwkv_w0, rwkv_w2, rwkv_a0, rwkv_a2, rwkv_g2, rwkv_k_k, rwkv_k_a, rwkv_r_k, rwkv_ln_g, rwkv_ln_b, proj_b, conv_dw, conv_dw_b, conv_ln_g, conv_ln_b, proj_c, w_out, router_group_w, router_group_b, router_expert_w, router_expert_b, moe_w_gate, moe_w_up, moe_w_down, final_g):
    w = dict(norm1_g=norm1_g, norm2_g=norm2_g, w_in=w_in, q_norm_g=q_norm_g, q_up=q_up, kv_norm_g=kv_norm_g,
             w_uk=w_uk, w_uv=w_uv, proj_a=proj_a, rwkv_mu=rwkv_mu, rwkv_w0=rwkv_w0, rwkv_w2=rwkv_w2,
             rwkv_a0=rwkv_a0, rwkv_a2=rwkv_a2, rwkv_g2=rwkv_g2, rwkv_k_k=rwkv_k_k, rwkv_k_a=rwkv_k_a,
             rwkv_r_k=rwkv_r_k.reshape(rwkv_r_k.shape[0], -1), rwkv_ln_g=rwkv_ln_g, rwkv_ln_b=rwkv_ln_b,
             proj_b=proj_b, conv_dw=conv_dw, conv_dw_b=conv_dw_b, conv_ln_g=conv_ln_g, conv_ln_b=conv_ln_b,
             proj_c=proj_c, w_out=w_out, router_group_w=router_group_w, router_group_b=router_group_b,
             router_expert_w=router_expert_w, router_expert_b=router_expert_b, moe_w_gate=moe_w_gate,
             moe_w_up=moe_w_up, moe_w_down=moe_w_down)
    depth = ada_w.shape[0]
    bp = x_prompt.shape[0]
    D = D_MODEL
    mod_all = _ada_mod(jnp.concatenate([c_prompt, c_sample], axis=0), ada_w, ada_b)
    fg = final_g.reshape(1, D)
    hp, hs = x_prompt, x_sample
    new_p, new_s = [], []
    for l in range(depth):
        lw = _layer_weights(l, w)
        mod_p = mod_all[l, :bp].reshape(bp, N_MOD, D)
        mod_s = mod_all[l, bp:].reshape(-1, N_MOD, D)
        final = l == depth - 1
        hp, st_p = _trunk_layer(hp, mod_p, None, None, None, None, None, lw, fg, final)
        hs, st_s = _trunk_layer(hs, mod_s, cache_kv_latent[l], cache_k_rope[l], state_rwkv_shift[l],
                                state_rwkv_wkv[l], state_conv[l], lw, fg, final)
        new_p.append(st_p)
        new_s.append(st_s)
    stack = lambda states, i: jnp.stack([s[i] for s in states], axis=0)
    return ((hp, hs) + tuple(stack(new_p, i) for i in range(5)) + tuple(stack(new_s, i) for i in range(5)))
```

```python
import functools

import numpy as np
import jax
import jax.numpy as jnp
from jax import lax
from jax.experimental import pallas as pl
from jax.experimental.pallas import tpu as pltpu

F32 = jnp.float32
BF16 = jnp.bfloat16

D_MODEL = 1024
CHUNK = 64
NORM_EPS = 1e-6
MLA_HEADS = 8
Q_LORA = 384
KV_LORA = 256
NOPE_DIM = 64
ROPE_DIM = 32
V_HEAD = 64
ROPE_THETA = 10000.0
SM_SCALE = (NOPE_DIM + ROPE_DIM) ** -0.5
LOG2_E = 1.4426950408889634
RWKV_HEADS = 4
RWKV_HEAD = 64
RWKV_DIM = RWKV_HEADS * RWKV_HEAD
W_LORA = 64
A_LORA = 64
G_LORA = 128
RWKV_IN = 3 * RWKV_DIM + W_LORA + A_LORA + G_LORA
GN_EPS = 64e-5
CONV_DIM = 256
CONV_WIDTH = 31
LN_EPS = 1e-5
MLA_IN = Q_LORA + KV_LORA + ROPE_DIM
N_MOD = 6
N_GROUPS = 4
EXPERTS_PER_GROUP = 4
N_EXPERTS = 16
D_EXPERT = 256
WKV_CHUNK = 64
CONV_HALO = 32
ROUTER_LANES = 128
PAIR_KEYS = 4 * 256
VMEM_LIMIT = 56 * 1024 * 1024


def _cp(sem):
    return pltpu.CompilerParams(dimension_semantics=sem, vmem_limit_bytes=VMEM_LIMIT)


def _full(a):
    nd = a.ndim
    return pl.BlockSpec(a.shape, lambda *_: (0,) * nd)


def _dot(a, b, dims=(((1,), (0,)), ((), ()))):
    return lax.dot_general(a.astype(BF16), b.astype(BF16), dims, preferred_element_type=F32)


def _split(a):
    hi = a.astype(BF16)
    lo = (a - hi.astype(F32)).astype(BF16)
    return hi, lo


def _dot_hl(a, b, dims=(((1,), (0,)), ((), ()))):
    hi, lo = _split(a)
    bb = b.astype(BF16)
    return (lax.dot_general(hi, bb, dims, preferred_element_type=F32)
            + lax.dot_general(lo, bb, dims, preferred_element_type=F32))


def _dot3(a, b, dims=(((1,), (0,)), ((), ()))):
    ah, al = _split(a)
    bh, bl = _split(b)
    d = lambda x, y: lax.dot_general(x, y, dims, preferred_element_type=F32)
    return d(ah, bh) + d(ah, bl) + d(al, bh)


def _sigmoid(x):
    return 0.5 * jnp.tanh(0.5 * x) + 0.5


def _rms(x, g):
    return x * lax.rsqrt(jnp.mean(x * x, axis=-1, keepdims=True) + NORM_EPS) * g


def _ada_kernel(c_ref, w_ref, b_ref, o_ref):
    c = c_ref[...]
    o_ref[0] = _dot(c * _sigmoid(c), w_ref[0]) + b_ref[0]


def _ada_mod(c_all, ada_w, ada_b):
    L, D, C = ada_w.shape
    R = c_all.shape[0]
    nb = C // D
    return pl.pallas_call(
        _ada_kernel,
        grid=(L, nb),
        in_specs=[pl.BlockSpec((R, D), lambda l, j: (0, 0)),
                  pl.BlockSpec((1, D, D), lambda l, j: (l, 0, j)),
                  pl.BlockSpec((1, 1, D), lambda l, j: (l, 0, j))],
        out_specs=pl.BlockSpec((1, R, D), lambda l, j: (l, 0, j)),
        out_shape=jax.ShapeDtypeStruct((L, R, C), F32),
        compiler_params=_cp(("parallel", "parallel")),
        name="ada_mod",
    )(c_all, ada_w, ada_b.reshape(L, 1, C))


_NT = (((1,), (1,)), ((), ()))


def _pair_keys(lat, rope8, wukq):
    lane = lax.broadcasted_iota(jnp.int32, rope8.shape, 1) & 127
    rope_slot = jnp.where((lane >= NOPE_DIM) & (lane < NOPE_DIM + ROPE_DIM), rope8, 0.0)
    return (_dot(lat, wukq) + jnp.concatenate([rope_slot] * (MLA_HEADS // 2), axis=1)).astype(BF16)


def _in_proj_kernel(x_ref, mod_ref, g1_ref, wq_ref, wkv_ref, wkvt_ref, wkr_ref, wrw_ref, wcv_ref, qg_ref, wqnt_ref,
                    wqrt_ref, wqrst_ref, wukq_ref, kvg_ref, kvgc_ref, cos_ref, sin_ref, cost_ref, sint_ref,
                    lat_ref, kr_ref, kq_ref, latt_ref, qt_ref, prw_ref, glu_ref):
    x = x_ref[0]
    mod = mod_ref[0]
    tm = x.shape[0]
    h = _rms(x, g1_ref[...]) * (1.0 + mod[1:2]) + mod[0:1]
    hb = h.astype(BF16)
    qn = _rms(_dot(hb, wq_ref[...]), qg_ref[...]).astype(BF16)
    qnt = _dot(wqnt_ref[...], qn, _NT)
    qrt = _dot(wqrt_ref[...], qn, _NT)
    qrst = _dot(wqrst_ref[...], qn, _NT)
    qrope = qrt * cost_ref[...] + qrst * sint_ref[...]
    zpad = jnp.zeros((128 - NOPE_DIM - ROPE_DIM, tm), F32)
    for hd in range(MLA_HEADS):
        qh = jnp.concatenate([qnt[NOPE_DIM * hd:NOPE_DIM * (hd + 1)], qrope[ROPE_DIM * hd:ROPE_DIM * (hd + 1)],
                              zpad], axis=0)
        qt_ref[0, hd] = (qh * (SM_SCALE * LOG2_E)).astype(BF16)
    lat = _rms(_dot(hb, wkv_ref[...]), kvg_ref[...])
    lat_ref[0] = lat
    pkvt = _dot(wkvt_ref[...], hb, _NT)
    latt = pkvt * lax.rsqrt(jnp.mean(pkvt * pkvt, axis=0, keepdims=True) + NORM_EPS) * kvgc_ref[...]
    latt_ref[0] = latt.astype(BF16)
    pkr = _dot(hb, wkr_ref[...])
    krt = pkr[:, 0:256] * cos_ref[...] + pkr[:, 256:512] * sin_ref[...]
    kr_ref[0] = krt[:, 0:ROPE_DIM]
    kq_ref[0] = _pair_keys(lat, krt, wukq_ref[...])
    prw_ref[0] = _dot(hb, wrw_ref[...])
    pcv = _dot(hb, wcv_ref[...])
    glu_ref[0] = pcv[:, 0:CONV_DIM] * _sigmoid(pcv[:, CONV_DIM:2 * CONV_DIM])


def _in_proj(x, mod, lw, tabs, tm):
    B, T, D = x.shape
    nT = T // tm
    cos8, sin8, cos8t, sin8t = tabs
    row = lambda w: pl.BlockSpec((1, tm, w), lambda b, i: (b, i, 0))
    col = pl.BlockSpec((1, 256, tm), lambda b, i: (b, 0, i))
    args = (lw['g1'], lw['wq'], lw['wkv'], lw['wkvt'], lw['wkr'], lw['wrw'], lw['wcv'], lw['qg'], lw['wqnt'],
            lw['wqrt'], lw['wqrst'], lw['wukq'], lw['kvg'], lw['kvgc'])
    return pl.pallas_call(
        _in_proj_kernel,
        grid=(B, nT),
        in_specs=[row(D), pl.BlockSpec((1, N_MOD, D), lambda b, i: (b, 0, 0))] + [_full(a) for a in args]
                 + [pl.BlockSpec((tm, 256), lambda b, i: (i, 0))] * 2
                 + [pl.BlockSpec((256, tm), lambda b, i: (0, i))] * 2,
        out_specs=[row(KV_LORA), row(ROPE_DIM), row(PAIR_KEYS), col,
                   pl.BlockSpec((1, MLA_HEADS, 128, tm), lambda b, i: (b, 0, 0, i)),
                   row(RWKV_IN), row(CONV_DIM)],
        out_shape=[jax.ShapeDtypeStruct((B, T, KV_LORA), F32),
                   jax.ShapeDtypeStruct((B, T, ROPE_DIM), F32),
                   jax.ShapeDtypeStruct((B, T, PAIR_KEYS), BF16),
                   jax.ShapeDtypeStruct((B, KV_LORA, T), BF16),
                   jax.ShapeDtypeStruct((B, MLA_HEADS, 128, T), BF16),
                   jax.ShapeDtypeStruct((B, T, RWKV_IN), F32),
                   jax.ShapeDtypeStruct((B, T, CONV_DIM), F32)],
        compiler_params=_cp(("parallel", "parallel")),
        name="in_proj",
    )(x, mod, *args, cos8, sin8, cos8t, sin8t)


def _key_up_kernel(lat_ref, rope8_ref, wukq_ref, kq_ref):
    kq_ref[0] = _pair_keys(lat_ref[0], rope8_ref[0], wukq_ref[...])


def _key_up(lat, rope8, wukq, tm):
    B, S, _ = lat.shape
    row = lambda w: pl.BlockSpec((1, tm, w), lambda b, i: (b, i, 0))
    return pl.pallas_call(
        _key_up_kernel,
        grid=(B, S // tm),
        in_specs=[row(KV_LORA), row(256), _full(wukq)],
        out_specs=row(PAIR_KEYS),
        out_shape=jax.ShapeDtypeStruct((B, S, PAIR_KEYS), BF16),
        compiler_params=_cp(("parallel", "parallel")),
        name="key_up",
    )(lat, rope8, wukq)


def _attn_kernel(qi_ref, ki_ref, last_ref, qt_ref, k_ref, vt_ref, wuvt_ref, pat_ref, o_ref,
                 qcat, m_s, l_s, acc, *, tq, tk, past, n_keys):
    p = pl.program_id(1)
    qi = qi_ref[p]
    ki = ki_ref[p]
    H = MLA_HEADS
    NP = H // 2
    nh = tq // 128
    NC = NP * nh

    @pl.when(ki == 0)
    def _():
        qcat[...] = jnp.zeros(qcat.shape, BF16)
        for pr in range(NP):
            for hf in range(nh):
                pos = slice(128 * hf, 128 * (hf + 1))
                qcat[pr * nh + hf, 0:128, 0:128] = qt_ref[0, 2 * pr, :, pos]
                qcat[pr * nh + hf, 128:256, 128:256] = qt_ref[0, 2 * pr + 1, :, pos]
        m_s[...] = jnp.full(m_s.shape, -jnp.inf, F32)
        l_s[...] = jnp.zeros(l_s.shape, F32)
        acc[...] = jnp.zeros(acc.shape, F32)

    def step(bias):
        vt = vt_ref[0]
        scores = {}

        def qk(ch):
            pr = ch // nh
            scores[ch] = jnp.dot(k_ref[0, :, 256 * pr:256 * (pr + 1)], qcat[ch], preferred_element_type=F32)

        qk(0)
        qk(1)
        for ch in range(NC):
            if ch + 2 < NC:
                qk(ch + 2)
            s = scores.pop(ch)
            if bias is not None:
                s = s + bias[ch % nh]
            m_prev = m_s[ch]
            m_new = jnp.maximum(m_prev, jnp.max(s, axis=0, keepdims=True))
            alpha = jnp.exp2(m_prev - m_new)
            pe = jnp.exp2(s - m_new)
            l_s[ch] = alpha * l_s[ch] + jnp.sum(pe, axis=0, keepdims=True)
            acc[ch] = alpha * acc[ch] + jnp.dot(vt, pe.astype(BF16), preferred_element_type=F32)
            m_s[ch] = m_new

    is_last = last_ref[p] == 1

    @pl.when(jnp.logical_not(is_last))
    def _():
        step(None)

    @pl.when(is_last)
    def _():
        kpos = ki * tk + lax.broadcasted_iota(jnp.int32, (tk, 1), 0)
        biases = []
        for hf in range(nh):
            t = 128 * hf + lax.broadcasted_iota(jnp.int32, (1, 128), 1)
            limit = jnp.minimum((((past + qi * tq + t) >> 6) + 1) << 6, n_keys)
            bias = jnp.where(kpos < limit, 0.0, -jnp.inf)
            biases.append(jnp.concatenate([bias, bias], axis=1))
        step(biases)
        heads = []
        for pr in range(NP):
            o = [acc[pr * nh + hf] / l_s[pr * nh + hf] for hf in range(nh)]
            for j in range(2):
                oh = jnp.concatenate([x[:, 128 * j:128 * (j + 1)] for x in o], axis=1)
                heads.append(_dot(wuvt_ref[2 * pr + j], oh))
        out_t = _dot(pat_ref[...], jnp.concatenate(heads, axis=0))
        o_ref[0] = out_t.T


def _attn_tiles(T, past):
    n_keys = past + T
    tq = 256 if T % 256 == 0 else 128
    t_pad = -(-T // tq) * tq
    tk = -(-n_keys // 128) * 128 if n_keys <= 2304 else 512
    s_pad = -(-n_keys // tk) * tk
    return tq, t_pad, tk, s_pad, n_keys


def _attention(qt, keys, latt, wuvt, proj_at, past, T):
    B, H = qt.shape[:2]
    tq, t_pad, tk, s_pad, n_keys = _attn_tiles(T, past)
    assert keys.shape[1] == s_pad and latt.shape[2] == s_pad and qt.shape[3] == t_pad and past % CHUNK == 0
    qi, ki, last = [], [], []
    for i in range(t_pad // tq):
        lim = min(((past + (i + 1) * tq - 1) // CHUNK + 1) * CHUNK, n_keys)
        nk = -(-lim // tk)
        for j in range(nk):
            qi.append(i), ki.append(j), last.append(int(j == nk - 1))
    npairs = len(qi)
    tabs = [jnp.asarray(np.asarray(a, np.int32)) for a in (qi, ki, last)]
    nc = (H // 2) * (tq // 128)
    grid_spec = pltpu.PrefetchScalarGridSpec(
        num_scalar_prefetch=3,
        grid=(B, npairs),
        in_specs=[pl.BlockSpec((1, H, 128, tq), lambda b, p, qi, ki, la: (b, 0, 0, qi[p])),
                  pl.BlockSpec((1, tk, PAIR_KEYS), lambda b, p, qi, ki, la: (b, ki[p], 0)),
                  pl.BlockSpec((1, KV_LORA, tk), lambda b, p, qi, ki, la: (b, 0, ki[p])),
                  pl.BlockSpec(wuvt.shape, lambda b, p, qi, ki, la: (0, 0, 0)),
                  pl.BlockSpec(proj_at.shape, lambda b, p, qi, ki, la: (0, 0))],
        out_specs=pl.BlockSpec((1, tq, D_MODEL), lambda b, p, qi, ki, la: (b, qi[p], 0)),
        scratch_shapes=[pltpu.VMEM((nc, 256, 256), BF16), pltpu.VMEM((nc, 1, 256), F32),
                        pltpu.VMEM((nc, 1, 256), F32), pltpu.VMEM((nc, KV_LORA, 256), F32)])
    return pl.pallas_call(
        functools.partial(_attn_kernel, tq=tq, tk=tk, past=past, n_keys=n_keys),
        grid_spec=grid_spec,
        out_shape=jax.ShapeDtypeStruct((B, t_pad, D_MODEL), F32),
        compiler_params=_cp(("parallel", "arbitrary")),
        name="attention",
    )(*tabs, qt, keys, latt, wuvt, proj_at)


def _head_sum(x, ones_bd):
    return _dot_hl(x, ones_bd)


def _rwkv_features(pr, prev, wts):
    mu, w0, w2p, a0, a2p, g2, k_k, k_a, r_k, ones_bd = wts
    xs = pr + (prev - pr) * mu
    r = xs[:, 0:256]
    k = xs[:, 256:512]
    v = xs[:, 512:768]
    wa = xs[:, 768:896]
    gl = xs[:, 896:1024]
    nz = -(w0 + _dot(jnp.tanh(wa), w2p))
    softplus = jnp.maximum(nz, 0.0) + jnp.log(1.0 + jnp.exp(-jnp.abs(nz)))
    lw = -jnp.exp(-softplus - 0.5)
    a = _sigmoid(a0 + _dot(wa, a2p))
    g = _dot(_sigmoid(gl), g2)
    kk = k * k_k
    kk = kk * lax.rsqrt(_head_sum(kk * kk, ones_bd) + 1e-12)
    kf = k * (1.0 + (a - 1.0) * k_a)
    bonus = _head_sum(r * kf * r_k, ones_bd) * v
    return r, kf, kk * a, kk, v, lw, g, bonus


def _wkv_kernel(prw_ref, shift_ref, s0_ref, mu_ref, w0_ref, w2_ref, a0_ref, a2_ref, g2_ref, kk_ref, ka_ref, rk_ref,
                ones_ref, lng_ref, lnb_ref, yb_ref, sout_ref, S, carry, *, nb, G, n_valid):
    L = WKV_CHUNK
    W = RWKV_DIM

    @pl.when(pl.program_id(0) == 0)
    def _():
        S[...] = s0_ref[...]
        carry[...] = shift_ref[...]

    wts = tuple(ref[...] for ref in (mu_ref, w0_ref, w2_ref, a0_ref, a2_ref, g2_ref, kk_ref, ka_ref, rk_ref, ones_ref))
    ones_bd = wts[-1]
    ln_g = lng_ref[...]
    ln_b = lnb_ref[...]
    ri = lax.broadcasted_iota(jnp.int32, (W, W), 0)
    ci = lax.broadcasted_iota(jnp.int32, (W, W), 1)
    same = (ri >> 6) == (ci >> 6)
    strict = same & ((ci & 63) < (ri & 63))
    incl = same & ((ci & 63) <= (ri & 63))
    eye = (ri == ci).astype(F32)
    tri = (lax.broadcasted_iota(jnp.int32, (L, L), 1) <= lax.broadcasted_iota(jnp.int32, (L, L), 0)).astype(BF16)
    lane_head = lax.broadcasted_iota(jnp.int32, (L, W), 1) >> 6
    row = lax.broadcasted_iota(jnp.int32, (L, 1), 0)
    nt = (((1,), (1,)), ((), ()))
    tn = (((0,), (0,)), ((), ()))

    def stack(x):
        return jnp.concatenate([jnp.where(lane_head == hd, x, 0.0) for hd in range(RWKV_HEADS)], axis=0)

    def collapse(z):
        return z[0:L] + z[L:2 * L] + z[2 * L:3 * L] + z[3 * L:4 * L]

    def features(i):
        pr = prw_ref[i]
        prev = jnp.where(row == 0, carry[i], pltpu.roll(pr, 1, 0))
        carry[i] = pr[L - 1:L, :]
        f = _rwkv_features(pr, prev, wts)
        if n_valid < L:
            f = tuple(jnp.where(row < n_valid, x, 0.0) for x in f[:6]) + f[6:]
        return f

    def body(it, _):
        ids = [it * G + g for g in range(G)]
        each = lambda f, *xs: [f(*a) for a in zip(*xs)]
        r, kf, bb, kk, v, ll, gate, bonus = zip(*[features(i) for i in ids])
        s_prev = [S[i] for i in ids]
        split = each(_split, ll)
        c = each(lambda hl: jnp.dot(tri, hl[0], preferred_element_type=F32)
                 + jnp.dot(tri, hl[1], preferred_element_type=F32), split)
        c_last = each(lambda x: x[L - 1:L, :], c)
        e_neg = each(lambda x: jnp.exp(-x), c)
        e_end = each(lambda x, y: jnp.exp(y - x), c, c_last)
        rt = each(lambda x, y: stack(x * jnp.exp(y)), r, c)
        kkt = each(lambda x, y, z: stack(x * jnp.exp(y - z)), kk, c, ll)
        bt = each(lambda x, y: stack(x * y), bb, e_neg)
        kt = each(lambda x, y: stack(x * y), kf, e_neg)
        a_m = each(lambda x, y: jnp.where(strict, _dot(x, y, nt), 0.0), kkt, bt)
        b_m = each(lambda x, y: jnp.where(strict, _dot(x, y, nt), 0.0), kkt, kt)
        m1 = each(lambda x, y: jnp.where(incl, _dot(x, y, nt), 0.0), rt, bt)
        m2 = each(lambda x, y: jnp.where(incl, _dot(x, y, nt), 0.0), rt, kt)
        t_m = each(lambda x: eye - x, a_m)
        a_p = a_m
        for _ in range(5):
            a_p = each(lambda x: _dot(x, x), a_p)
            t_m = each(lambda x, y: x + _dot(x, y), t_m, a_p)
        vs = each(stack, v)
        w1 = each(lambda x, y: collapse(_dot(x, y)), t_m, kkt)
        bv = each(_dot, b_m, vs)
        w2 = each(lambda x, y: collapse(_dot(x, y)), t_m, bv)
        u = each(lambda x, y, z: -(_dot(x, y, nt) + z), w1, s_prev, w2)
        y0 = each(lambda x, y: _dot(collapse(x), y, nt), rt, s_prev)
        y1 = each(lambda a, b_, c_, d: collapse(_dot(a, stack(b_)) + _dot(c_, d)), m1, u, m2, vs)
        upd = each(lambda a, b_, c_, d, e: _dot(a, b_ * e, tn) + _dot(c_, d * e, tn), u, bb, v, kf, e_end)
        for g, i in enumerate(ids):
            S[i] = s_prev[g] * jnp.exp(c_last[g]) + jnp.where(same, upd[g], 0.0)
            y = y0[g] + y1[g]
            mean = _head_sum(y, ones_bd) * (1.0 / RWKV_HEAD)
            yc = y - mean
            var = _head_sum(yc * yc, ones_bd) * (1.0 / RWKV_HEAD)
            yb = (yc * lax.rsqrt(var + GN_EPS) * ln_g + ln_b + bonus[g]) * gate[g]
            yb_ref[i] = yb.astype(BF16)
        return 0

    lax.fori_loop(0, nb // G, body, 0)
    sout_ref[...] = S[...]


def _wkv(prw, shift0, s0, lw, n_valid):
    B, T, _ = prw.shape
    L = WKV_CHUNK
    W = RWKV_DIM
    assert T % L == 0 and (n_valid == L or T == L)
    args = (lw['mu'], lw['w0'], lw['w2p'], lw['a0'], lw['a2p'], lw['g2'], lw['k_k'], lw['k_a'], lw['r_k'],
            lw['ones_bd'], lw['ln_g'], lw['ln_b'])
    st = pl.BlockSpec((B, W, W), lambda c: (0, 0, 0))
    return pl.pallas_call(
        functools.partial(_wkv_kernel, nb=B, G=4 if B % 4 == 0 else 1, n_valid=n_valid),
        grid=(T // L,),
        in_specs=[pl.BlockSpec((B, L, RWKV_IN), lambda c: (0, c, 0)), _full(shift0), st] + [_full(a) for a in args],
        out_specs=[pl.BlockSpec((B, L, W), lambda c: (0, c, 0)), st],
        out_shape=[jax.ShapeDtypeStruct((B, T, W), BF16), jax.ShapeDtypeStruct((B, W, W), F32)],
        scratch_shapes=[pltpu.VMEM((B, W, W), F32), pltpu.VMEM((B, 1, RWKV_IN), F32)],
        compiler_params=_cp(("arbitrary",)),
        name="wkv",
    )(prw, shift0, s0, *args)


def _conv_module(ext, glu, st_ref, dw_ref, dwb, lng, lnb, new_ref, first, tm):
    @pl.when(first)
    def _():
        ext[0:CONV_HALO, :] = st_ref[0]
    ext[CONV_HALO:CONV_HALO + tm, :] = glu
    off = CONV_HALO - (CONV_WIDTH - 1)
    ext_v = ext[...]
    n = CONV_HALO + tm
    acc = jnp.zeros((tm, CONV_DIM), F32)
    for res in range(8):
        taps = [w for w in range(CONV_WIDTH) if (off + w) % 8 == res]
        if not taps:
            continue
        sh = ext_v if res == 0 else pltpu.roll(ext_v, n - res, 0)
        for w in taps:
            base = off + w - res
            acc = acc + sh[base:base + tm, :] * dw_ref[w:w + 1, :]
    y = acc + dwb
    mu = jnp.mean(y, axis=-1, keepdims=True)
    var = jnp.mean(jnp.square(y - mu), axis=-1, keepdims=True)
    y = (y - mu) * lax.rsqrt(var + LN_EPS) * lng + lnb
    tail = ext[tm:tm + CONV_HALO, :]
    new_ref[0] = tail
    ext[0:CONV_HALO, :] = tail
    return y * _sigmoid(y)


def _merge_kernel(x_ref, mod_ref, g1_ref, wg_ref, oa_ref, yb_ref, pb_ref, glu_ref, st_ref, dw_ref, dwb_ref, clg_ref,
                  clb_ref, pc_ref, wo_ref, g2_ref, wr_ref, br_ref, x1_ref, h2_ref, comb_ref, cnew_ref, ext):
    x = x_ref[0]
    mod = mod_ref[0]
    tm = x.shape[0]
    yc = _conv_module(ext, glu_ref[0], st_ref, dw_ref, dwb_ref[...], clg_ref[...], clb_ref[...], cnew_ref,
                      pl.program_id(1) == 0, tm)
    hb = (_rms(x, g1_ref[...]) * (1.0 + mod[1:2]) + mod[0:1]).astype(BF16)
    D = D_MODEL
    merged = _sigmoid(_dot(hb, wg_ref[:, 0:D])) * oa_ref[0]
    merged = merged + _sigmoid(_dot(hb, wg_ref[:, D:2 * D])) * _dot(yb_ref[0], pb_ref[...])
    merged = merged + _sigmoid(_dot(hb, wg_ref[:, 2 * D:3 * D])) * _dot(yc, pc_ref[...])
    x1 = x + mod[2:3] * _dot(merged, wo_ref[...])
    x1_ref[0] = x1
    h2 = _rms(x1, g2_ref[...]) * (1.0 + mod[4:5]) + mod[3:4]
    h2_ref[0] = h2.astype(BF16)
    logits = _dot3(h2, wr_ref[...]) + br_ref[...]
    lane = lax.broadcasted_iota(jnp.int32, logits.shape, 1)
    neg = -jnp.inf
    is_g = (lane >= N_EXPERTS) & (lane < N_EXPERTS + N_GROUPS)
    gl = jnp.where(is_g, logits, neg)
    gmax = jnp.max(gl, axis=1, keepdims=True)
    gsum = jnp.sum(jnp.exp(gl - gmax), axis=1, keepdims=True)
    g_val = 1.0 / gsum
    g_idx = jnp.min(jnp.where(is_g & (gl == gmax), lane, 4 * ROUTER_LANES), axis=1, keepdims=True) - N_EXPERTS
    in_grp = (lane >= g_idx * EXPERTS_PER_GROUP) & (lane < (g_idx + 1) * EXPERTS_PER_GROUP)
    el = jnp.where(in_grp, logits, neg)
    v1 = jnp.max(el, axis=1, keepdims=True)
    i1 = jnp.min(jnp.where(in_grp & (el == v1), lane, 4 * ROUTER_LANES), axis=1, keepdims=True)
    el2 = jnp.where(lane == i1, neg, el)
    v2 = jnp.max(el2, axis=1, keepdims=True)
    i2 = jnp.min(jnp.where(in_grp & (lane != i1) & (el2 == v2), lane, 4 * ROUTER_LANES), axis=1, keepdims=True)
    e2 = jnp.exp(v2 - v1)
    w1 = g_val / (1.0 + e2)
    w2 = g_val * e2 / (1.0 + e2)
    comb_ref[0] = jnp.where(lane == i1, w1, 0.0) + jnp.where(lane == i2, w2, 0.0)


def _merge(x, mod, oa, yb, glu, st_pad, lw, tm):
    B, T, D = x.shape
    row = lambda w: pl.BlockSpec((1, tm, w), lambda b, i: (b, i, 0))
    modspec = pl.BlockSpec((1, N_MOD, D), lambda b, i: (b, 0, 0))
    halo = pl.BlockSpec((1, CONV_HALO, CONV_DIM), lambda b, i: (b, 0, 0))
    ins = [(x, row(D)), (mod, modspec), (lw['g1'], None), (lw['wg'], None), (oa, row(D)), (yb, row(256)),
           (lw['proj_b'], None), (glu, row(CONV_DIM)), (st_pad, halo), (lw['conv_dw'], None),
           (lw['conv_dw_b'], None), (lw['conv_ln_g'], None), (lw['conv_ln_b'], None), (lw['proj_c'], None),
           (lw['w_out'], None), (lw['g2n'], None), (lw['wr'], None), (lw['br'], None)]
    return pl.pallas_call(
        _merge_kernel,
        grid=(B, T // tm),
        in_specs=[s if s is not None else _full(a) for a, s in ins],
        out_specs=[row(D), row(D), row(ROUTER_LANES), halo],
        out_shape=[jax.ShapeDtypeStruct((B, T, D), F32), jax.ShapeDtypeStruct((B, T, D), BF16),
                   jax.ShapeDtypeStruct((B, T, ROUTER_LANES), F32),
                   jax.ShapeDtypeStruct((B, CONV_HALO, CONV_DIM), F32)],
        scratch_shapes=[pltpu.VMEM((CONV_HALO + tm, CONV_DIM), F32)],
        compiler_params=_cp(("parallel", "arbitrary")),
        name="merge",
    )(*[a for a, _ in ins])


def _moe_kernel(h2_ref, comb_ref, x1_ref, gt_ref, wgu_ref, wd_ref, fg_ref, o_ref, acc, *, final, rb):
    s = pl.program_id(2)
    tm = h2_ref.shape[1]
    nrb = tm // rb
    DE = D_EXPERT

    @pl.when(s == 0)
    def _():
        acc[...] = jnp.zeros_like(acc)

    wgu = wgu_ref[0]
    wd = wd_ref[0]
    gus = {}

    def up(i):
        gus[i] = _dot(h2_ref[0, i * rb:(i + 1) * rb, :], wgu)

    up(0)
    for i in range(nrb):
        if i + 1 < nrb:
            up(i + 1)
        gu = gus.pop(i)
        rows = slice(i * rb, (i + 1) * rb)
        comb = comb_ref[0, rows, :]
        lane = lax.broadcasted_iota(jnp.int32, comb.shape, 1)
        cws = [jnp.sum(jnp.where(lane == 2 * s + j, comb, 0.0), axis=1, keepdims=True) for j in range(2)]
        cw = jnp.concatenate([jnp.broadcast_to(c, (rb, DE)) for c in cws], axis=1)
        gate = gu[:, 0:2 * DE]
        act = gate * _sigmoid(gate) * gu[:, 2 * DE:4 * DE] * cw
        acc[rows, :] += _dot(act, wd)

    @pl.when(s == N_EXPERTS // 2 - 1)
    def _():
        x2 = x1_ref[0] + gt_ref[0] * acc[...]
        o_ref[0] = _rms(x2, fg_ref[...]) if final else x2


def _moe(h2, comb, x1, gt, lw, final_g, tm, final):
    B, T, D = x1.shape
    gt_rows = gt.shape[1]
    gt_spec = (pl.BlockSpec((1, 1, D), lambda b, i, e: (b, 0, 0)) if gt_rows == 1
               else pl.BlockSpec((1, tm, D), lambda b, i, e: (b, i, 0)))
    row = lambda w: pl.BlockSpec((1, tm, w), lambda b, i, e: (b, i, 0))
    return pl.pallas_call(
        functools.partial(_moe_kernel, final=final, rb=min(tm, 256)),
        grid=(B, T // tm, N_EXPERTS // 2),
        in_specs=[row(D), row(ROUTER_LANES), row(D), gt_spec,
                  pl.BlockSpec((1, D, 4 * D_EXPERT), lambda b, i, e: (e, 0, 0)),
                  pl.BlockSpec((1, 2 * D_EXPERT, D), lambda b, i, e: (e, 0, 0)),
                  pl.BlockSpec((1, D), lambda b, i, e: (0, 0))],
        out_specs=row(D),
        out_shape=jax.ShapeDtypeStruct((B, T, D), F32),
        scratch_shapes=[pltpu.VMEM((tm, D), F32)],
        compiler_params=_cp(("parallel", "parallel", "arbitrary")),
        name="moe",
    )(h2, comb, x1, gt, lw['wgu'], lw['wd'], final_g)


def _layer_weights(l, w):
    D = D_MODEL
    w_in = w['w_in'][l]
    c_q, c_kv, c_kr = Q_LORA, Q_LORA + KV_LORA, MLA_IN
    c_rw, c_cv = MLA_IN + RWKV_IN, MLA_IN + RWKV_IN + 2 * CONV_DIM
    half = ROPE_DIM // 2
    swap = np.concatenate([np.arange(half, ROPE_DIM), np.arange(half)])
    wkr = w_in[:, c_kv:c_kr]
    q_up = w['q_up'][l].reshape(Q_LORA, MLA_HEADS, NOPE_DIM + ROPE_DIM)
    q_rope = q_up[:, :, NOPE_DIM:]
    wukq = jnp.zeros((KV_LORA, PAIR_KEYS), F32)
    for hd in range(MLA_HEADS):
        wukq = wukq.at[:, 128 * hd:128 * hd + NOPE_DIM].set(w['w_uk'][l][:, hd, :])
    zl = jnp.zeros((W_LORA, RWKV_DIM), F32)
    head = np.arange(RWKV_DIM) // RWKV_HEAD
    wr = jnp.zeros((D, ROUTER_LANES), F32)
    wr = wr.at[:, 0:N_EXPERTS].set(w['router_expert_w'][l]).at[:, N_EXPERTS:N_EXPERTS + N_GROUPS].set(
        w['router_group_w'][l])
    br = jnp.zeros((1, ROUTER_LANES), F32)
    br = br.at[0, 0:N_EXPERTS].set(w['router_expert_b'][l]).at[0, N_EXPERTS:N_EXPERTS + N_GROUPS].set(
        w['router_group_b'][l])
    r2 = lambda a: a.reshape(1, -1)
    pair_cols = lambda a: jnp.concatenate([a[0::2], a[1::2]], axis=2)
    return dict(
        g1=r2(w['norm1_g'][l]), g2n=r2(w['norm2_g'][l]),
        wq=w_in[:, :c_q].astype(BF16), wkv=w_in[:, c_q:c_kv].astype(BF16),
        wkr=jnp.concatenate([jnp.tile(wkr, (1, 8)), jnp.tile(wkr[:, swap], (1, 8))], axis=1).astype(BF16),
        wrw=w_in[:, c_kr:c_rw].astype(BF16), wcv=w_in[:, c_rw:c_cv].astype(BF16), wg=w_in[:, c_cv:].astype(BF16),
        wkvt=w_in[:, c_q:c_kv].T.astype(BF16), kvgc=w['kv_norm_g'][l].reshape(-1, 1),
        qg=r2(w['q_norm_g'][l]), wqnt=q_up[:, :, :NOPE_DIM].reshape(Q_LORA, -1).T.astype(BF16),
        wukq=wukq.astype(BF16), kvg=r2(w['kv_norm_g'][l]),
        wqrt=q_rope.reshape(Q_LORA, -1).T.astype(BF16), wqrst=q_rope[:, :, swap].reshape(Q_LORA, -1).T.astype(BF16),
        wuvt=jnp.transpose(w['w_uv'][l], (1, 2, 0)).astype(BF16), proj_at=w['proj_a'][l].T.astype(BF16),
        mu=r2(w['rwkv_mu'][l]), w0=r2(w['rwkv_w0'][l]), a0=r2(w['rwkv_a0'][l]),
        w2p=jnp.concatenate([w['rwkv_w2'][l], zl], axis=0).astype(BF16),
        a2p=jnp.concatenate([zl, w['rwkv_a2'][l]], axis=0).astype(BF16),
        g2=w['rwkv_g2'][l].astype(BF16), k_k=r2(w['rwkv_k_k'][l]), k_a=r2(w['rwkv_k_a'][l]),
        r_k=r2(w['rwkv_r_k'][l]), ln_g=r2(w['rwkv_ln_g'][l]), ln_b=r2(w['rwkv_ln_b'][l]),
        ones_bd=jnp.asarray((head[:, None] == head[None, :]).astype(np.float32)).astype(BF16),
        proj_b=w['proj_b'][l].astype(BF16),
        conv_dw=w['conv_dw'][l], conv_dw_b=r2(w['conv_dw_b'][l]), conv_ln_g=r2(w['conv_ln_g'][l]),
        conv_ln_b=r2(w['conv_ln_b'][l]), proj_c=w['proj_c'][l].astype(BF16),
        w_out=w['w_out'][l].astype(BF16), wr=wr, br=br,
        wgu=jnp.concatenate([pair_cols(w['moe_w_gate'][l]), pair_cols(w['moe_w_up'][l])], axis=2).astype(BF16),
        wd=w['moe_w_down'][l].reshape(N_EXPERTS // 2, 2 * D_EXPERT, D).astype(BF16),
    )


def _rope_tables(past, T):
    half = ROPE_DIM // 2
    inv = ROPE_THETA ** (-jnp.arange(half, dtype=F32) / half)
    ang = (past + jnp.arange(T)).astype(F32)[:, None] * inv[None, :]
    cos, sin = jnp.cos(ang), jnp.sin(ang)
    cos8 = jnp.tile(jnp.concatenate([cos, cos], axis=1), (1, 8))
    sin8 = jnp.tile(jnp.concatenate([-sin, sin], axis=1), (1, 8))
    return cos8, sin8, cos8.T, sin8.T


def _row_tile(T, cap):
    return min(T, cap)


def _trunk_layer(x, mod, cache_lat, cache_rope, shift_st, wkv_st, conv_st, lw, final_g, final):
    B, T, D = x.shape
    past = 0 if cache_lat is None else cache_lat.shape[1]
    tm = _row_tile(T, 256)
    lat, kr, keys, latt, qt, prw, glu = _in_proj(x, mod, lw, _rope_tables(past, T), tm)

    _, t_pad, _, s_pad, n_keys = _attn_tiles(T, past)
    if past:
        old = _key_up(cache_lat, jnp.tile(cache_rope, (1, 1, 8)), lw['wukq'], _row_tile(past, 512))
        keys = jnp.concatenate([old, keys], axis=1)
        latt = jnp.concatenate([jnp.swapaxes(cache_lat, 1, 2).astype(BF16), latt], axis=2)
    if s_pad != n_keys:
        keys = jnp.pad(keys, ((0, 0), (0, s_pad - n_keys), (0, 0)))
        latt = jnp.pad(latt, ((0, 0), (0, 0), (0, s_pad - n_keys)))
    if t_pad != T:
        qt = jnp.pad(qt, ((0, 0), (0, 0), (0, 0), (0, t_pad - T)))
    o_a = _attention(qt, keys, latt, lw['wuvt'], lw['proj_at'], past, T)[:, :T]

    L = WKV_CHUNK
    Tp = -(-T // L) * L
    prw_p = prw if Tp == T else jnp.pad(prw, ((0, 0), (0, Tp - T), (0, 0)))
    shift0 = jnp.zeros((B, 1, RWKV_IN), F32) if shift_st is None else shift_st[:, None, :]
    W = RWKV_DIM
    s0 = jnp.zeros((B, W, W), F32)
    if wkv_st is not None:
        for hd in range(RWKV_HEADS):
            o = hd * RWKV_HEAD
            s0 = s0.at[:, o:o + RWKV_HEAD, o:o + RWKV_HEAD].set(wkv_st[:, hd])
    yb, s_bd = _wkv(prw_p, shift0, s0, lw, L if Tp == T else T)
    yb = yb[:, :T]
    wkv_new = jnp.stack([s_bd[:, hd * RWKV_HEAD:(hd + 1) * RWKV_HEAD, hd * RWKV_HEAD:(hd + 1) * RWKV_HEAD]
                         for hd in range(RWKV_HEADS)], axis=1)

    pad_rows = CONV_HALO - (CONV_WIDTH - 1)
    st = jnp.zeros((B, CONV_WIDTH - 1, CONV_DIM), F32) if conv_st is None else conv_st
    x1, h2, comb, conv_tail = _merge(x, mod, o_a, yb, glu, jnp.pad(st, ((0, 0), (pad_rows, 0), (0, 0))), lw, tm)
    conv_new = conv_tail[:, pad_rows:]

    if T >= 1024:
        x2 = _moe(h2, comb, x1, mod[:, 5:6], lw, final_g, 1024, final)
    else:
        gt = jnp.broadcast_to(mod[:, 5:6], (B, T, D)).reshape(1, B * T, D)
        x2 = _moe(h2.reshape(1, B * T, D), comb.reshape(1, B * T, ROUTER_LANES), x1.reshape(1, B * T, D), gt, lw,
                  final_g, B * T, final).reshape(B, T, D)
    return x2, (lat, kr, prw[:, -1], wkv_new, conv_new)


def kernel(x_prompt, x_sample, c_prompt, c_sample, cache_kv_latent, cache_k_rope, state_rwkv_shift, state_rwkv_wkv, state_conv, ada_w, ada_b, norm1_g, norm2_g, w_in, q_norm_g, q_up, kv_norm_g, w_uk, w_uv, proj_a, rwkv_mu, rwkv_w0, rwkv_w2, rwkv_a0, rwkv_a2, rwkv_g2, rwkv_k_k, rwkv_k_a, rwkv_r_k, rwkv_ln_g, rwkv_ln_b, proj_b, conv_dw, conv_dw_b, conv_ln_g, conv_ln_b, proj_c, w_out, router_group_w, router_group_b, router_expert_w, router_expert_b, moe_w_gate, moe_w_up, moe_w_down, final_g):
    w = dict(norm1_g=norm1_g, norm2_g=norm2_g, w_in=w_in, q_norm_g=q_norm_g, q_up=q_up, kv_norm_g=kv_norm_g,
             w_uk=w_uk, w_uv=w_uv, proj_a=proj_a, rwkv_mu=rwkv_mu, rwkv_w0=rwkv_w0, rwkv_w2=rwkv_w2,
             rwkv_a0=rwkv_a0, rwkv_a2=rwkv_a2, rwkv_g2=rwkv_g2, rwkv_k_k=rwkv_k_k, rwkv_k_a=rwkv_k_a,
             rwkv_r_k=rwkv_r_k.reshape(rwkv_r_k.shape[0], -1), rwkv_ln_g=rwkv_ln_g, rwkv_ln_b=rwkv_ln_b,
             proj_b=proj_b, conv_dw=conv_dw, conv_dw_b=conv_dw_b, conv_ln_g=conv_ln_g, conv_ln_b=conv_ln_b,
             proj_c=proj_c, w_out=w_out, router_group_w=router_group_w, router_group_b=router_group_b,
             router_expert_w=router_expert_w, router_expert_b=router_expert_b, moe_w_gate=moe_w_gate,
             moe_w_up=moe_w_up, moe_w_down=moe_w_down)
    depth = ada_w.shape[0]
    bp = x_prompt.shape[0]
    D = D_MODEL
    mod_all = _ada_mod(jnp.concatenate([c_prompt, c_sample], axis=0), ada_w, ada_b)
    fg = final_g.reshape(1, D)
    hp, hs = x_prompt, x_sample
    new_p, new_s = [], []
    for l in range(depth):
        lw = _layer_weights(l, w)
        mod_p = mod_all[l, :bp].reshape(bp, N_MOD, D)
        mod_s = mod_all[l, bp:].reshape(-1, N_MOD, D)
        final = l == depth - 1
        hp, st_p = _trunk_layer(hp, mod_p, None, None, None, None, None, lw, fg, final)
        hs, st_s = _trunk_layer(hs, mod_s, cache_kv_latent[l], cache_k_rope[l], state_rwkv_shift[l],
                                state_rwkv_wkv[l], state_conv[l], lw, fg, final)
        new_p.append(st_p)
        new_s.append(st_s)
    stack = lambda states, i: jnp.stack([s[i] for s in states], axis=0)
    return ((hp, hs) + tuple(stack(new_p, i) for i in range(5)) + tuple(stack(new_s, i) for i in range(5)))
```

```python
import functools

import numpy as np
import jax
import jax.numpy as jnp
from jax import lax
from jax.experimental import pallas as pl
from jax.experimental.pallas import tpu as pltpu

F32 = jnp.float32
BF16 = jnp.bfloat16

D_MODEL = 1024
CHUNK = 64
NORM_EPS = 1e-6
MLA_HEADS = 8
Q_LORA = 384
KV_LORA = 256
NOPE_DIM = 64
ROPE_DIM = 32
V_HEAD = 64
ROPE_THETA = 10000.0
SM_SCALE = (NOPE_DIM + ROPE_DIM) ** -0.5
LOG2_E = 1.4426950408889634
RWKV_HEADS = 4
RWKV_HEAD = 64
RWKV_DIM = RWKV_HEADS * RWKV_HEAD
W_LORA = 64
A_LORA = 64
G_LORA = 128
RWKV_IN = 3 * RWKV_DIM + W_LORA + A_LORA + G_LORA
GN_EPS = 64e-5
CONV_DIM = 256
CONV_WIDTH = 31
LN_EPS = 1e-5
MLA_IN = Q_LORA + KV_LORA + ROPE_DIM
N_MOD = 6
N_GROUPS = 4
EXPERTS_PER_GROUP = 4
N_EXPERTS = 16
D_EXPERT = 256
WKV_CHUNK = 64
CONV_HALO = 32
ROUTER_LANES = 128
PAIR_KEYS = 4 * 256
VT_ROWS = KV_LORA + 16
QK_AHEAD = 5
VMEM_LIMIT = 56 * 1024 * 1024


def _cp(sem):
    return pltpu.CompilerParams(dimension_semantics=sem, vmem_limit_bytes=VMEM_LIMIT)


def _full(a):
    nd = a.ndim
    return pl.BlockSpec(a.shape, lambda *_: (0,) * nd)


def _dot(a, b, dims=(((1,), (0,)), ((), ()))):
    return lax.dot_general(a.astype(BF16), b.astype(BF16), dims, preferred_element_type=F32)


def _split(a):
    hi = a.astype(BF16)
    lo = (a - hi.astype(F32)).astype(BF16)
    return hi, lo


def _dot_hl(a, b, dims=(((1,), (0,)), ((), ()))):
    hi, lo = _split(a)
    bb = b.astype(BF16)
    return (lax.dot_general(hi, bb, dims, preferred_element_type=F32)
            + lax.dot_general(lo, bb, dims, preferred_element_type=F32))


def _dot3(a, b, dims=(((1,), (0,)), ((), ()))):
    ah, al = _split(a)
    bh, bl = _split(b)
    d = lambda x, y: lax.dot_general(x, y, dims, preferred_element_type=F32)
    return d(ah, bh) + d(ah, bl) + d(al, bh)


def _sigmoid(x):
    return 0.5 * jnp.tanh(0.5 * x) + 0.5


def _rms(x, g):
    return x * lax.rsqrt(jnp.mean(x * x, axis=-1, keepdims=True) + NORM_EPS) * g


def _ada_kernel(c_ref, w_ref, b_ref, o_ref):
    c = c_ref[...]
    o_ref[0] = _dot(c * _sigmoid(c), w_ref[0]) + b_ref[0]


def _ada_mod(c_all, ada_w, ada_b):
    L, D, C = ada_w.shape
    R = c_all.shape[0]
    nb = C // D
    return pl.pallas_call(
        _ada_kernel,
        grid=(L, nb),
        in_specs=[pl.BlockSpec((R, D), lambda l, j: (0, 0)),
                  pl.BlockSpec((1, D, D), lambda l, j: (l, 0, j)),
                  pl.BlockSpec((1, 1, D), lambda l, j: (l, 0, j))],
        out_specs=pl.BlockSpec((1, R, D), lambda l, j: (l, 0, j)),
        out_shape=jax.ShapeDtypeStruct((L, R, C), F32),
        compiler_params=_cp(("parallel", "parallel")),
        name="ada_mod",
    )(c_all, ada_w, ada_b.reshape(L, 1, C))


_NT = (((1,), (1,)), ((), ()))


def _pair_keys(lat, rope8, wukq):
    lane = lax.broadcasted_iota(jnp.int32, rope8.shape, 1) & 127
    rope_slot = jnp.where((lane >= NOPE_DIM) & (lane < NOPE_DIM + ROPE_DIM), rope8, 0.0)
    return (_dot(lat, wukq) + jnp.concatenate([rope_slot] * (MLA_HEADS // 2), axis=1)).astype(BF16)


def _in_proj_kernel(x_ref, mod_ref, g1_ref, wq_ref, wkv_ref, wkvt_ref, wkr_ref, wrw_ref, wcv_ref, qg_ref, wqnt_ref,
                    wqrt_ref, wqrst_ref, wukq_ref, kvg_ref, kvgc_ref, cos_ref, sin_ref, cost_ref, sint_ref,
                    lat_ref, kr_ref, kq_ref, latt_ref, qt_ref, prw_ref, glu_ref):
    x = x_ref[0]
    mod = mod_ref[0]
    tm = x.shape[0]
    h = _rms(x, g1_ref[...]) * (1.0 + mod[1:2]) + mod[0:1]
    hb = h.astype(BF16)
    qn = _rms(_dot(hb, wq_ref[...]), qg_ref[...]).astype(BF16)
    qnt = _dot(wqnt_ref[...], qn, _NT)
    qrt = _dot(wqrt_ref[...], qn, _NT)
    qrst = _dot(wqrst_ref[...], qn, _NT)
    qrope = qrt * cost_ref[...] + qrst * sint_ref[...]
    zpad = jnp.zeros((128 - NOPE_DIM - ROPE_DIM, tm), F32)
    for hd in range(MLA_HEADS):
        qh = jnp.concatenate([qnt[NOPE_DIM * hd:NOPE_DIM * (hd + 1)], qrope[ROPE_DIM * hd:ROPE_DIM * (hd + 1)],
                              zpad], axis=0)
        qt_ref[0, hd] = (qh * (SM_SCALE * LOG2_E)).astype(BF16)
    lat = _rms(_dot(hb, wkv_ref[...]), kvg_ref[...])
    lat_ref[0] = lat
    pkvt = _dot(wkvt_ref[...], hb, _NT)
    latt = pkvt * lax.rsqrt(jnp.mean(pkvt * pkvt, axis=0, keepdims=True) + NORM_EPS) * kvgc_ref[...]
    latt_ref[0, 0:KV_LORA, :] = latt.astype(BF16)
    latt_ref[0, KV_LORA:VT_ROWS, :] = jnp.ones((VT_ROWS - KV_LORA, tm), BF16)
    pkr = _dot(hb, wkr_ref[...])
    krt = pkr[:, 0:256] * cos_ref[...] + pkr[:, 256:512] * sin_ref[...]
    kr_ref[0] = krt[:, 0:ROPE_DIM]
    kq_ref[0] = _pair_keys(lat, krt, wukq_ref[...])
    prw_ref[0] = _dot(hb, wrw_ref[...])
    pcv = _dot(hb, wcv_ref[...])
    glu_ref[0] = pcv[:, 0:CONV_DIM] * _sigmoid(pcv[:, CONV_DIM:2 * CONV_DIM])


def _in_proj(x, mod, lw, tabs, tm):
    B, T, D = x.shape
    nT = T // tm
    cos8, sin8, cos8t, sin8t = tabs
    row = lambda w: pl.BlockSpec((1, tm, w), lambda b, i: (b, i, 0))
    col = pl.BlockSpec((1, VT_ROWS, tm), lambda b, i: (b, 0, i))
    args = (lw['g1'], lw['wq'], lw['wkv'], lw['wkvt'], lw['wkr'], lw['wrw'], lw['wcv'], lw['qg'], lw['wqnt'],
            lw['wqrt'], lw['wqrst'], lw['wukq'], lw['kvg'], lw['kvgc'])
    return pl.pallas_call(
        _in_proj_kernel,
        grid=(B, nT),
        in_specs=[row(D), pl.BlockSpec((1, N_MOD, D), lambda b, i: (b, 0, 0))] + [_full(a) for a in args]
                 + [pl.BlockSpec((tm, 256), lambda b, i: (i, 0))] * 2
                 + [pl.BlockSpec((256, tm), lambda b, i: (0, i))] * 2,
        out_specs=[row(KV_LORA), row(ROPE_DIM), row(PAIR_KEYS), col,
                   pl.BlockSpec((1, MLA_HEADS, 128, tm), lambda b, i: (b, 0, 0, i)),
                   row(RWKV_IN), row(CONV_DIM)],
        out_shape=[jax.ShapeDtypeStruct((B, T, KV_LORA), F32),
                   jax.ShapeDtypeStruct((B, T, ROPE_DIM), F32),
                   jax.ShapeDtypeStruct((B, T, PAIR_KEYS), BF16),
                   jax.ShapeDtypeStruct((B, VT_ROWS, T), BF16),
                   jax.ShapeDtypeStruct((B, MLA_HEADS, 128, T), BF16),
                   jax.ShapeDtypeStruct((B, T, RWKV_IN), F32),
                   jax.ShapeDtypeStruct((B, T, CONV_DIM), F32)],
        compiler_params=_cp(("parallel", "parallel")),
        name="in_proj",
    )(x, mod, *args, cos8, sin8, cos8t, sin8t)


def _key_up_kernel(lat_ref, rope8_ref, wukq_ref, kq_ref):
    kq_ref[0] = _pair_keys(lat_ref[0], rope8_ref[0], wukq_ref[...])


def _key_up(lat, rope8, wukq, tm):
    B, S, _ = lat.shape
    row = lambda w: pl.BlockSpec((1, tm, w), lambda b, i: (b, i, 0))
    return pl.pallas_call(
        _key_up_kernel,
        grid=(B, S // tm),
        in_specs=[row(KV_LORA), row(256), _full(wukq)],
        out_specs=row(PAIR_KEYS),
        out_shape=jax.ShapeDtypeStruct((B, S, PAIR_KEYS), BF16),
        compiler_params=_cp(("parallel", "parallel")),
        name="key_up",
    )(lat, rope8, wukq)


def _attn_kernel(qi_ref, ki_ref, last_ref, qt_ref, k_ref, vt_ref, wuvt_ref, pat_ref, o_ref,
                 qcat, m_s, acc, *, tq, tk, past, n_keys):
    p = pl.program_id(1)
    qi = qi_ref[p]
    ki = ki_ref[p]
    H = MLA_HEADS
    NP = H // 2
    nh = tq // 128
    NC = NP * nh

    @pl.when(ki == 0)
    def _():
        qcat[...] = jnp.zeros(qcat.shape, BF16)
        for pr in range(NP):
            for hf in range(nh):
                pos = slice(128 * hf, 128 * (hf + 1))
                qcat[pr * nh + hf, 0:128, 0:128] = qt_ref[0, 2 * pr, :, pos]
                qcat[pr * nh + hf, 128:256, 128:256] = qt_ref[0, 2 * pr + 1, :, pos]
        m_s[...] = jnp.full(m_s.shape, -jnp.inf, F32)
        acc[...] = jnp.zeros(acc.shape, F32)

    def step(bias):
        vt = vt_ref[0]
        scores = {}

        def qk(ch):
            pr = ch // nh
            scores[ch] = jnp.dot(k_ref[0, :, 256 * pr:256 * (pr + 1)], qcat[ch], preferred_element_type=F32)

        for ch in range(min(QK_AHEAD, NC)):
            qk(ch)
        for ch in range(NC):
            if ch + QK_AHEAD < NC:
                qk(ch + QK_AHEAD)
            s = scores.pop(ch)
            if bias is not None:
                s = s + bias[ch % nh]
            m_prev = m_s[ch]
            m_new = jnp.maximum(m_prev, jnp.max(s, axis=0, keepdims=True))
            alpha = jnp.exp2(m_prev - m_new)
            pe = jnp.exp2(s - m_new)
            acc[ch] = alpha * acc[ch] + jnp.dot(vt, pe.astype(BF16), preferred_element_type=F32)
            m_s[ch] = m_new

    is_last = last_ref[p] == 1

    @pl.when(jnp.logical_not(is_last))
    def _():
        step(None)

    @pl.when(is_last)
    def _():
        kpos = ki * tk + lax.broadcasted_iota(jnp.int32, (tk, 1), 0)
        biases = []
        for hf in range(nh):
            t = 128 * hf + lax.broadcasted_iota(jnp.int32, (1, 128), 1)
            limit = jnp.minimum((((past + qi * tq + t) >> 6) + 1) << 6, n_keys)
            bias = jnp.where(kpos < limit, 0.0, -jnp.inf)
            biases.append(jnp.concatenate([bias, bias], axis=1))
        step(biases)
        heads = []
        for pr in range(NP):
            o = [acc[pr * nh + hf, 0:KV_LORA, :] / acc[pr * nh + hf, KV_LORA:KV_LORA + 1, :]
                 for hf in range(nh)]
            for j in range(2):
                oh = jnp.concatenate([x[:, 128 * j:128 * (j + 1)] for x in o], axis=1)
                heads.append(_dot(wuvt_ref[2 * pr + j], oh))
        out_t = _dot(pat_ref[...], jnp.concatenate(heads, axis=0))
        o_ref[0] = out_t.T


def _attn_tiles(T, past):
    n_keys = past + T
    tq = 256 if T % 256 == 0 else 128
    t_pad = -(-T // tq) * tq
    tk = -(-n_keys // 128) * 128 if n_keys <= 2304 else 512
    s_pad = -(-n_keys // tk) * tk
    return tq, t_pad, tk, s_pad, n_keys


def _attention(qt, keys, latt, wuvt, proj_at, past, T):
    B, H = qt.shape[:2]
    tq, t_pad, tk, s_pad, n_keys = _attn_tiles(T, past)
    assert keys.shape[1] == s_pad and latt.shape[2] == s_pad and qt.shape[3] == t_pad and past % CHUNK == 0
    qi, ki, last = [], [], []
    for i in range(t_pad // tq):
        lim = min(((past + (i + 1) * tq - 1) // CHUNK + 1) * CHUNK, n_keys)
        nk = -(-lim // tk)
        for j in range(nk):
            qi.append(i), ki.append(j), last.append(int(j == nk - 1))
    npairs = len(qi)
    tabs = [jnp.asarray(np.asarray(a, np.int32)) for a in (qi, ki, last)]
    nc = (H // 2) * (tq // 128)
    grid_spec = pltpu.PrefetchScalarGridSpec(
        num_scalar_prefetch=3,
        grid=(B, npairs),
        in_specs=[pl.BlockSpec((1, H, 128, tq), lambda b, p, qi, ki, la: (b, 0, 0, qi[p])),
                  pl.BlockSpec((1, tk, PAIR_KEYS), lambda b, p, qi, ki, la: (b, ki[p], 0)),
                  pl.BlockSpec((1, VT_ROWS, tk), lambda b, p, qi, ki, la: (b, 0, ki[p])),
                  pl.BlockSpec(wuvt.shape, lambda b, p, qi, ki, la: (0, 0, 0)),
                  pl.BlockSpec(proj_at.shape, lambda b, p, qi, ki, la: (0, 0))],
        out_specs=pl.BlockSpec((1, tq, D_MODEL), lambda b, p, qi, ki, la: (b, qi[p], 0)),
        scratch_shapes=[pltpu.VMEM((nc, 256, 256), BF16), pltpu.VMEM((nc, 1, 256), F32),
                        pltpu.VMEM((nc, VT_ROWS, 256), F32)])
    return pl.pallas_call(
        functools.partial(_attn_kernel, tq=tq, tk=tk, past=past, n_keys=n_keys),
        grid_spec=grid_spec,
        out_shape=jax.ShapeDtypeStruct((B, t_pad, D_MODEL), F32),
        compiler_params=_cp(("parallel", "arbitrary")),
        name="attention",
    )(*tabs, qt, keys, latt, wuvt, proj_at)


def _head_sum(x, ones_bd):
    return _dot_hl(x, ones_bd)


def _rwkv_features(pr, prev, wts):
    mu, w0, w2p, a0, a2p, g2, k_k, k_a, r_k, ones_bd = wts
    xs = pr + (prev - pr) * mu
    r = xs[:, 0:256]
    k = xs[:, 256:512]
    v = xs[:, 512:768]
    wa = xs[:, 768:896]
    gl = xs[:, 896:1024]
    nz = -(w0 + _dot(jnp.tanh(wa), w2p))
    softplus = jnp.maximum(nz, 0.0) + jnp.log(1.0 + jnp.exp(-jnp.abs(nz)))
    lw = -jnp.exp(-softplus - 0.5)
    a = _sigmoid(a0 + _dot(wa, a2p))
    g = _dot(_sigmoid(gl), g2)
    kk = k * k_k
    kk = kk * lax.rsqrt(_head_sum(kk * kk, ones_bd) + 1e-12)
    kf = k * (1.0 + (a - 1.0) * k_a)
    bonus = _head_sum(r * kf * r_k, ones_bd) * v
    return r, kf, kk * a, kk, v, lw, g, bonus


def _wkv_kernel(prw_ref, shift_ref, s0_ref, mu_ref, w0_ref, w2_ref, a0_ref, a2_ref, g2_ref, kk_ref, ka_ref, rk_ref,
                ones_ref, lng_ref, lnb_ref, yb_ref, sout_ref, S, carry, *, nb, G, n_valid):
    L = WKV_CHUNK
    W = RWKV_DIM

    @pl.when(pl.program_id(0) == 0)
    def _():
        S[...] = s0_ref[...]
        carry[...] = shift_ref[...]

    wts = tuple(ref[...] for ref in (mu_ref, w0_ref, w2_ref, a0_ref, a2_ref, g2_ref, kk_ref, ka_ref, rk_ref, ones_ref))
    ones_bd = wts[-1]
    ln_g = lng_ref[...]
    ln_b = lnb_ref[...]
    ri = lax.broadcasted_iota(jnp.int32, (W, W), 0)
    ci = lax.broadcasted_iota(jnp.int32, (W, W), 1)
    same = (ri >> 6) == (ci >> 6)
    strict = same & ((ci & 63) < (ri & 63))
    incl = same & ((ci & 63) <= (ri & 63))
    eye = (ri == ci).astype(F32)
    tri = (lax.broadcasted_iota(jnp.int32, (L, L), 1) <= lax.broadcasted_iota(jnp.int32, (L, L), 0)).astype(BF16)
    lane_head = lax.broadcasted_iota(jnp.int32, (L, W), 1) >> 6
    row = lax.broadcasted_iota(jnp.int32, (L, 1), 0)
    nt = (((1,), (1,)), ((), ()))
    tn = (((0,), (0,)), ((), ()))

    def stack(x):
        return jnp.concatenate([jnp.where(lane_head == hd, x, 0.0) for hd in range(RWKV_HEADS)], axis=0)

    def collapse(z):
        return z[0:L] + z[L:2 * L] + z[2 * L:3 * L] + z[3 * L:4 * L]

    def features(i):
        pr = prw_ref[i]
        prev = jnp.where(row == 0, carry[i], pltpu.roll(pr, 1, 0))
        carry[i] = pr[L - 1:L, :]
        f = _rwkv_features(pr, prev, wts)
        if n_valid < L:
            f = tuple(jnp.where(row < n_valid, x, 0.0) for x in f[:6]) + f[6:]
        return f

    def body(it, _):
        ids = [it * G + g for g in range(G)]
        each = lambda f, *xs: [f(*a) for a in zip(*xs)]
        r, kf, bb, kk, v, ll, gate, bonus = zip(*[features(i) for i in ids])
        s_prev = [S[i] for i in ids]
        split = each(_split, ll)
        c = each(lambda hl: jnp.dot(tri, hl[0], preferred_element_type=F32)
                 + jnp.dot(tri, hl[1], preferred_element_type=F32), split)
        c_last = each(lambda x: x[L - 1:L, :], c)
        e_neg = each(lambda x: jnp.exp(-x), c)
        e_end = each(lambda x, y: jnp.exp(y - x), c, c_last)
        rt = each(lambda x, y: stack(x * jnp.exp(y)), r, c)
        kkt = each(lambda x, y, z: stack(x * jnp.exp(y - z)), kk, c, ll)
        bt = each(lambda x, y: stack(x * y), bb, e_neg)
        kt = each(lambda x, y: stack(x * y), kf, e_neg)
        a_m = each(lambda x, y: jnp.where(strict, _dot(x, y, nt), 0.0), kkt, bt)
        b_m = each(lambda x, y: jnp.where(strict, _dot(x, y, nt), 0.0), kkt, kt)
        m1 = each(lambda x, y: jnp.where(incl, _dot(x, y, nt), 0.0), rt, bt)
        m2 = each(lambda x, y: jnp.where(incl, _dot(x, y, nt), 0.0), rt, kt)
        t_m = each(lambda x: eye - x, a_m)
        a_p = a_m
        for _ in range(5):
            a_p = each(lambda x: _dot(x, x), a_p)
            t_m = each(lambda x, y: x + _dot(x, y), t_m, a_p)
        vs = each(stack, v)
        w1 = each(lambda x, y: collapse(_dot(x, y)), t_m, kkt)
        bv = each(_dot, b_m, vs)
        w2 = each(lambda x, y: collapse(_dot(x, y)), t_m, bv)
        u = each(lambda x, y, z: -(_dot(x, y, nt) + z), w1, s_prev, w2)
        y0 = each(lambda x, y: _dot(collapse(x), y, nt), rt, s_prev)
        y1 = each(lambda a, b_, c_, d: collapse(_dot(a, stack(b_)) + _dot(c_, d)), m1, u, m2, vs)
        upd = each(lambda a, b_, c_, d, e: _dot(a, b_ * e, tn) + _dot(c_, d * e, tn), u, bb, v, kf, e_end)
        for g, i in enumerate(ids):
            S[i] = s_prev[g] * jnp.exp(c_last[g]) + jnp.where(same, upd[g], 0.0)
            y = y0[g] + y1[g]
            mean = _head_sum(y, ones_bd) * (1.0 / RWKV_HEAD)
            yc = y - mean
            var = _head_sum(yc * yc, ones_bd) * (1.0 / RWKV_HEAD)
            yb = (yc * lax.rsqrt(var + GN_EPS) * ln_g + ln_b + bonus[g]) * gate[g]
            yb_ref[i] = yb.astype(BF16)
        return 0

    lax.fori_loop(0, nb // G, body, 0)
    sout_ref[...] = S[...]


def _wkv(prw, shift0, s0, lw, n_valid):
    B, T, _ = prw.shape
    L = WKV_CHUNK
    W = RWKV_DIM
    assert T % L == 0 and (n_valid == L or T == L)
    args = (lw['mu'], lw['w0'], lw['w2p'], lw['a0'], lw['a2p'], lw['g2'], lw['k_k'], lw['k_a'], lw['r_k'],
            lw['ones_bd'], lw['ln_g'], lw['ln_b'])
    st = pl.BlockSpec((B, W, W), lambda c: (0, 0, 0))
    return pl.pallas_call(
        functools.partial(_wkv_kernel, nb=B, G=4 if B % 4 == 0 else 1, n_valid=n_valid),
        grid=(T // L,),
        in_specs=[pl.BlockSpec((B, L, RWKV_IN), lambda c: (0, c, 0)), _full(shift0), st] + [_full(a) for a in args],
        out_specs=[pl.BlockSpec((B, L, W), lambda c: (0, c, 0)), st],
        out_shape=[jax.ShapeDtypeStruct((B, T, W), BF16), jax.ShapeDtypeStruct((B, W, W), F32)],
        scratch_shapes=[pltpu.VMEM((B, W, W), F32), pltpu.VMEM((B, 1, RWKV_IN), F32)],
        compiler_params=_cp(("arbitrary",)),
        name="wkv",
    )(prw, shift0, s0, *args)


def _conv_module(ext, glu, st_ref, dw_ref, dwb, lng, lnb, new_ref, first, tm):
    @pl.when(first)
    def _():
        ext[0:CONV_HALO, :] = st_ref[0]
    ext[CONV_HALO:CONV_HALO + tm, :] = glu
    off = CONV_HALO - (CONV_WIDTH - 1)
    ext_v = ext[...]
    n = CONV_HALO + tm
    acc = jnp.zeros((tm, CONV_DIM), F32)
    for res in range(8):
        taps = [w for w in range(CONV_WIDTH) if (off + w) % 8 == res]
        if not taps:
            continue
        sh = ext_v if res == 0 else pltpu.roll(ext_v, n - res, 0)
        for w in taps:
            base = off + w - res
            acc = acc + sh[base:base + tm, :] * dw_ref[w:w + 1, :]
    y = acc + dwb
    mu = jnp.mean(y, axis=-1, keepdims=True)
    var = jnp.mean(jnp.square(y - mu), axis=-1, keepdims=True)
    y = (y - mu) * lax.rsqrt(var + LN_EPS) * lng + lnb
    tail = ext[tm:tm + CONV_HALO, :]
    new_ref[0] = tail
    ext[0:CONV_HALO, :] = tail
    return y * _sigmoid(y)


def _merge_kernel(x_ref, mod_ref, g1_ref, wg_ref, oa_ref, yb_ref, pb_ref, glu_ref, st_ref, dw_ref, dwb_ref, clg_ref,
                  clb_ref, pc_ref, wo_ref, g2_ref, wr_ref, br_ref, x1_ref, h2_ref, comb_ref, cnew_ref, ext):
    x = x_ref[0]
    mod = mod_ref[0]
    tm = x.shape[0]
    yc = _conv_module(ext, glu_ref[0], st_ref, dw_ref, dwb_ref[...], clg_ref[...], clb_ref[...], cnew_ref,
                      pl.program_id(1) == 0, tm)
    hb = (_rms(x, g1_ref[...]) * (1.0 + mod[1:2]) + mod[0:1]).astype(BF16)
    D = D_MODEL
    merged = _sigmoid(_dot(hb, wg_ref[:, 0:D])) * oa_ref[0]
    merged = merged + _sigmoid(_dot(hb, wg_ref[:, D:2 * D])) * _dot(yb_ref[0], pb_ref[...])
    merged = merged + _sigmoid(_dot(hb, wg_ref[:, 2 * D:3 * D])) * _dot(yc, pc_ref[...])
    x1 = x + mod[2:3] * _dot(merged, wo_ref[...])
    x1_ref[0] = x1
    h2 = _rms(x1, g2_ref[...]) * (1.0 + mod[4:5]) + mod[3:4]
    h2_ref[0] = h2.astype(BF16)
    logits = _dot3(h2, wr_ref[...]) + br_ref[...]
    lane = lax.broadcasted_iota(jnp.int32, logits.shape, 1)
    neg = -jnp.inf
    is_g = (lane >= N_EXPERTS) & (lane < N_EXPERTS + N_GROUPS)
    gl = jnp.where(is_g, logits, neg)
    gmax = jnp.max(gl, axis=1, keepdims=True)
    gsum = jnp.sum(jnp.exp(gl - gmax), axis=1, keepdims=True)
    g_val = 1.0 / gsum
    g_idx = jnp.min(jnp.where(is_g & (gl == gmax), lane, 4 * ROUTER_LANES), axis=1, keepdims=True) - N_EXPERTS
    in_grp = (lane >= g_idx * EXPERTS_PER_GROUP) & (lane < (g_idx + 1) * EXPERTS_PER_GROUP)
    el = jnp.where(in_grp, logits, neg)
    v1 = jnp.max(el, axis=1, keepdims=True)
    i1 = jnp.min(jnp.where(in_grp & (el == v1), lane, 4 * ROUTER_LANES), axis=1, keepdims=True)
    el2 = jnp.where(lane == i1, neg, el)
    v2 = jnp.max(el2, axis=1, keepdims=True)
    i2 = jnp.min(jnp.where(in_grp & (lane != i1) & (el2 == v2), lane, 4 * ROUTER_LANES), axis=1, keepdims=True)
    e2 = jnp.exp(v2 - v1)
    w1 = g_val / (1.0 + e2)
    w2 = g_val * e2 / (1.0 + e2)
    comb_ref[0] = jnp.where(lane == i1, w1, 0.0) + jnp.where(lane == i2, w2, 0.0)


def _merge(x, mod, oa, yb, glu, st_pad, lw, tm):
    B, T, D = x.shape
    row = lambda w: pl.BlockSpec((1, tm, w), lambda b, i: (b, i, 0))
    modspec = pl.BlockSpec((1, N_MOD, D), lambda b, i: (b, 0, 0))
    halo = pl.BlockSpec((1, CONV_HALO, CONV_DIM), lambda b, i: (b, 0, 0))
    ins = [(x, row(D)), (mod, modspec), (lw['g1'], None), (lw['wg'], None), (oa, row(D)), (yb, row(256)),
           (lw['proj_b'], None), (glu, row(CONV_DIM)), (st_pad, halo), (lw['conv_dw'], None),
           (lw['conv_dw_b'], None), (lw['conv_ln_g'], None), (lw['conv_ln_b'], None), (lw['proj_c'], None),
           (lw['w_out'], None), (lw['g2n'], None), (lw['wr'], None), (lw['br'], None)]
    return pl.pallas_call(
        _merge_kernel,
        grid=(B, T // tm),
        in_specs=[s if s is not None else _full(a) for a, s in ins],
        out_specs=[row(D), row(D), row(ROUTER_LANES), halo],
        out_shape=[jax.ShapeDtypeStruct((B, T, D), F32), jax.ShapeDtypeStruct((B, T, D), BF16),
                   jax.ShapeDtypeStruct((B, T, ROUTER_LANES), F32),
                   jax.ShapeDtypeStruct((B, CONV_HALO, CONV_DIM), F32)],
        scratch_shapes=[pltpu.VMEM((CONV_HALO + tm, CONV_DIM), F32)],
        compiler_params=_cp(("parallel", "arbitrary")),
        name="merge",
    )(*[a for a, _ in ins])


def _moe_kernel(h2_ref, comb_ref, x1_ref, gt_ref, wgu_ref, wd_ref, fg_ref, o_ref, acc, *, final, rb):
    s = pl.program_id(2)
    tm = h2_ref.shape[1]
    nrb = tm // rb
    DE = D_EXPERT

    @pl.when(s == 0)
    def _():
        acc[...] = jnp.zeros_like(acc)

    wgu = wgu_ref[0]
    wd = wd_ref[0]
    gus = {}

    def up(i):
        gus[i] = _dot(h2_ref[0, i * rb:(i + 1) * rb, :], wgu)

    up(0)
    for i in range(nrb):
        if i + 1 < nrb:
            up(i + 1)
        gu = gus.pop(i)
        rows = slice(i * rb, (i + 1) * rb)
        comb = comb_ref[0, rows, :]
        lane = lax.broadcasted_iota(jnp.int32, comb.shape, 1)
        cws = [jnp.sum(jnp.where(lane == 2 * s + j, comb, 0.0), axis=1, keepdims=True) for j in range(2)]
        cw = jnp.concatenate([jnp.broadcast_to(c, (rb, DE)) for c in cws], axis=1)
        gate = gu[:, 0:2 * DE]
        act = gate * _sigmoid(gate) * gu[:, 2 * DE:4 * DE] * cw
        acc[rows, :] += _dot(act, wd)

    @pl.when(s == N_EXPERTS // 2 - 1)
    def _():
        x2 = x1_ref[0] + gt_ref[0] * acc[...]
        o_ref[0] = _rms(x2, fg_ref[...]) if final else x2


def _moe(h2, comb, x1, gt, lw, final_g, tm, final):
    B, T, D = x1.shape
    gt_rows = gt.shape[1]
    gt_spec = (pl.BlockSpec((1, 1, D), lambda b, i, e: (b, 0, 0)) if gt_rows == 1
               else pl.BlockSpec((1, tm, D), lambda b, i, e: (b, i, 0)))
    row = lambda w: pl.BlockSpec((1, tm, w), lambda b, i, e: (b, i, 0))
    return pl.pallas_call(
        functools.partial(_moe_kernel, final=final, rb=min(tm, 256)),
        grid=(B, T // tm, N_EXPERTS // 2),
        in_specs=[row(D), row(ROUTER_LANES), row(D), gt_spec,
                  pl.BlockSpec((1, D, 4 * D_EXPERT), lambda b, i, e: (e, 0, 0)),
                  pl.BlockSpec((1, 2 * D_EXPERT, D), lambda b, i, e: (e, 0, 0)),
                  pl.BlockSpec((1, D), lambda b, i, e: (0, 0))],
        out_specs=row(D),
        out_shape=jax.ShapeDtypeStruct((B, T, D), F32),
        scratch_shapes=[pltpu.VMEM((tm, D), F32)],
        compiler_params=_cp(("parallel", "parallel", "arbitrary")),
        name="moe",
    )(h2, comb, x1, gt, lw['wgu'], lw['wd'], final_g)


def _layer_weights(l, w):
    D = D_MODEL
    w_in = w['w_in'][l]
    c_q, c_kv, c_kr = Q_LORA, Q_LORA + KV_LORA, MLA_IN
    c_rw, c_cv = MLA_IN + RWKV_IN, MLA_IN + RWKV_IN + 2 * CONV_DIM
    half = ROPE_DIM // 2
    swap = np.concatenate([np.arange(half, ROPE_DIM), np.arange(half)])
    wkr = w_in[:, c_kv:c_kr]
    q_up = w['q_up'][l].reshape(Q_LORA, MLA_HEADS, NOPE_DIM + ROPE_DIM)
    q_rope = q_up[:, :, NOPE_DIM:]
    wukq = jnp.zeros((KV_LORA, PAIR_KEYS), F32)
    for hd in range(MLA_HEADS):
        wukq = wukq.at[:, 128 * hd:128 * hd + NOPE_DIM].set(w['w_uk'][l][:, hd, :])
    zl = jnp.zeros((W_LORA, RWKV_DIM), F32)
    head = np.arange(RWKV_DIM) // RWKV_HEAD
    wr = jnp.zeros((D, ROUTER_LANES), F32)
    wr = wr.at[:, 0:N_EXPERTS].set(w['router_expert_w'][l]).at[:, N_EXPERTS:N_EXPERTS + N_GROUPS].set(
        w['router_group_w'][l])
    br = jnp.zeros((1, ROUTER_LANES), F32)
    br = br.at[0, 0:N_EXPERTS].set(w['router_expert_b'][l]).at[0, N_EXPERTS:N_EXPERTS + N_GROUPS].set(
        w['router_group_b'][l])
    r2 = lambda a: a.reshape(1, -1)
    pair_cols = lambda a: jnp.concatenate([a[0::2], a[1::2]], axis=2)
    return dict(
        g1=r2(w['norm1_g'][l]), g2n=r2(w['norm2_g'][l]),
        wq=w_in[:, :c_q].astype(BF16), wkv=w_in[:, c_q:c_kv].astype(BF16),
        wkr=jnp.concatenate([jnp.tile(wkr, (1, 8)), jnp.tile(wkr[:, swap], (1, 8))], axis=1).astype(BF16),
        wrw=w_in[:, c_kr:c_rw].astype(BF16), wcv=w_in[:, c_rw:c_cv].astype(BF16), wg=w_in[:, c_cv:].astype(BF16),
        wkvt=w_in[:, c_q:c_kv].T.astype(BF16), kvgc=w['kv_norm_g'][l].reshape(-1, 1),
        qg=r2(w['q_norm_g'][l]), wqnt=q_up[:, :, :NOPE_DIM].reshape(Q_LORA, -1).T.astype(BF16),
        wukq=wukq.astype(BF16), kvg=r2(w['kv_norm_g'][l]),
        wqrt=q_rope.reshape(Q_LORA, -1).T.astype(BF16), wqrst=q_rope[:, :, swap].reshape(Q_LORA, -1).T.astype(BF16),
        wuvt=jnp.transpose(w['w_uv'][l], (1, 2, 0)).astype(BF16), proj_at=w['proj_a'][l].T.astype(BF16),
        mu=r2(w['rwkv_mu'][l]), w0=r2(w['rwkv_w0'][l]), a0=r2(w['rwkv_a0'][l]),
        w2p=jnp.concatenate([w['rwkv_w2'][l], zl], axis=0).astype(BF16),
        a2p=jnp.concatenate([zl, w['rwkv_a2'][l]], axis=0).astype(BF16),
        g2=w['rwkv_g2'][l].astype(BF16), k_k=r2(w['rwkv_k_k'][l]), k_a=r2(w['rwkv_k_a'][l]),
        r_k=r2(w['rwkv_r_k'][l]), ln_g=r2(w['rwkv_ln_g'][l]), ln_b=r2(w['rwkv_ln_b'][l]),
        ones_bd=jnp.asarray((head[:, None] == head[None, :]).astype(np.float32)).astype(BF16),
        proj_b=w['proj_b'][l].astype(BF16),
        conv_dw=w['conv_dw'][l], conv_dw_b=r2(w['conv_dw_b'][l]), conv_ln_g=r2(w['conv_ln_g'][l]),
        conv_ln_b=r2(w['conv_ln_b'][l]), proj_c=w['proj_c'][l].astype(BF16),
        w_out=w['w_out'][l].astype(BF16), wr=wr, br=br,
        wgu=jnp.concatenate([pair_cols(w['moe_w_gate'][l]), pair_cols(w['moe_w_up'][l])], axis=2).astype(BF16),
        wd=w['moe_w_down'][l].reshape(N_EXPERTS // 2, 2 * D_EXPERT, D).astype(BF16),
    )


def _rope_tables(past, T):
    half = ROPE_DIM // 2
    inv = ROPE_THETA ** (-jnp.arange(half, dtype=F32) / half)
    ang = (past + jnp.arange(T)).astype(F32)[:, None] * inv[None, :]
    cos, sin = jnp.cos(ang), jnp.sin(ang)
    cos8 = jnp.tile(jnp.concatenate([cos, cos], axis=1), (1, 8))
    sin8 = jnp.tile(jnp.concatenate([-sin, sin], axis=1), (1, 8))
    return cos8, sin8, cos8.T, sin8.T


def _row_tile(T, cap):
    return min(T, cap)


def _trunk_layer(x, mod, cache_lat, cache_rope, shift_st, wkv_st, conv_st, lw, final_g, final):
    B, T, D = x.shape
    past = 0 if cache_lat is None else cache_lat.shape[1]
    tm = _row_tile(T, 512)
    lat, kr, keys, latt, qt, prw, glu = _in_proj(x, mod, lw, _rope_tables(past, T), tm)

    _, t_pad, _, s_pad, n_keys = _attn_tiles(T, past)
    row_pad = ((0, 0), (0, s_pad - n_keys), (0, 0))
    if past:
        lat_all = jnp.pad(jnp.concatenate([cache_lat, lat], axis=1), row_pad)
        rope_all = jnp.pad(jnp.concatenate([cache_rope, kr], axis=1), row_pad)
        key_tile = max(d for d in range(8, 641, 8) if s_pad % d == 0)
        keys = _key_up(lat_all, jnp.tile(rope_all, (1, 1, 8)), lw['wukq'], key_tile)
        old_t = jnp.concatenate([jnp.swapaxes(cache_lat, 1, 2).astype(BF16),
                                 jnp.ones((B, VT_ROWS - KV_LORA, past), BF16)], axis=1)
        latt = jnp.concatenate([old_t, latt], axis=2)
    elif s_pad != n_keys:
        keys = jnp.pad(keys, row_pad)
    if s_pad != n_keys:
        latt = jnp.pad(latt, ((0, 0), (0, 0), (0, s_pad - n_keys)))
    if t_pad != T:
        qt = jnp.pad(qt, ((0, 0), (0, 0), (0, 0), (0, t_pad - T)))
    o_a = _attention(qt, keys, latt, lw['wuvt'], lw['proj_at'], past, T)[:, :T]

    L = WKV_CHUNK
    Tp = -(-T // L) * L
    prw_p = prw if Tp == T else jnp.pad(prw, ((0, 0), (0, Tp - T), (0, 0)))
    shift0 = jnp.zeros((B, 1, RWKV_IN), F32) if shift_st is None else shift_st[:, None, :]
    W = RWKV_DIM
    s0 = jnp.zeros((B, W, W), F32)
    if wkv_st is not None:
        for hd in range(RWKV_HEADS):
            o = hd * RWKV_HEAD
            s0 = s0.at[:, o:o + RWKV_HEAD, o:o + RWKV_HEAD].set(wkv_st[:, hd])
    yb, s_bd = _wkv(prw_p, shift0, s0, lw, L if Tp == T else T)
    yb = yb[:, :T]
    wkv_new = jnp.stack([s_bd[:, hd * RWKV_HEAD:(hd + 1) * RWKV_HEAD, hd * RWKV_HEAD:(hd + 1) * RWKV_HEAD]
                         for hd in range(RWKV_HEADS)], axis=1)

    pad_rows = CONV_HALO - (CONV_WIDTH - 1)
    st = jnp.zeros((B, CONV_WIDTH - 1, CONV_DIM), F32) if conv_st is None else conv_st
    x1, h2, comb, conv_tail = _merge(x, mod, o_a, yb, glu, jnp.pad(st, ((0, 0), (pad_rows, 0), (0, 0))), lw, tm)
    conv_new = conv_tail[:, pad_rows:]

    if T >= 1024:
        x2 = _moe(h2, comb, x1, mod[:, 5:6], lw, final_g, 1024, final)
    else:
        gt = jnp.broadcast_to(mod[:, 5:6], (B, T, D)).reshape(1, B * T, D)
        x2 = _moe(h2.reshape(1, B * T, D), comb.reshape(1, B * T, ROUTER_LANES), x1.reshape(1, B * T, D), gt, lw,
                  final_g, B * T, final).reshape(B, T, D)
    return x2, (lat, kr, prw[:, -1], wkv_new, conv_new)


def kernel(x_prompt, x_sample, c_prompt, c_sample, cache_kv_latent, cache_k_rope, state_rwkv_shift, state_rwkv_wkv, state_conv, ada_w, ada_b, norm1_g, norm2_g, w_in, q_norm_g, q_up, kv_norm_g, w_uk, w_uv, proj_a, rwkv_mu, rwkv_w0, rwkv_w2, rwkv_a0, rwkv_a2, rwkv_g2, rwkv_k_k, rwkv_k_a, rwkv_r_k, rwkv_ln_g, rwkv_ln_b, proj_b, conv_dw, conv_dw_b, conv_ln_g, conv_ln_b, proj_c, w_out, router_group_w, router_group_b, router_expert_w, router_expert_b, moe_w_gate, moe_w_up, moe_w_down, final_g):
    w = dict(norm1_g=norm1_g, norm2_g=norm2_g, w_in=w_in, q_norm_g=q_norm_g, q_up=q_up, kv_norm_g=kv_norm_g,
             w_uk=w_uk, w_uv=w_uv, proj_a=proj_a, rwkv_mu=rwkv_mu, rwkv_w0=rwkv_w0, rwkv_w2=rwkv_w2,
             rwkv_a0=rwkv_a0, rwkv_a2=rwkv_a2, rwkv_g2=rwkv_g2, rwkv_k_k=rwkv_k_k, rwkv_k_a=rwkv_k_a,
             rwkv_r_k=rwkv_r_k.reshape(rwkv_r_k.shape[0], -1), rwkv_ln_g=rwkv_ln_g, rwkv_ln_b=rwkv_ln_b,
             proj_b=proj_b, conv_dw=conv_dw, conv_dw_b=conv_dw_b, conv_ln_g=conv_ln_g, conv_ln_b=conv_ln_b,
             proj_c=proj_c, w_out=w_out, router_group_w=router_group_w, router_group_b=router_group_b,
             router_expert_w=router_expert_w, router_expert_b=router_expert_b, moe_w_gate=moe_w_gate,
             moe_w_up=moe_w_up, moe_w_down=moe_w_down)
    depth = ada_w.shape[0]
    bp = x_prompt.shape[0]
    D = D_MODEL
    mod_all = _ada_mod(jnp.concatenate([c_prompt, c_sample], axis=0), ada_w, ada_b)
    fg = final_g.reshape(1, D)
    hp, hs = x_prompt, x_sample
    new_p, new_s = [], []
    for l in range(depth):
        lw = _layer_weights(l, w)
        mod_p = mod_all[l, :bp].reshape(bp, N_MOD, D)
        mod_s = mod_all[l, bp:].reshape(-1, N_MOD, D)
        final = l == depth - 1
        hp, st_p = _trunk_layer(hp, mod_p, None, None, None, None, None, lw, fg, final)
        hs, st_s = _trunk_layer(hs, mod_s, cache_kv_latent[l], cache_k_rope[l], state_rwkv_shift[l],
                                state_rwkv_wkv[l], state_conv[l], lw, fg, final)
        new_p.append(st_p)
        new_s.append(st_s)
    stack = lambda states, i: jnp.stack([s[i] for s in states], axis=0)
    return ((hp, hs) + tuple(stack(new_p, i) for i in range(5)) + tuple(stack(new_s, i) for i in range(5)))
```

```python
import functools

import numpy as np
import jax
import jax.numpy as jnp
from jax import lax
from jax.experimental import pallas as pl
from jax.experimental.pallas import tpu as pltpu

F32 = jnp.float32
BF16 = jnp.bfloat16

D_MODEL = 1024
CHUNK = 64
NORM_EPS = 1e-6
MLA_HEADS = 8
Q_LORA = 384
KV_LORA = 256
NOPE_DIM = 64
ROPE_DIM = 32
V_HEAD = 64
ROPE_THETA = 10000.0
SM_SCALE = (NOPE_DIM + ROPE_DIM) ** -0.5
LOG2_E = 1.4426950408889634
RWKV_HEADS = 4
RWKV_HEAD = 64
RWKV_DIM = RWKV_HEADS * RWKV_HEAD
W_LORA = 64
A_LORA = 64
G_LORA = 128
RWKV_IN = 3 * RWKV_DIM + W_LORA + A_LORA + G_LORA
GN_EPS = 64e-5
CONV_DIM = 256
CONV_WIDTH = 31
LN_EPS = 1e-5
MLA_IN = Q_LORA + KV_LORA + ROPE_DIM
N_MOD = 6
N_GROUPS = 4
EXPERTS_PER_GROUP = 4
N_EXPERTS = 16
D_EXPERT = 256
WKV_CHUNK = 64
CONV_HALO = 32
ROUTER_LANES = 128
PAIR_KEYS = 4 * 256
VT_ROWS = KV_LORA + 16
MOE_EPS = 4
QK_AHEAD = 5
VMEM_LIMIT = 56 * 1024 * 1024


def _cp(sem):
    return pltpu.CompilerParams(dimension_semantics=sem, vmem_limit_bytes=VMEM_LIMIT)


def _full(a):
    nd = a.ndim
    return pl.BlockSpec(a.shape, lambda *_: (0,) * nd)


def _dot(a, b, dims=(((1,), (0,)), ((), ()))):
    return lax.dot_general(a.astype(BF16), b.astype(BF16), dims, preferred_element_type=F32)


def _split(a):
    hi = a.astype(BF16)
    lo = (a - hi.astype(F32)).astype(BF16)
    return hi, lo


def _dot_hl(a, b, dims=(((1,), (0,)), ((), ()))):
    hi, lo = _split(a)
    bb = b.astype(BF16)
    return (lax.dot_general(hi, bb, dims, preferred_element_type=F32)
            + lax.dot_general(lo, bb, dims, preferred_element_type=F32))


def _dot3(a, b, dims=(((1,), (0,)), ((), ()))):
    ah, al = _split(a)
    bh, bl = _split(b)
    d = lambda x, y: lax.dot_general(x, y, dims, preferred_element_type=F32)
    return d(ah, bh) + d(ah, bl) + d(al, bh)


def _sigmoid(x):
    return 0.5 * jnp.tanh(0.5 * x) + 0.5


def _rms(x, g):
    return x * lax.rsqrt(jnp.mean(x * x, axis=-1, keepdims=True) + NORM_EPS) * g


def _ada_kernel(c_ref, w_ref, b_ref, o_ref):
    c = c_ref[...]
    o_ref[0] = _dot(c * _sigmoid(c), w_ref[0]) + b_ref[0]


def _ada_mod(c_all, ada_w, ada_b):
    L, D, C = ada_w.shape
    R = c_all.shape[0]
    nb = C // D
    return pl.pallas_call(
        _ada_kernel,
        grid=(L, nb),
        in_specs=[pl.BlockSpec((R, D), lambda l, j: (0, 0)),
                  pl.BlockSpec((1, D, D), lambda l, j: (l, 0, j)),
                  pl.BlockSpec((1, 1, D), lambda l, j: (l, 0, j))],
        out_specs=pl.BlockSpec((1, R, D), lambda l, j: (l, 0, j)),
        out_shape=jax.ShapeDtypeStruct((L, R, C), F32),
        compiler_params=_cp(("parallel", "parallel")),
        name="ada_mod",
    )(c_all, ada_w, ada_b.reshape(L, 1, C))


_NT = (((1,), (1,)), ((), ()))


def _pair_keys(lat, rope8, wukq):
    lane = lax.broadcasted_iota(jnp.int32, rope8.shape, 1) & 127
    rope_slot = jnp.where((lane >= NOPE_DIM) & (lane < NOPE_DIM + ROPE_DIM), rope8, 0.0)
    return (_dot(lat, wukq) + jnp.concatenate([rope_slot] * (MLA_HEADS // 2), axis=1)).astype(BF16)


def _in_proj_kernel(x_ref, mod_ref, g1_ref, wq_ref, wkv_ref, wkvt_ref, wkr_ref, wrw_ref, wcv_ref, qg_ref, wqnt_ref,
                    wqrt_ref, wqrst_ref, wukq_ref, kvg_ref, kvgc_ref, cos_ref, sin_ref, cost_ref, sint_ref,
                    lat_ref, kr_ref, kq_ref, latt_ref, qt_ref, prw_ref, glu_ref):
    x = x_ref[0]
    mod = mod_ref[0]
    tm = x.shape[0]
    h = _rms(x, g1_ref[...]) * (1.0 + mod[1:2]) + mod[0:1]
    hb = h.astype(BF16)
    qn = _rms(_dot(hb, wq_ref[...]), qg_ref[...]).astype(BF16)
    qnt = _dot(wqnt_ref[...], qn, _NT)
    qrt = _dot(wqrt_ref[...], qn, _NT)
    qrst = _dot(wqrst_ref[...], qn, _NT)
    qrope = qrt * cost_ref[...] + qrst * sint_ref[...]
    zpad = jnp.zeros((128 - NOPE_DIM - ROPE_DIM, tm), F32)
    for hd in range(MLA_HEADS):
        qh = jnp.concatenate([qnt[NOPE_DIM * hd:NOPE_DIM * (hd + 1)], qrope[ROPE_DIM * hd:ROPE_DIM * (hd + 1)],
                              zpad], axis=0)
        qt_ref[0, hd] = (qh * (SM_SCALE * LOG2_E)).astype(BF16)
    lat = _rms(_dot(hb, wkv_ref[...]), kvg_ref[...])
    lat_ref[0] = lat
    pkvt = _dot(wkvt_ref[...], hb, _NT)
    latt = pkvt * lax.rsqrt(jnp.mean(pkvt * pkvt, axis=0, keepdims=True) + NORM_EPS) * kvgc_ref[...]
    latt_ref[0, 0:KV_LORA, :] = latt.astype(BF16)
    latt_ref[0, KV_LORA:VT_ROWS, :] = jnp.ones((VT_ROWS - KV_LORA, tm), BF16)
    pkr = _dot(hb, wkr_ref[...])
    krt = pkr[:, 0:256] * cos_ref[...] + pkr[:, 256:512] * sin_ref[...]
    kr_ref[0] = krt[:, 0:ROPE_DIM]
    kq_ref[0] = _pair_keys(lat, krt, wukq_ref[...])
    prw_ref[0] = _dot(hb, wrw_ref[...])
    pcv = _dot(hb, wcv_ref[...])
    glu_ref[0] = pcv[:, 0:CONV_DIM] * _sigmoid(pcv[:, CONV_DIM:2 * CONV_DIM])


def _in_proj(x, mod, lw, tabs, tm):
    B, T, D = x.shape
    nT = T // tm
    cos8, sin8, cos8t, sin8t = tabs
    row = lambda w: pl.BlockSpec((1, tm, w), lambda b, i: (b, i, 0))
    col = pl.BlockSpec((1, VT_ROWS, tm), lambda b, i: (b, 0, i))
    args = (lw['g1'], lw['wq'], lw['wkv'], lw['wkvt'], lw['wkr'], lw['wrw'], lw['wcv'], lw['qg'], lw['wqnt'],
            lw['wqrt'], lw['wqrst'], lw['wukq'], lw['kvg'], lw['kvgc'])
    return pl.pallas_call(
        _in_proj_kernel,
        grid=(B, nT),
        in_specs=[row(D), pl.BlockSpec((1, N_MOD, D), lambda b, i: (b, 0, 0))] + [_full(a) for a in args]
                 + [pl.BlockSpec((tm, 256), lambda b, i: (i, 0))] * 2
                 + [pl.BlockSpec((256, tm), lambda b, i: (0, i))] * 2,
        out_specs=[row(KV_LORA), row(ROPE_DIM), row(PAIR_KEYS), col,
                   pl.BlockSpec((1, MLA_HEADS, 128, tm), lambda b, i: (b, 0, 0, i)),
                   row(RWKV_IN), row(CONV_DIM)],
        out_shape=[jax.ShapeDtypeStruct((B, T, KV_LORA), F32),
                   jax.ShapeDtypeStruct((B, T, ROPE_DIM), F32),
                   jax.ShapeDtypeStruct((B, T, PAIR_KEYS), BF16),
                   jax.ShapeDtypeStruct((B, VT_ROWS, T), BF16),
                   jax.ShapeDtypeStruct((B, MLA_HEADS, 128, T), BF16),
                   jax.ShapeDtypeStruct((B, T, RWKV_IN), F32),
                   jax.ShapeDtypeStruct((B, T, CONV_DIM), F32)],
        compiler_params=_cp(("parallel", "parallel")),
        name="in_proj",
    )(x, mod, *args, cos8, sin8, cos8t, sin8t)


def _key_up_kernel(lat_ref, rope8_ref, wukq_ref, kq_ref):
    kq_ref[0] = _pair_keys(lat_ref[0], rope8_ref[0], wukq_ref[...])


def _key_up(lat, rope8, wukq, tm):
    B, S, _ = lat.shape
    row = lambda w: pl.BlockSpec((1, tm, w), lambda b, i: (b, i, 0))
    return pl.pallas_call(
        _key_up_kernel,
        grid=(B, S // tm),
        in_specs=[row(KV_LORA), row(256), _full(wukq)],
        out_specs=row(PAIR_KEYS),
        out_shape=jax.ShapeDtypeStruct((B, S, PAIR_KEYS), BF16),
        compiler_params=_cp(("parallel", "parallel")),
        name="key_up",
    )(lat, rope8, wukq)


def _attn_kernel(qi_ref, ki_ref, last_ref, qt_ref, k_ref, vt_ref, wuvt_ref, pat_ref, o_ref,
                 qcat, m_s, acc, *, tq, tk, past, n_keys):
    p = pl.program_id(1)
    qi = qi_ref[p]
    ki = ki_ref[p]
    H = MLA_HEADS
    NP = H // 2
    nh = tq // 128
    NC = NP * nh

    @pl.when(ki == 0)
    def _():
        qcat[...] = jnp.zeros(qcat.shape, BF16)
        for pr in range(NP):
            for hf in range(nh):
                pos = slice(128 * hf, 128 * (hf + 1))
                qcat[pr * nh + hf, 0:128, 0:128] = qt_ref[0, 2 * pr, :, pos]
                qcat[pr * nh + hf, 128:256, 128:256] = qt_ref[0, 2 * pr + 1, :, pos]
        m_s[...] = jnp.full(m_s.shape, -jnp.inf, F32)
        acc[...] = jnp.zeros(acc.shape, F32)

    def step(bias):
        vt = vt_ref[0]
        scores = {}

        def qk(ch):
            pr = ch // nh
            scores[ch] = jnp.dot(k_ref[0, :, 256 * pr:256 * (pr + 1)], qcat[ch], preferred_element_type=F32)

        for ch in range(min(QK_AHEAD, NC)):
            qk(ch)
        for ch in range(NC):
            if ch + QK_AHEAD < NC:
                qk(ch + QK_AHEAD)
            s = scores.pop(ch)
            if bias is not None:
                s = s + bias[ch % nh]
            m_prev = m_s[ch]
            m_new = jnp.maximum(m_prev, jnp.max(s, axis=0, keepdims=True))
            alpha = jnp.exp2(m_prev - m_new)
            pe = jnp.exp2(s - m_new)
            acc[ch] = alpha * acc[ch] + jnp.dot(vt, pe.astype(BF16), preferred_element_type=F32)
            m_s[ch] = m_new

    is_last = last_ref[p] == 1

    @pl.when(jnp.logical_not(is_last))
    def _():
        step(None)

    @pl.when(is_last)
    def _():
        kpos = ki * tk + lax.broadcasted_iota(jnp.int32, (tk, 1), 0)
        biases = []
        for hf in range(nh):
            t = 128 * hf + lax.broadcasted_iota(jnp.int32, (1, 128), 1)
            limit = jnp.minimum((((past + qi * tq + t) >> 6) + 1) << 6, n_keys)
            bias = jnp.where(kpos < limit, 0.0, -jnp.inf)
            biases.append(jnp.concatenate([bias, bias], axis=1))
        step(biases)
        heads = []
        for pr in range(NP):
            o = [acc[pr * nh + hf, 0:KV_LORA, :] / acc[pr * nh + hf, KV_LORA:KV_LORA + 1, :]
                 for hf in range(nh)]
            for j in range(2):
                oh = jnp.concatenate([x[:, 128 * j:128 * (j + 1)] for x in o], axis=1)
                heads.append(_dot(wuvt_ref[2 * pr + j], oh))
        out_t = _dot(pat_ref[...], jnp.concatenate(heads, axis=0))
        o_ref[0] = out_t.T


def _attn_tiles(T, past):
    n_keys = past + T
    tq = 512 if T % 512 == 0 else 128
    t_pad = -(-T // tq) * tq
    tk = -(-n_keys // 128) * 128 if n_keys <= 2304 else 512
    s_pad = -(-n_keys // tk) * tk
    return tq, t_pad, tk, s_pad, n_keys


def _attention(qt, keys, latt, wuvt, proj_at, past, T):
    B, H = qt.shape[:2]
    tq, t_pad, tk, s_pad, n_keys = _attn_tiles(T, past)
    assert keys.shape[1] == s_pad and latt.shape[2] == s_pad and qt.shape[3] == t_pad and past % CHUNK == 0
    qi, ki, last = [], [], []
    for i in range(t_pad // tq):
        lim = min(((past + (i + 1) * tq - 1) // CHUNK + 1) * CHUNK, n_keys)
        nk = -(-lim // tk)
        for j in range(nk):
            qi.append(i), ki.append(j), last.append(int(j == nk - 1))
    npairs = len(qi)
    tabs = [jnp.asarray(np.asarray(a, np.int32)) for a in (qi, ki, last)]
    nc = (H // 2) * (tq // 128)
    grid_spec = pltpu.PrefetchScalarGridSpec(
        num_scalar_prefetch=3,
        grid=(B, npairs),
        in_specs=[pl.BlockSpec((1, H, 128, tq), lambda b, p, qi, ki, la: (b, 0, 0, qi[p])),
                  pl.BlockSpec((1, tk, PAIR_KEYS), lambda b, p, qi, ki, la: (b, ki[p], 0)),
                  pl.BlockSpec((1, VT_ROWS, tk), lambda b, p, qi, ki, la: (b, 0, ki[p])),
                  pl.BlockSpec(wuvt.shape, lambda b, p, qi, ki, la: (0, 0, 0)),
                  pl.BlockSpec(proj_at.shape, lambda b, p, qi, ki, la: (0, 0))],
        out_specs=pl.BlockSpec((1, tq, D_MODEL), lambda b, p, qi, ki, la: (b, qi[p], 0)),
        scratch_shapes=[pltpu.VMEM((nc, 256, 256), BF16), pltpu.VMEM((nc, 1, 256), F32),
                        pltpu.VMEM((nc, VT_ROWS, 256), F32)])
    return pl.pallas_call(
        functools.partial(_attn_kernel, tq=tq, tk=tk, past=past, n_keys=n_keys),
        grid_spec=grid_spec,
        out_shape=jax.ShapeDtypeStruct((B, t_pad, D_MODEL), F32),
        compiler_params=_cp(("parallel", "arbitrary")),
        name="attention",
    )(*tabs, qt, keys, latt, wuvt, proj_at)


def _head_sum(x, ones_bd):
    return _dot_hl(x, ones_bd)


def _rwkv_features(pr, prev, wts):
    mu, w0, w2p, a0, a2p, g2, k_k, k_a, r_k, ones_bd = wts
    xs = pr + (prev - pr) * mu
    r = xs[:, 0:256]
    k = xs[:, 256:512]
    v = xs[:, 512:768]
    wa = xs[:, 768:896]
    gl = xs[:, 896:1024]
    nz = -(w0 + _dot(jnp.tanh(wa), w2p))
    softplus = jnp.maximum(nz, 0.0) + jnp.log(1.0 + jnp.exp(-jnp.abs(nz)))
    lw = -jnp.exp(-softplus - 0.5)
    a = _sigmoid(a0 + _dot(wa, a2p))
    g = _dot(_sigmoid(gl), g2)
    kk = k * k_k
    kk = kk * lax.rsqrt(_head_sum(kk * kk, ones_bd) + 1e-12)
    kf = k * (1.0 + (a - 1.0) * k_a)
    bonus = _head_sum(r * kf * r_k, ones_bd) * v
    return r, kf, kk * a, kk, v, lw, g, bonus


def _wkv_kernel(prw_ref, shift_ref, s0_ref, mu_ref, w0_ref, w2_ref, a0_ref, a2_ref, g2_ref, kk_ref, ka_ref, rk_ref,
                ones_ref, lng_ref, lnb_ref, yb_ref, sout_ref, S, carry, *, nb, G, n_valid):
    L = WKV_CHUNK
    W = RWKV_DIM

    @pl.when(pl.program_id(0) == 0)
    def _():
        S[...] = s0_ref[...]
        carry[...] = shift_ref[...]

    wts = tuple(ref[...] for ref in (mu_ref, w0_ref, w2_ref, a0_ref, a2_ref, g2_ref, kk_ref, ka_ref, rk_ref, ones_ref))
    ones_bd = wts[-1]
    ln_g = lng_ref[...]
    ln_b = lnb_ref[...]
    ri = lax.broadcasted_iota(jnp.int32, (W, W), 0)
    ci = lax.broadcasted_iota(jnp.int32, (W, W), 1)
    same = (ri >> 6) == (ci >> 6)
    strict = same & ((ci & 63) < (ri & 63))
    incl = same & ((ci & 63) <= (ri & 63))
    eye = (ri == ci).astype(F32)
    tri = (lax.broadcasted_iota(jnp.int32, (L, L), 1) <= lax.broadcasted_iota(jnp.int32, (L, L), 0)).astype(BF16)
    lane_head = lax.broadcasted_iota(jnp.int32, (L, W), 1) >> 6
    row = lax.broadcasted_iota(jnp.int32, (L, 1), 0)
    nt = (((1,), (1,)), ((), ()))
    tn = (((0,), (0,)), ((), ()))

    def stack(x):
        return jnp.concatenate([jnp.where(lane_head == hd, x, 0.0) for hd in range(RWKV_HEADS)], axis=0)

    def collapse(z):
        return z[0:L] + z[L:2 * L] + z[2 * L:3 * L] + z[3 * L:4 * L]

    def features(i):
        pr = prw_ref[i]
        prev = jnp.where(row == 0, carry[i], pltpu.roll(pr, 1, 0))
        carry[i] = pr[L - 1:L, :]
        f = _rwkv_features(pr, prev, wts)
        if n_valid < L:
            f = tuple(jnp.where(row < n_valid, x, 0.0) for x in f[:6]) + f[6:]
        return f

    def body(it, _):
        ids = [it * G + g for g in range(G)]
        each = lambda f, *xs: [f(*a) for a in zip(*xs)]
        r, kf, bb, kk, v, ll, gate, bonus = zip(*[features(i) for i in ids])
        s_prev = [S[i] for i in ids]
        split = each(_split, ll)
        c = each(lambda hl: jnp.dot(tri, hl[0], preferred_element_type=F32)
                 + jnp.dot(tri, hl[1], preferred_element_type=F32), split)
        c_last = each(lambda x: x[L - 1:L, :], c)
        e_neg = each(lambda x: jnp.exp(-x), c)
        e_end = each(lambda x, y: jnp.exp(y - x), c, c_last)
        rt = each(lambda x, y: stack(x * jnp.exp(y)), r, c)
        kkt = each(lambda x, y, z: stack(x * jnp.exp(y - z)), kk, c, ll)
        bt = each(lambda x, y: stack(x * y), bb, e_neg)
        kt = each(lambda x, y: stack(x * y), kf, e_neg)
        a_m = each(lambda x, y: jnp.where(strict, _dot(x, y, nt), 0.0), kkt, bt)
        b_m = each(lambda x, y: jnp.where(strict, _dot(x, y, nt), 0.0), kkt, kt)
        m1 = each(lambda x, y: jnp.where(incl, _dot(x, y, nt), 0.0), rt, bt)
        m2 = each(lambda x, y: jnp.where(incl, _dot(x, y, nt), 0.0), rt, kt)
        t_m = each(lambda x: eye - x, a_m)
        a_p = a_m
        for _ in range(5):
            a_p = each(lambda x: _dot(x, x), a_p)
            t_m = each(lambda x, y: x + _dot(x, y), t_m, a_p)
        vs = each(stack, v)
        w1 = each(lambda x, y: collapse(_dot(x, y)), t_m, kkt)
        bv = each(_dot, b_m, vs)
        w2 = each(lambda x, y: collapse(_dot(x, y)), t_m, bv)
        u = each(lambda x, y, z: -(_dot(x, y, nt) + z), w1, s_prev, w2)
        y0 = each(lambda x, y: _dot(collapse(x), y, nt), rt, s_prev)
        y1 = each(lambda a, b_, c_, d: collapse(_dot(a, stack(b_)) + _dot(c_, d)), m1, u, m2, vs)
        upd = each(lambda a, b_, c_, d, e: _dot(a, b_ * e, tn) + _dot(c_, d * e, tn), u, bb, v, kf, e_end)
        for g, i in enumerate(ids):
            S[i] = s_prev[g] * jnp.exp(c_last[g]) + jnp.where(same, upd[g], 0.0)
            y = y0[g] + y1[g]
            mean = _head_sum(y, ones_bd) * (1.0 / RWKV_HEAD)
            yc = y - mean
            var = _head_sum(yc * yc, ones_bd) * (1.0 / RWKV_HEAD)
            yb = (yc * lax.rsqrt(var + GN_EPS) * ln_g + ln_b + bonus[g]) * gate[g]
            yb_ref[i] = yb.astype(BF16)
        return 0

    lax.fori_loop(0, nb // G, body, 0)

    @pl.when(pl.program_id(0) == pl.num_programs(0) - 1)
    def _():
        sout_ref[...] = S[...]


def _wkv(prw, shift0, s0, lw, n_valid):
    B, T, _ = prw.shape
    L = WKV_CHUNK
    W = RWKV_DIM
    assert T % L == 0 and (n_valid == L or T == L)
    args = (lw['mu'], lw['w0'], lw['w2p'], lw['a0'], lw['a2p'], lw['g2'], lw['k_k'], lw['k_a'], lw['r_k'],
            lw['ones_bd'], lw['ln_g'], lw['ln_b'])
    st = pl.BlockSpec((B, W, W), lambda c: (0, 0, 0))
    return pl.pallas_call(
        functools.partial(_wkv_kernel, nb=B, G=4 if B % 4 == 0 else 1, n_valid=n_valid),
        grid=(T // L,),
        in_specs=[pl.BlockSpec((B, L, RWKV_IN), lambda c: (0, c, 0)), _full(shift0), st] + [_full(a) for a in args],
        out_specs=[pl.BlockSpec((B, L, W), lambda c: (0, c, 0)), st],
        out_shape=[jax.ShapeDtypeStruct((B, T, W), BF16), jax.ShapeDtypeStruct((B, W, W), F32)],
        scratch_shapes=[pltpu.VMEM((B, W, W), F32), pltpu.VMEM((B, 1, RWKV_IN), F32)],
        compiler_params=_cp(("arbitrary",)),
        name="wkv",
    )(prw, shift0, s0, *args)


def _conv_module(ext, glu, st_ref, dw_ref, dwb, lng, lnb, new_ref, first, tm):
    @pl.when(first)
    def _():
        ext[0:CONV_HALO, :] = st_ref[0]
    ext[CONV_HALO:CONV_HALO + tm, :] = glu
    off = CONV_HALO - (CONV_WIDTH - 1)
    ext_v = ext[...]
    n = CONV_HALO + tm
    acc = jnp.zeros((tm, CONV_DIM), F32)
    for res in range(8):
        taps = [w for w in range(CONV_WIDTH) if (off + w) % 8 == res]
        if not taps:
            continue
        sh = ext_v if res == 0 else pltpu.roll(ext_v, n - res, 0)
        for w in taps:
            base = off + w - res
            acc = acc + sh[base:base + tm, :] * dw_ref[w:w + 1, :]
    y = acc + dwb
    mu = jnp.mean(y, axis=-1, keepdims=True)
    var = jnp.mean(jnp.square(y - mu), axis=-1, keepdims=True)
    y = (y - mu) * lax.rsqrt(var + LN_EPS) * lng + lnb
    tail = ext[tm:tm + CONV_HALO, :]
    new_ref[0] = tail
    ext[0:CONV_HALO, :] = tail
    return y * _sigmoid(y)


def _merge_kernel(x_ref, mod_ref, g1_ref, wg_ref, oa_ref, yb_ref, pb_ref, glu_ref, st_ref, dw_ref, dwb_ref, clg_ref,
                  clb_ref, pc_ref, wo_ref, g2_ref, wr_ref, br_ref, x1_ref, h2_ref, comb_ref, cnew_ref, ext):
    x = x_ref[0]
    mod = mod_ref[0]
    tm = x.shape[0]
    yc = _conv_module(ext, glu_ref[0], st_ref, dw_ref, dwb_ref[...], clg_ref[...], clb_ref[...], cnew_ref,
                      pl.program_id(1) == 0, tm)
    hb = (_rms(x, g1_ref[...]) * (1.0 + mod[1:2]) + mod[0:1]).astype(BF16)
    D = D_MODEL
    merged = _sigmoid(_dot(hb, wg_ref[:, 0:D])) * oa_ref[0]
    merged = merged + _sigmoid(_dot(hb, wg_ref[:, D:2 * D])) * _dot(yb_ref[0], pb_ref[...])
    merged = merged + _sigmoid(_dot(hb, wg_ref[:, 2 * D:3 * D])) * _dot(yc, pc_ref[...])
    x1 = x + mod[2:3] * _dot(merged, wo_ref[...])
    x1_ref[0] = x1
    h2 = _rms(x1, g2_ref[...]) * (1.0 + mod[4:5]) + mod[3:4]
    h2_ref[0] = h2.astype(BF16)
    logits = _dot3(h2, wr_ref[...]) + br_ref[...]
    lane = lax.broadcasted_iota(jnp.int32, logits.shape, 1)
    neg = -jnp.inf
    is_g = (lane >= N_EXPERTS) & (lane < N_EXPERTS + N_GROUPS)
    gl = jnp.where(is_g, logits, neg)
    gmax = jnp.max(gl, axis=1, keepdims=True)
    gsum = jnp.sum(jnp.exp(gl - gmax), axis=1, keepdims=True)
    g_val = 1.0 / gsum
    g_idx = jnp.min(jnp.where(is_g & (gl == gmax), lane, 4 * ROUTER_LANES), axis=1, keepdims=True) - N_EXPERTS
    in_grp = (lane >= g_idx * EXPERTS_PER_GROUP) & (lane < (g_idx + 1) * EXPERTS_PER_GROUP)
    el = jnp.where(in_grp, logits, neg)
    v1 = jnp.max(el, axis=1, keepdims=True)
    i1 = jnp.min(jnp.where(in_grp & (el == v1), lane, 4 * ROUTER_LANES), axis=1, keepdims=True)
    el2 = jnp.where(lane == i1, neg, el)
    v2 = jnp.max(el2, axis=1, keepdims=True)
    i2 = jnp.min(jnp.where(in_grp & (lane != i1) & (el2 == v2), lane, 4 * ROUTER_LANES), axis=1, keepdims=True)
    e2 = jnp.exp(v2 - v1)
    w1 = g_val / (1.0 + e2)
    w2 = g_val * e2 / (1.0 + e2)
    comb_ref[0] = jnp.where(lane == i1, w1, 0.0) + jnp.where(lane == i2, w2, 0.0)


def _merge(x, mod, oa, yb, glu, st_pad, lw, tm):
    B, T, D = x.shape
    row = lambda w: pl.BlockSpec((1, tm, w), lambda b, i: (b, i, 0))
    modspec = pl.BlockSpec((1, N_MOD, D), lambda b, i: (b, 0, 0))
    halo = pl.BlockSpec((1, CONV_HALO, CONV_DIM), lambda b, i: (b, 0, 0))
    ins = [(x, row(D)), (mod, modspec), (lw['g1'], None), (lw['wg'], None), (oa, row(D)), (yb, row(256)),
           (lw['proj_b'], None), (glu, row(CONV_DIM)), (st_pad, halo), (lw['conv_dw'], None),
           (lw['conv_dw_b'], None), (lw['conv_ln_g'], None), (lw['conv_ln_b'], None), (lw['proj_c'], None),
           (lw['w_out'], None), (lw['g2n'], None), (lw['wr'], None), (lw['br'], None)]
    return pl.pallas_call(
        _merge_kernel,
        grid=(B, T // tm),
        in_specs=[s if s is not None else _full(a) for a, s in ins],
        out_specs=[row(D), row(D), row(ROUTER_LANES), halo],
        out_shape=[jax.ShapeDtypeStruct((B, T, D), F32), jax.ShapeDtypeStruct((B, T, D), BF16),
                   jax.ShapeDtypeStruct((B, T, ROUTER_LANES), F32),
                   jax.ShapeDtypeStruct((B, CONV_HALO, CONV_DIM), F32)],
        scratch_shapes=[pltpu.VMEM((CONV_HALO + tm, CONV_DIM), F32)],
        compiler_params=_cp(("parallel", "arbitrary")),
        name="merge",
    )(*[a for a, _ in ins])


def _moe_kernel(h2_ref, comb_ref, x1_ref, gt_ref, wgu_ref, wd_ref, fg_ref, o_ref, acc, *, final, rb):
    s = pl.program_id(2)
    tm = h2_ref.shape[1]
    nrb = tm // rb
    DE = D_EXPERT

    @pl.when(s == 0)
    def _():
        acc[...] = jnp.zeros_like(acc)

    wgu = wgu_ref[0]
    wd = wd_ref[0]
    gus = {}

    def up(i):
        gus[i] = _dot(h2_ref[0, i * rb:(i + 1) * rb, :], wgu)

    up(0)
    for i in range(nrb):
        if i + 1 < nrb:
            up(i + 1)
        gu = gus.pop(i)
        rows = slice(i * rb, (i + 1) * rb)
        comb = comb_ref[0, rows, :]
        lane = lax.broadcasted_iota(jnp.int32, comb.shape, 1)
        cws = [jnp.sum(jnp.where(lane == MOE_EPS * s + j, comb, 0.0), axis=1, keepdims=True) for j in range(MOE_EPS)]
        cw = jnp.concatenate([jnp.broadcast_to(c, (rb, DE)) for c in cws], axis=1)
        gate = gu[:, 0:MOE_EPS * DE]
        act = gate * _sigmoid(gate) * gu[:, MOE_EPS * DE:2 * MOE_EPS * DE] * cw
        acc[rows, :] += _dot(act, wd)

    @pl.when(s == N_EXPERTS // MOE_EPS - 1)
    def _():
        x2 = x1_ref[0] + gt_ref[0] * acc[...]
        o_ref[0] = _rms(x2, fg_ref[...]) if final else x2


def _moe(h2, comb, x1, gt, lw, final_g, tm, final):
    B, T, D = x1.shape
    gt_rows = gt.shape[1]
    gt_spec = (pl.BlockSpec((1, 1, D), lambda b, i, e: (b, 0, 0)) if gt_rows == 1
               else pl.BlockSpec((1, tm, D), lambda b, i, e: (b, i, 0)))
    row = lambda w: pl.BlockSpec((1, tm, w), lambda b, i, e: (b, i, 0))
    return pl.pallas_call(
        functools.partial(_moe_kernel, final=final, rb=min(tm, 256)),
        grid=(B, T // tm, N_EXPERTS // MOE_EPS),
        in_specs=[row(D), row(ROUTER_LANES), row(D), gt_spec,
                  pl.BlockSpec((1, D, 2 * MOE_EPS * D_EXPERT), lambda b, i, e: (e, 0, 0)),
                  pl.BlockSpec((1, MOE_EPS * D_EXPERT, D), lambda b, i, e: (e, 0, 0)),
                  pl.BlockSpec((1, D), lambda b, i, e: (0, 0))],
        out_specs=row(D),
        out_shape=jax.ShapeDtypeStruct((B, T, D), F32),
        scratch_shapes=[pltpu.VMEM((tm, D), F32)],
        compiler_params=_cp(("parallel", "parallel", "arbitrary")),
        name="moe",
    )(h2, comb, x1, gt, lw['wgu'], lw['wd'], final_g)


def _layer_weights(l, w):
    D = D_MODEL
    w_in = w['w_in'][l]
    c_q, c_kv, c_kr = Q_LORA, Q_LORA + KV_LORA, MLA_IN
    c_rw, c_cv = MLA_IN + RWKV_IN, MLA_IN + RWKV_IN + 2 * CONV_DIM
    half = ROPE_DIM // 2
    swap = np.concatenate([np.arange(half, ROPE_DIM), np.arange(half)])
    wkr = w_in[:, c_kv:c_kr]
    q_up = w['q_up'][l].reshape(Q_LORA, MLA_HEADS, NOPE_DIM + ROPE_DIM)
    q_rope = q_up[:, :, NOPE_DIM:]
    wukq = jnp.zeros((KV_LORA, PAIR_KEYS), F32)
    for hd in range(MLA_HEADS):
        wukq = wukq.at[:, 128 * hd:128 * hd + NOPE_DIM].set(w['w_uk'][l][:, hd, :])
    zl = jnp.zeros((W_LORA, RWKV_DIM), F32)
    head = np.arange(RWKV_DIM) // RWKV_HEAD
    wr = jnp.zeros((D, ROUTER_LANES), F32)
    wr = wr.at[:, 0:N_EXPERTS].set(w['router_expert_w'][l]).at[:, N_EXPERTS:N_EXPERTS + N_GROUPS].set(
        w['router_group_w'][l])
    br = jnp.zeros((1, ROUTER_LANES), F32)
    br = br.at[0, 0:N_EXPERTS].set(w['router_expert_b'][l]).at[0, N_EXPERTS:N_EXPERTS + N_GROUPS].set(
        w['router_group_b'][l])
    r2 = lambda a: a.reshape(1, -1)
    pair_cols = lambda a: jnp.concatenate([a[j::MOE_EPS] for j in range(MOE_EPS)], axis=2)
    return dict(
        g1=r2(w['norm1_g'][l]), g2n=r2(w['norm2_g'][l]),
        wq=w_in[:, :c_q].astype(BF16), wkv=w_in[:, c_q:c_kv].astype(BF16),
        wkr=jnp.concatenate([jnp.tile(wkr, (1, 8)), jnp.tile(wkr[:, swap], (1, 8))], axis=1).astype(BF16),
        wrw=w_in[:, c_kr:c_rw].astype(BF16), wcv=w_in[:, c_rw:c_cv].astype(BF16), wg=w_in[:, c_cv:].astype(BF16),
        wkvt=w_in[:, c_q:c_kv].T.astype(BF16), kvgc=w['kv_norm_g'][l].reshape(-1, 1),
        qg=r2(w['q_norm_g'][l]), wqnt=q_up[:, :, :NOPE_DIM].reshape(Q_LORA, -1).T.astype(BF16),
        wukq=wukq.astype(BF16), kvg=r2(w['kv_norm_g'][l]),
        wqrt=q_rope.reshape(Q_LORA, -1).T.astype(BF16), wqrst=q_rope[:, :, swap].reshape(Q_LORA, -1).T.astype(BF16),
        wuvt=jnp.transpose(w['w_uv'][l], (1, 2, 0)).astype(BF16), proj_at=w['proj_a'][l].T.astype(BF16),
        mu=r2(w['rwkv_mu'][l]), w0=r2(w['rwkv_w0'][l]), a0=r2(w['rwkv_a0'][l]),
        w2p=jnp.concatenate([w['rwkv_w2'][l], zl], axis=0).astype(BF16),
        a2p=jnp.concatenate([zl, w['rwkv_a2'][l]], axis=0).astype(BF16),
        g2=w['rwkv_g2'][l].astype(BF16), k_k=r2(w['rwkv_k_k'][l]), k_a=r2(w['rwkv_k_a'][l]),
        r_k=r2(w['rwkv_r_k'][l]), ln_g=r2(w['rwkv_ln_g'][l]), ln_b=r2(w['rwkv_ln_b'][l]),
        ones_bd=jnp.asarray((head[:, None] == head[None, :]).astype(np.float32)).astype(BF16),
        proj_b=w['proj_b'][l].astype(BF16),
        conv_dw=w['conv_dw'][l], conv_dw_b=r2(w['conv_dw_b'][l]), conv_ln_g=r2(w['conv_ln_g'][l]),
        conv_ln_b=r2(w['conv_ln_b'][l]), proj_c=w['proj_c'][l].astype(BF16),
        w_out=w['w_out'][l].astype(BF16), wr=wr, br=br,
        wgu=jnp.concatenate([pair_cols(w['moe_w_gate'][l]), pair_cols(w['moe_w_up'][l])], axis=2).astype(BF16),
        wd=w['moe_w_down'][l].reshape(N_EXPERTS // MOE_EPS, MOE_EPS * D_EXPERT, D).astype(BF16),
    )


def _rope_tables(past, T):
    half = ROPE_DIM // 2
    inv = ROPE_THETA ** (-jnp.arange(half, dtype=F32) / half)
    ang = (past + jnp.arange(T)).astype(F32)[:, None] * inv[None, :]
    cos, sin = jnp.cos(ang), jnp.sin(ang)
    cos8 = jnp.tile(jnp.concatenate([cos, cos], axis=1), (1, 8))
    sin8 = jnp.tile(jnp.concatenate([-sin, sin], axis=1), (1, 8))
    return cos8, sin8, cos8.T, sin8.T


def _row_tile(T, cap):
    return min(T, cap)


def _trunk_layer(x, mod, cache_lat, cache_rope, shift_st, wkv_st, conv_st, lw, final_g, final):
    B, T, D = x.shape
    past = 0 if cache_lat is None else cache_lat.shape[1]
    tm = _row_tile(T, 512)
    lat, kr, keys, latt, qt, prw, glu = _in_proj(x, mod, lw, _rope_tables(past, T), tm)

    _, t_pad, _, s_pad, n_keys = _attn_tiles(T, past)
    row_pad = ((0, 0), (0, s_pad - n_keys), (0, 0))
    if past:
        lat_all = jnp.pad(jnp.concatenate([cache_lat, lat], axis=1), row_pad)
        rope_all = jnp.pad(jnp.concatenate([cache_rope, kr], axis=1), row_pad)
        key_tile = max(d for d in range(8, 641, 8) if s_pad % d == 0)
        keys = _key_up(lat_all, jnp.tile(rope_all, (1, 1, 8)), lw['wukq'], key_tile)
        old_t = jnp.concatenate([jnp.swapaxes(cache_lat, 1, 2).astype(BF16),
                                 jnp.ones((B, VT_ROWS - KV_LORA, past), BF16)], axis=1)
        latt = jnp.concatenate([old_t, latt], axis=2)
    elif s_pad != n_keys:
        keys = jnp.pad(keys, row_pad)
    if s_pad != n_keys:
        latt = jnp.pad(latt, ((0, 0), (0, 0), (0, s_pad - n_keys)))
    if t_pad != T:
        qt = jnp.pad(qt, ((0, 0), (0, 0), (0, 0), (0, t_pad - T)))
    o_a = _attention(qt, keys, latt, lw['wuvt'], lw['proj_at'], past, T)[:, :T]

    L = WKV_CHUNK
    Tp = -(-T // L) * L
    prw_p = prw if Tp == T else jnp.pad(prw, ((0, 0), (0, Tp - T), (0, 0)))
    shift0 = jnp.zeros((B, 1, RWKV_IN), F32) if shift_st is None else shift_st[:, None, :]
    W = RWKV_DIM
    s0 = jnp.zeros((B, W, W), F32)
    if wkv_st is not None:
        for hd in range(RWKV_HEADS):
            o = hd * RWKV_HEAD
            s0 = s0.at[:, o:o + RWKV_HEAD, o:o + RWKV_HEAD].set(wkv_st[:, hd])
    yb, s_bd = _wkv(prw_p, shift0, s0, lw, L if Tp == T else T)
    yb = yb[:, :T]
    wkv_new = jnp.stack([s_bd[:, hd * RWKV_HEAD:(hd + 1) * RWKV_HEAD, hd * RWKV_HEAD:(hd + 1) * RWKV_HEAD]
                         for hd in range(RWKV_HEADS)], axis=1)

    pad_rows = CONV_HALO - (CONV_WIDTH - 1)
    st = jnp.zeros((B, CONV_WIDTH - 1, CONV_DIM), F32) if conv_st is None else conv_st
    x1, h2, comb, conv_tail = _merge(x, mod, o_a, yb, glu, jnp.pad(st, ((0, 0), (pad_rows, 0), (0, 0))), lw, tm)
    conv_new = conv_tail[:, pad_rows:]

    if T >= 1024:
        x2 = _moe(h2, comb, x1, mod[:, 5:6], lw, final_g, 1024, final)
    else:
        gt = jnp.broadcast_to(mod[:, 5:6], (B, T, D)).reshape(1, B * T, D)
        x2 = _moe(h2.reshape(1, B * T, D), comb.reshape(1, B * T, ROUTER_LANES), x1.reshape(1, B * T, D), gt, lw,
                  final_g, B * T, final).reshape(B, T, D)
    return x2, (lat, kr, prw[:, -1], wkv_new, conv_new)


def kernel(x_prompt, x_sample, c_prompt, c_sample, cache_kv_latent, cache_k_rope, state_rwkv_shift, state_rwkv_wkv, state_conv, ada_w, ada_b, norm1_g, norm2_g, w_in, q_norm_g, q_up, kv_norm_g, w_uk, w_uv, proj_a, rwkv_mu, rwkv_w0, rwkv_w2, rwkv_a0, rwkv_a2, rwkv_g2, rwkv_k_k, rwkv_k_a, rwkv_r_k, rwkv_ln_g, rwkv_ln_b, proj_b, conv_dw, conv_dw_b, conv_ln_g, conv_ln_b, proj_c, w_out, router_group_w, router_group_b, router_expert_w, router_expert_b, moe_w_gate, moe_w_up, moe_w_down, final_g):
    w = dict(norm1_g=norm1_g, norm2_g=norm2_g, w_in=w_in, q_norm_g=q_norm_g, q_up=q_up, kv_norm_g=kv_norm_g,
             w_uk=w_uk, w_uv=w_uv, proj_a=proj_a, rwkv_mu=rwkv_mu, rwkv_w0=rwkv_w0, rwkv_w2=rwkv_w2,
             rwkv_a0=rwkv_a0, rwkv_a2=rwkv_a2, rwkv_g2=rwkv_g2, rwkv_k_k=rwkv_k_k, rwkv_k_a=rwkv_k_a,
             rwkv_r_k=rwkv_r_k.reshape(rwkv_r_k.shape[0], -1), rwkv_ln_g=rwkv_ln_g, rwkv_ln_b=rwkv_ln_b,
             proj_b=proj_b, conv_dw=conv_dw, conv_dw_b=conv_dw_b, conv_ln_g=conv_ln_g, conv_ln_b=conv_ln_b,
             proj_c=proj_c, w_out=w_out, router_group_w=router_group_w, router_group_b=router_group_b,
             router_expert_w=router_expert_w, router_expert_b=router_expert_b, moe_w_gate=moe_w_gate,
             moe_w_up=moe_w_up, moe_w_down=moe_w_down)
    depth = ada_w.shape[0]
    bp = x_prompt.shape[0]
    D = D_MODEL
    mod_all = _ada_mod(jnp.concatenate([c_prompt, c_sample], axis=0), ada_w, ada_b)
    fg = final_g.reshape(1, D)
    hp, hs = x_prompt, x_sample
    new_p, new_s = [], []
    for l in range(depth):
        lw = _layer_weights(l, w)
        mod_p = mod_all[l, :bp].reshape(bp, N_MOD, D)
        mod_s = mod_all[l, bp:].reshape(-1, N_MOD, D)
        final = l == depth - 1
        hp, st_p = _trunk_layer(hp, mod_p, None, None, None, None, None, lw, fg, final)
        hs, st_s = _trunk_layer(hs, mod_s, cache_kv_latent[l], cache_k_rope[l], state_rwkv_shift[l],
                                state_rwkv_wkv[l], state_conv[l], lw, fg, final)
        new_p.append(st_p)
        new_s.append(st_s)
    stack = lambda states, i: jnp.stack([s[i] for s in states], axis=0)
    return ((hp, hs) + tuple(stack(new_p, i) for i in range(5)) + tuple(stack(new_s, i) for i in range(5)))
```

```python
import functools

import numpy as np
import jax
import jax.numpy as jnp
from jax import lax
from jax.experimental import pallas as pl
from jax.experimental.pallas import tpu as pltpu

F32 = jnp.float32
BF16 = jnp.bfloat16

D_MODEL = 1024
CHUNK = 64
NORM_EPS = 1e-6
MLA_HEADS = 8
Q_LORA = 384
KV_LORA = 256
NOPE_DIM = 64
ROPE_DIM = 32
V_HEAD = 64
ROPE_THETA = 10000.0
SM_SCALE = (NOPE_DIM + ROPE_DIM) ** -0.5
LOG2_E = 1.4426950408889634
RWKV_HEADS = 4
RWKV_HEAD = 64
RWKV_DIM = RWKV_HEADS * RWKV_HEAD
W_LORA = 64
A_LORA = 64
G_LORA = 128
RWKV_IN = 3 * RWKV_DIM + W_LORA + A_LORA + G_LORA
GN_EPS = 64e-5
CONV_DIM = 256
CONV_WIDTH = 31
LN_EPS = 1e-5
MLA_IN = Q_LORA + KV_LORA + ROPE_DIM
N_MOD = 6
N_GROUPS = 4
EXPERTS_PER_GROUP = 4
N_EXPERTS = 16
D_EXPERT = 256
WKV_CHUNK = 64
CONV_HALO = 32
ROUTER_LANES = 128
PAIR_KEYS = 4 * 256
VT_ROWS = KV_LORA + 16
MOE_EPS = 4
QK_AHEAD = 8
VMEM_LIMIT = 56 * 1024 * 1024


def _cp(sem):
    return pltpu.CompilerParams(dimension_semantics=sem, vmem_limit_bytes=VMEM_LIMIT)


def _full(a):
    nd = a.ndim
    return pl.BlockSpec(a.shape, lambda *_: (0,) * nd)


def _dot(a, b, dims=(((1,), (0,)), ((), ()))):
    return lax.dot_general(a.astype(BF16), b.astype(BF16), dims, preferred_element_type=F32)


def _split(a):
    hi = a.astype(BF16)
    lo = (a - hi.astype(F32)).astype(BF16)
    return hi, lo


def _dot_hl(a, b, dims=(((1,), (0,)), ((), ()))):
    hi, lo = _split(a)
    bb = b.astype(BF16)
    return (lax.dot_general(hi, bb, dims, preferred_element_type=F32)
            + lax.dot_general(lo, bb, dims, preferred_element_type=F32))


def _dot_split(a, b):
    m, n = a.shape[0], b.shape[1]
    ah, al = _split(a)
    bh, bl = _split(b)
    p = jnp.dot(jnp.concatenate([ah, al], axis=0), jnp.concatenate([bh, bl], axis=1), preferred_element_type=F32)
    return (p[0:m, 0:n] + p[0:m, n:2 * n]) + (p[m:2 * m, 0:n] + p[m:2 * m, n:2 * n])


def _sigmoid(x):
    return 0.5 * jnp.tanh(0.5 * x) + 0.5


def _rms(x, g):
    return x * lax.rsqrt(jnp.mean(x * x, axis=-1, keepdims=True) + NORM_EPS) * g


def _ada_kernel(c_ref, w_ref, b_ref, o_ref):
    c = c_ref[...]
    o_ref[0] = _dot(c * _sigmoid(c), w_ref[0]) + b_ref[0]


def _ada_mod(c_all, ada_w, ada_b):
    L, D, C = ada_w.shape
    R = c_all.shape[0]
    nb = C // D
    return pl.pallas_call(
        _ada_kernel,
        grid=(L, nb),
        in_specs=[pl.BlockSpec((R, D), lambda l, j: (0, 0)),
                  pl.BlockSpec((1, D, D), lambda l, j: (l, 0, j)),
                  pl.BlockSpec((1, 1, D), lambda l, j: (l, 0, j))],
        out_specs=pl.BlockSpec((1, R, D), lambda l, j: (l, 0, j)),
        out_shape=jax.ShapeDtypeStruct((L, R, C), F32),
        compiler_params=_cp(("parallel", "parallel")),
        name="ada_mod",
    )(c_all, ada_w, ada_b.reshape(L, 1, C))


_NT = (((1,), (1,)), ((), ()))


def _pair_keys(lat, rope8, wukq):
    lane = lax.broadcasted_iota(jnp.int32, rope8.shape, 1) & 127
    rope_slot = jnp.where((lane >= NOPE_DIM) & (lane < NOPE_DIM + ROPE_DIM), rope8, 0.0)
    return (_dot(lat, wukq) + jnp.concatenate([rope_slot] * (MLA_HEADS // 2), axis=1)).astype(BF16)


def _in_proj_kernel(x_ref, mod_ref, g1_ref, wq_ref, wkv_ref, wkvt_ref, wkr_ref, wrw_ref, wcv_ref, qg_ref, wqnt_ref,
                    wqrt_ref, wqrst_ref, wukq_ref, kvg_ref, kvgc_ref, cos_ref, sin_ref, cost_ref, sint_ref,
                    lat_ref, kr_ref, kq_ref, latt_ref, qt_ref, prw_ref, glu_ref):
    x = x_ref[0]
    mod = mod_ref[0]
    tm = x.shape[0]
    h = _rms(x, g1_ref[...]) * (1.0 + mod[1:2]) + mod[0:1]
    hb = h.astype(BF16)
    qn = _rms(_dot(hb, wq_ref[...]), qg_ref[...]).astype(BF16)
    qnt = _dot(wqnt_ref[...], qn, _NT)
    qrt = _dot(wqrt_ref[...], qn, _NT)
    qrst = _dot(wqrst_ref[...], qn, _NT)
    qrope = qrt * cost_ref[...] + qrst * sint_ref[...]
    zpad = jnp.zeros((128 - NOPE_DIM - ROPE_DIM, tm), F32)
    for hd in range(MLA_HEADS):
        qh = jnp.concatenate([qnt[NOPE_DIM * hd:NOPE_DIM * (hd + 1)], qrope[ROPE_DIM * hd:ROPE_DIM * (hd + 1)],
                              zpad], axis=0)
        qt_ref[0, hd] = (qh * (SM_SCALE * LOG2_E)).astype(BF16)
    lat = _rms(_dot(hb, wkv_ref[...]), kvg_ref[...])
    lat_ref[0] = lat
    if tm % 128 == 0:
        latt = lat.T
    else:
        pkvt = _dot(wkvt_ref[...], hb, _NT)
        latt = pkvt * lax.rsqrt(jnp.mean(pkvt * pkvt, axis=0, keepdims=True) + NORM_EPS) * kvgc_ref[...]
    latt_ref[0, 0:KV_LORA, :] = latt.astype(BF16)
    latt_ref[0, KV_LORA:VT_ROWS, :] = jnp.ones((VT_ROWS - KV_LORA, tm), BF16)
    pkr = _dot(hb, wkr_ref[...])
    krt = pkr[:, 0:256] * cos_ref[...] + pkr[:, 256:512] * sin_ref[...]
    kr_ref[0] = krt[:, 0:ROPE_DIM]
    kq_ref[0] = _pair_keys(lat, krt, wukq_ref[...])
    prw_ref[0] = _dot(hb, wrw_ref[...])
    pcv = _dot(hb, wcv_ref[...])
    glu_ref[0] = pcv[:, 0:CONV_DIM] * _sigmoid(pcv[:, CONV_DIM:2 * CONV_DIM])


def _in_proj(x, mod, lw, tabs, tm):
    B, T, D = x.shape
    nT = T // tm
    cos8, sin8, cos8t, sin8t = tabs
    row = lambda w: pl.BlockSpec((1, tm, w), lambda b, i: (b, i, 0))
    col = pl.BlockSpec((1, VT_ROWS, tm), lambda b, i: (b, 0, i))
    args = (lw['g1'], lw['wq'], lw['wkv'], lw['wkvt'], lw['wkr'], lw['wrw'], lw['wcv'], lw['qg'], lw['wqnt'],
            lw['wqrt'], lw['wqrst'], lw['wukq'], lw['kvg'], lw['kvgc'])
    return pl.pallas_call(
        _in_proj_kernel,
        grid=(B, nT),
        in_specs=[row(D), pl.BlockSpec((1, N_MOD, D), lambda b, i: (b, 0, 0))] + [_full(a) for a in args]
                 + [pl.BlockSpec((tm, 256), lambda b, i: (i, 0))] * 2
                 + [pl.BlockSpec((256, tm), lambda b, i: (0, i))] * 2,
        out_specs=[row(KV_LORA), row(ROPE_DIM), row(PAIR_KEYS), col,
                   pl.BlockSpec((1, MLA_HEADS, 128, tm), lambda b, i: (b, 0, 0, i)),
                   row(RWKV_IN), row(CONV_DIM)],
        out_shape=[jax.ShapeDtypeStruct((B, T, KV_LORA), F32),
                   jax.ShapeDtypeStruct((B, T, ROPE_DIM), F32),
                   jax.ShapeDtypeStruct((B, T, PAIR_KEYS), BF16),
                   jax.ShapeDtypeStruct((B, VT_ROWS, T), BF16),
                   jax.ShapeDtypeStruct((B, MLA_HEADS, 128, T), BF16),
                   jax.ShapeDtypeStruct((B, T, RWKV_IN), F32),
                   jax.ShapeDtypeStruct((B, T, CONV_DIM), F32)],
        compiler_params=_cp(("parallel", "parallel")),
        name="in_proj",
    )(x, mod, *args, cos8, sin8, cos8t, sin8t)


def _key_up_kernel(lat_ref, rope8_ref, wukq_ref, kq_ref):
    kq_ref[0] = _pair_keys(lat_ref[0], rope8_ref[0], wukq_ref[...])


def _key_up(lat, rope8, wukq, tm):
    B, S, _ = lat.shape
    row = lambda w: pl.BlockSpec((1, tm, w), lambda b, i: (b, i, 0))
    return pl.pallas_call(
        _key_up_kernel,
        grid=(B, S // tm),
        in_specs=[row(KV_LORA), row(256), _full(wukq)],
        out_specs=row(PAIR_KEYS),
        out_shape=jax.ShapeDtypeStruct((B, S, PAIR_KEYS), BF16),
        compiler_params=_cp(("parallel", "parallel")),
        name="key_up",
    )(lat, rope8, wukq)


def _attn_kernel(qi_ref, ki_ref, last_ref, qt_ref, k_ref, vt_ref, wuvt_ref, pat_ref, o_ref,
                 qcat, m_s, acc, *, tq, tk, past, n_keys):
    p = pl.program_id(1)
    qi = qi_ref[p]
    ki = ki_ref[p]
    H = MLA_HEADS
    NP = H // 2
    nh = tq // 128
    NC = NP * nh
    diag_aligned = tq == tk and past % tk == 0

    @pl.when(ki == 0)
    def _():
        qcat[...] = jnp.zeros(qcat.shape, BF16)
        for pr in range(NP):
            for hf in range(nh):
                pos = slice(128 * hf, 128 * (hf + 1))
                qcat[pr * nh + hf, 0:128, 0:128] = qt_ref[0, 2 * pr, :, pos]
                qcat[pr * nh + hf, 128:256, 128:256] = qt_ref[0, 2 * pr + 1, :, pos]
        m_s[...] = jnp.full(m_s.shape, -jnp.inf, F32)
        acc[...] = jnp.zeros(acc.shape, F32)

    def step(bias):
        def nkeys(ch):
            return 128 * (ch % nh + 1) if (bias is not None and diag_aligned) else tk

        scores = {}

        def qk(ch):
            pr = ch // nh
            scores[ch] = jnp.dot(k_ref[0, 0:nkeys(ch), 256 * pr:256 * (pr + 1)], qcat[ch],
                                 preferred_element_type=F32)

        for ch in range(min(QK_AHEAD, NC)):
            qk(ch)
        for ch in range(NC):
            if ch + QK_AHEAD < NC:
                qk(ch + QK_AHEAD)
            s = scores.pop(ch)
            if bias is not None:
                s = s + bias[ch % nh][0:nkeys(ch), :]
            m_prev = m_s[ch]
            m_new = jnp.maximum(m_prev, jnp.max(s, axis=0, keepdims=True))
            alpha = jnp.exp2(m_prev - m_new)
            pe = jnp.exp2(s - m_new)
            acc[ch] = alpha * acc[ch] + jnp.dot(vt_ref[0, :, 0:nkeys(ch)], pe.astype(BF16),
                                                preferred_element_type=F32)
            m_s[ch] = m_new

    is_last = last_ref[p] == 1

    @pl.when(jnp.logical_not(is_last))
    def _():
        step(None)

    @pl.when(is_last)
    def _():
        kpos = ki * tk + lax.broadcasted_iota(jnp.int32, (tk, 1), 0)
        biases = []
        for hf in range(nh):
            t = 128 * hf + lax.broadcasted_iota(jnp.int32, (1, 128), 1)
            limit = jnp.minimum((((past + qi * tq + t) >> 6) + 1) << 6, n_keys)
            bias = jnp.where(kpos < limit, 0.0, -jnp.inf)
            biases.append(jnp.concatenate([bias, bias], axis=1))
        step(biases)
        heads = []
        for pr in range(NP):
            o = [acc[pr * nh + hf, 0:KV_LORA, :] / acc[pr * nh + hf, KV_LORA:KV_LORA + 1, :]
                 for hf in range(nh)]
            for j in range(2):
                oh = jnp.concatenate([x[:, 128 * j:128 * (j + 1)] for x in o], axis=1)
                heads.append(_dot(wuvt_ref[2 * pr + j], oh))
        out_t = _dot(pat_ref[...], jnp.concatenate(heads, axis=0))
        o_ref[0] = out_t.T


def _attn_tiles(T, past):
    n_keys = past + T
    tq = 512 if T % 512 == 0 else 128
    t_pad = -(-T // tq) * tq
    tk = -(-n_keys // 128) * 128 if n_keys <= 2304 else 512
    s_pad = -(-n_keys // tk) * tk
    return tq, t_pad, tk, s_pad, n_keys


def _attention(qt, keys, latt, wuvt, proj_at, past, T):
    B, H = qt.shape[:2]
    tq, t_pad, tk, s_pad, n_keys = _attn_tiles(T, past)
    assert keys.shape[1] == s_pad and latt.shape[2] == s_pad and qt.shape[3] == t_pad and past % CHUNK == 0
    qi, ki, last = [], [], []
    for i in range(t_pad // tq):
        lim = min(((past + (i + 1) * tq - 1) // CHUNK + 1) * CHUNK, n_keys)
        nk = -(-lim // tk)
        for j in range(nk):
            qi.append(i), ki.append(j), last.append(int(j == nk - 1))
    npairs = len(qi)
    tabs = [jnp.asarray(np.asarray(a, np.int32)) for a in (qi, ki, last)]
    nc = (H // 2) * (tq // 128)
    grid_spec = pltpu.PrefetchScalarGridSpec(
        num_scalar_prefetch=3,
        grid=(B, npairs),
        in_specs=[pl.BlockSpec((1, H, 128, tq), lambda b, p, qi, ki, la: (b, 0, 0, qi[p])),
                  pl.BlockSpec((1, tk, PAIR_KEYS), lambda b, p, qi, ki, la: (b, ki[p], 0)),
                  pl.BlockSpec((1, VT_ROWS, tk), lambda b, p, qi, ki, la: (b, 0, ki[p])),
                  pl.BlockSpec(wuvt.shape, lambda b, p, qi, ki, la: (0, 0, 0)),
                  pl.BlockSpec(proj_at.shape, lambda b, p, qi, ki, la: (0, 0))],
        out_specs=pl.BlockSpec((1, tq, D_MODEL), lambda b, p, qi, ki, la: (b, qi[p], 0)),
        scratch_shapes=[pltpu.VMEM((nc, 256, 256), BF16), pltpu.VMEM((nc, 1, 256), F32),
                        pltpu.VMEM((nc, VT_ROWS, 256), F32)])
    return pl.pallas_call(
        functools.partial(_attn_kernel, tq=tq, tk=tk, past=past, n_keys=n_keys),
        grid_spec=grid_spec,
        out_shape=jax.ShapeDtypeStruct((B, t_pad, D_MODEL), F32),
        compiler_params=_cp(("parallel", "arbitrary")),
        name="attention",
    )(*tabs, qt, keys, latt, wuvt, proj_at)


def _head_sum(x, ones_bd):
    return _dot_hl(x, ones_bd)


def _rwkv_features(pr, prev, wts):
    mu, w0, w2p, a0, a2p, g2, k_k, k_a, r_k, ones_bd = wts
    xs = pr + (prev - pr) * mu
    r = xs[:, 0:256]
    k = xs[:, 256:512]
    v = xs[:, 512:768]
    wa = xs[:, 768:896]
    gl = xs[:, 896:1024]
    nz = -(w0 + _dot(jnp.tanh(wa), w2p))
    softplus = jnp.maximum(nz, 0.0) + jnp.log(1.0 + jnp.exp(-jnp.abs(nz)))
    lw = -jnp.exp(-softplus - 0.5)
    a = _sigmoid(a0 + _dot(wa, a2p))
    g = _dot(_sigmoid(gl), g2)
    kk = k * k_k
    kk = kk * lax.rsqrt(_head_sum(kk * kk, ones_bd) + 1e-12)
    kf = k * (1.0 + (a - 1.0) * k_a)
    bonus = _head_sum(r * kf * r_k, ones_bd) * v
    return r, kf, kk * a, kk, v, lw, g, bonus


def _wkv_kernel(prw_ref, shift_ref, s0_ref, mu_ref, w0_ref, w2_ref, a0_ref, a2_ref, g2_ref, kk_ref, ka_ref, rk_ref,
                ones_ref, lng_ref, lnb_ref, yb_ref, sout_ref, S, carry, *, nb, G, n_valid):
    L = WKV_CHUNK
    W = RWKV_DIM

    @pl.when(pl.program_id(0) == 0)
    def _():
        S[...] = s0_ref[...]
        carry[...] = shift_ref[...]

    wts = tuple(ref[...] for ref in (mu_ref, w0_ref, w2_ref, a0_ref, a2_ref, g2_ref, kk_ref, ka_ref, rk_ref, ones_ref))
    ones_bd = wts[-1]
    ln_g = lng_ref[...]
    ln_b = lnb_ref[...]
    ri = lax.broadcasted_iota(jnp.int32, (W, W), 0)
    ci = lax.broadcasted_iota(jnp.int32, (W, W), 1)
    same = (ri >> 6) == (ci >> 6)
    strict = same & ((ci & 63) < (ri & 63))
    incl = same & ((ci & 63) <= (ri & 63))
    eye = (ri == ci).astype(F32)
    tri = (lax.broadcasted_iota(jnp.int32, (L, L), 1) <= lax.broadcasted_iota(jnp.int32, (L, L), 0)).astype(BF16)
    lane_head = lax.broadcasted_iota(jnp.int32, (L, W), 1) >> 6
    row = lax.broadcasted_iota(jnp.int32, (L, 1), 0)
    nt = (((1,), (1,)), ((), ()))
    tn = (((0,), (0,)), ((), ()))

    def stack(x):
        return jnp.concatenate([jnp.where(lane_head == hd, x, 0.0) for hd in range(RWKV_HEADS)], axis=0)

    def collapse(z):
        return z[0:L] + z[L:2 * L] + z[2 * L:3 * L] + z[3 * L:4 * L]

    def features(i):
        pr = prw_ref[i]
        prev = jnp.where(row == 0, carry[i], pltpu.roll(pr, 1, 0))
        carry[i] = pr[L - 1:L, :]
        f = _rwkv_features(pr, prev, wts)
        if n_valid < L:
            f = tuple(jnp.where(row < n_valid, x, 0.0) for x in f[:6]) + f[6:]
        return f

    def body(it, _):
        ids = [it * G + g for g in range(G)]
        each = lambda f, *xs: [f(*a) for a in zip(*xs)]
        r, kf, bb, kk, v, ll, gate, bonus = zip(*[features(i) for i in ids])
        s_prev = [S[i] for i in ids]
        split = each(_split, ll)
        c = each(lambda hl: jnp.dot(tri, hl[0], preferred_element_type=F32)
                 + jnp.dot(tri, hl[1], preferred_element_type=F32), split)
        c_last = each(lambda x: x[L - 1:L, :], c)
        e_neg = each(lambda x: jnp.exp(-x), c)
        e_end = each(lambda x, y: jnp.exp(y - x), c, c_last)
        rt = each(lambda x, y: stack(x * jnp.exp(y)), r, c)
        kkt = each(lambda x, y, z: stack(x * jnp.exp(y - z)), kk, c, ll)
        bt = each(lambda x, y: stack(x * y), bb, e_neg)
        kt = each(lambda x, y: stack(x * y), kf, e_neg)
        a_m = each(lambda x, y: jnp.where(strict, _dot(x, y, nt), 0.0), kkt, bt)
        b_m = each(lambda x, y: jnp.where(strict, _dot(x, y, nt), 0.0), kkt, kt)
        m1 = each(lambda x, y: jnp.where(incl, _dot(x, y, nt), 0.0), rt, bt)
        m2 = each(lambda x, y: jnp.where(incl, _dot(x, y, nt), 0.0), rt, kt)
        t_m = each(lambda x: eye - x, a_m)
        a_p = a_m
        for _ in range(5):
            a_p = each(lambda x: _dot(x, x), a_p)
            t_m = each(lambda x, y: x + _dot(x, y), t_m, a_p)
        vs = each(stack, v)
        w1 = each(lambda x, y: collapse(_dot(x, y)), t_m, kkt)
        bv = each(_dot, b_m, vs)
        w2 = each(lambda x, y: collapse(_dot(x, y)), t_m, bv)
        u = each(lambda x, y, z: -(_dot(x, y, nt) + z), w1, s_prev, w2)
        y0 = each(lambda x, y: _dot(collapse(x), y, nt), rt, s_prev)
        y1 = each(lambda a, b_, c_, d: collapse(_dot(a, stack(b_)) + _dot(c_, d)), m1, u, m2, vs)
        upd = each(lambda a, b_, c_, d, e: _dot(a, b_ * e, tn) + _dot(c_, d * e, tn), u, bb, v, kf, e_end)
        for g, i in enumerate(ids):
            S[i] = s_prev[g] * jnp.exp(c_last[g]) + jnp.where(same, upd[g], 0.0)
            y = y0[g] + y1[g]
            mean = _head_sum(y, ones_bd) * (1.0 / RWKV_HEAD)
            yc = y - mean
            var = _head_sum(yc * yc, ones_bd) * (1.0 / RWKV_HEAD)
            yb = (yc * lax.rsqrt(var + GN_EPS) * ln_g + ln_b + bonus[g]) * gate[g]
            yb_ref[i] = yb.astype(BF16)
        return 0

    lax.fori_loop(0, nb // G, body, 0)

    @pl.when(pl.program_id(0) == pl.num_programs(0) - 1)
    def _():
        sout_ref[...] = S[...]


def _wkv(prw, shift0, s0, lw, n_valid):
    B, T, _ = prw.shape
    L = WKV_CHUNK
    W = RWKV_DIM
    assert T % L == 0 and (n_valid == L or T == L)
    args = (lw['mu'], lw['w0'], lw['w2p'], lw['a0'], lw['a2p'], lw['g2'], lw['k_k'], lw['k_a'], lw['r_k'],
            lw['ones_bd'], lw['ln_g'], lw['ln_b'])
    st = pl.BlockSpec((B, W, W), lambda c: (0, 0, 0))
    return pl.pallas_call(
        functools.partial(_wkv_kernel, nb=B, G=8 if B % 8 == 0 else 1, n_valid=n_valid),
        grid=(T // L,),
        in_specs=[pl.BlockSpec((B, L, RWKV_IN), lambda c: (0, c, 0)), _full(shift0), st] + [_full(a) for a in args],
        out_specs=[pl.BlockSpec((B, L, W), lambda c: (0, c, 0)), st],
        out_shape=[jax.ShapeDtypeStruct((B, T, W), BF16), jax.ShapeDtypeStruct((B, W, W), F32)],
        scratch_shapes=[pltpu.VMEM((B, W, W), F32), pltpu.VMEM((B, 1, RWKV_IN), F32)],
        compiler_params=_cp(("arbitrary",)),
        name="wkv",
    )(prw, shift0, s0, *args)


def _conv_stage(ext, glu, st_ref, first, tm):
    @pl.when(first)
    def _():
        ext[0:CONV_HALO, :] = st_ref[0]
    ext[CONV_HALO:CONV_HALO + tm, :] = glu


def _conv_module(ext, dw_ref, dwb, lng, lnb, new_ref, tm):
    off = CONV_HALO - (CONV_WIDTH - 1)
    ext_v = ext[...]
    n = CONV_HALO + tm
    acc = jnp.zeros((tm, CONV_DIM), F32)
    for res in range(8):
        taps = [w for w in range(CONV_WIDTH) if (off + w) % 8 == res]
        if not taps:
            continue
        sh = ext_v if res == 0 else pltpu.roll(ext_v, n - res, 0)
        for w in taps:
            base = off + w - res
            acc = acc + sh[base:base + tm, :] * dw_ref[w:w + 1, :]
    y = acc + dwb
    mu = jnp.mean(y, axis=-1, keepdims=True)
    var = jnp.mean(jnp.square(y - mu), axis=-1, keepdims=True)
    y = (y - mu) * lax.rsqrt(var + LN_EPS) * lng + lnb
    tail = ext[tm:tm + CONV_HALO, :]
    new_ref[0] = tail
    ext[0:CONV_HALO, :] = tail
    return y * _sigmoid(y)


def _merge_kernel(x_ref, mod_ref, g1_ref, wg_ref, oa_ref, yb_ref, pb_ref, glu_ref, st_ref, dw_ref, dwb_ref, clg_ref,
                  clb_ref, pc_ref, wo_ref, g2_ref, wr_ref, br_ref, x1_ref, h2_ref, comb_ref, cnew_ref, ext):
    x = x_ref[0]
    mod = mod_ref[0]
    tm = x.shape[0]
    _conv_stage(ext, glu_ref[0], st_ref, pl.program_id(1) == 0, tm)
    yc = _conv_module(ext, dw_ref, dwb_ref[...], clg_ref[...], clb_ref[...], cnew_ref, tm)
    hb = (_rms(x, g1_ref[...]) * (1.0 + mod[1:2]) + mod[0:1]).astype(BF16)
    D = D_MODEL
    merged = _sigmoid(_dot(hb, wg_ref[:, 0:D])) * oa_ref[0]
    merged = merged + _sigmoid(_dot(hb, wg_ref[:, D:2 * D])) * _dot(yb_ref[0], pb_ref[...])
    merged = merged + _sigmoid(_dot(hb, wg_ref[:, 2 * D:3 * D])) * _dot(yc, pc_ref[...])
    x1 = x + mod[2:3] * _dot(merged, wo_ref[...])
    x1_ref[0] = x1
    h2 = _rms(x1, g2_ref[...]) * (1.0 + mod[4:5]) + mod[3:4]
    h2_ref[0] = h2.astype(BF16)
    logits = _dot_split(h2, wr_ref[...]) + br_ref[...]
    lane = lax.broadcasted_iota(jnp.int32, logits.shape, 1)
    neg = -jnp.inf
    is_g = (lane >= N_EXPERTS) & (lane < N_EXPERTS + N_GROUPS)
    gl = jnp.where(is_g, logits, neg)
    gmax = jnp.max(gl, axis=1, keepdims=True)
    gsum = jnp.sum(jnp.exp(gl - gmax), axis=1, keepdims=True)
    g_val = 1.0 / gsum
    g_idx = jnp.min(jnp.where(is_g & (gl == gmax), lane, 4 * ROUTER_LANES), axis=1, keepdims=True) - N_EXPERTS
    in_grp = (lane >= g_idx * EXPERTS_PER_GROUP) & (lane < (g_idx + 1) * EXPERTS_PER_GROUP)
    el = jnp.where(in_grp, logits, neg)
    v1 = jnp.max(el, axis=1, keepdims=True)
    i1 = jnp.min(jnp.where(in_grp & (el == v1), lane, 4 * ROUTER_LANES), axis=1, keepdims=True)
    el2 = jnp.where(lane == i1, neg, el)
    v2 = jnp.max(el2, axis=1, keepdims=True)
    i2 = jnp.min(jnp.where(in_grp & (lane != i1) & (el2 == v2), lane, 4 * ROUTER_LANES), axis=1, keepdims=True)
    e2 = jnp.exp(v2 - v1)
    w1 = g_val / (1.0 + e2)
    w2 = g_val * e2 / (1.0 + e2)
    comb_ref[0] = jnp.where(lane == i1, w1, 0.0) + jnp.where(lane == i2, w2, 0.0)


def _merge(x, mod, oa, yb, glu, st_pad, lw, tm):
    B, T, D = x.shape
    row = lambda w: pl.BlockSpec((1, tm, w), lambda b, i: (b, i, 0))
    modspec = pl.BlockSpec((1, N_MOD, D), lambda b, i: (b, 0, 0))
    halo = pl.BlockSpec((1, CONV_HALO, CONV_DIM), lambda b, i: (b, 0, 0))
    ins = [(x, row(D)), (mod, modspec), (lw['g1'], None), (lw['wg'], None), (oa, row(D)), (yb, row(256)),
           (lw['proj_b'], None), (glu, row(CONV_DIM)), (st_pad, halo), (lw['conv_dw'], None),
           (lw['conv_dw_b'], None), (lw['conv_ln_g'], None), (lw['conv_ln_b'], None), (lw['proj_c'], None),
           (lw['w_out'], None), (lw['g2n'], None), (lw['wr'], None), (lw['br'], None)]
    return pl.pallas_call(
        _merge_kernel,
        grid=(B, T // tm),
        in_specs=[s if s is not None else _full(a) for a, s in ins],
        out_specs=[row(D), row(D), row(ROUTER_LANES), halo],
        out_shape=[jax.ShapeDtypeStruct((B, T, D), F32), jax.ShapeDtypeStruct((B, T, D), BF16),
                   jax.ShapeDtypeStruct((B, T, ROUTER_LANES), F32),
                   jax.ShapeDtypeStruct((B, CONV_HALO, CONV_DIM), F32)],
        scratch_shapes=[pltpu.VMEM((CONV_HALO + tm, CONV_DIM), F32)],
        compiler_params=_cp(("parallel", "arbitrary")),
        name="merge",
    )(*[a for a, _ in ins])


def _moe_kernel(h2_ref, comb_ref, x1_ref, gt_ref, wgu_ref, wd_ref, fg_ref, o_ref, acc, *, final, rb):
    s = pl.program_id(2)
    tm = h2_ref.shape[1]
    nrb = tm // rb
    DE = D_EXPERT

    @pl.when(s == 0)
    def _():
        acc[...] = jnp.zeros_like(acc)

    wgu = wgu_ref[0]
    wd = wd_ref[0]
    gus = {}

    def up(i):
        gus[i] = _dot(h2_ref[0, i * rb:(i + 1) * rb, :], wgu)

    up(0)
    for i in range(nrb):
        if i + 1 < nrb:
            up(i + 1)
        gu = gus.pop(i)
        rows = slice(i * rb, (i + 1) * rb)
        comb = comb_ref[0, rows, :]
        lane = lax.broadcasted_iota(jnp.int32, comb.shape, 1)
        cws = [jnp.sum(jnp.where(lane == MOE_EPS * s + j, comb, 0.0), axis=1, keepdims=True) for j in range(MOE_EPS)]
        cw = jnp.concatenate([jnp.broadcast_to(c, (rb, DE)) for c in cws], axis=1)
        gate = gu[:, 0:MOE_EPS * DE]
        act = gate * _sigmoid(gate) * gu[:, MOE_EPS * DE:2 * MOE_EPS * DE] * cw
        acc[rows, :] += _dot(act, wd)

    @pl.when(s == N_EXPERTS // MOE_EPS - 1)
    def _():
        x2 = x1_ref[0] + gt_ref[0] * acc[...]
        o_ref[0] = _rms(x2, fg_ref[...]) if final else x2


def _moe(h2, comb, x1, gt, lw, final_g, tm, final):
    B, T, D = x1.shape
    gt_rows = gt.shape[1]
    gt_spec = (pl.BlockSpec((1, 1, D), lambda b, i, e: (b, 0, 0)) if gt_rows == 1
               else pl.BlockSpec((1, tm, D), lambda b, i, e: (b, i, 0)))
    row = lambda w: pl.BlockSpec((1, tm, w), lambda b, i, e: (b, i, 0))
    return pl.pallas_call(
        functools.partial(_moe_kernel, final=final, rb=min(tm, 256)),
        grid=(B, T // tm, N_EXPERTS // MOE_EPS),
        in_specs=[row(D), row(ROUTER_LANES), row(D), gt_spec,
                  pl.BlockSpec((1, D, 2 * MOE_EPS * D_EXPERT), lambda b, i, e: (e, 0, 0)),
                  pl.BlockSpec((1, MOE_EPS * D_EXPERT, D), lambda b, i, e: (e, 0, 0)),
                  pl.BlockSpec((1, D), lambda b, i, e: (0, 0))],
        out_specs=row(D),
        out_shape=jax.ShapeDtypeStruct((B, T, D), F32),
        scratch_shapes=[pltpu.VMEM((tm, D), F32)],
        compiler_params=_cp(("parallel", "parallel", "arbitrary")),
        name="moe",
    )(h2, comb, x1, gt, lw['wgu'], lw['wd'], final_g)


def _layer_weights(l, w):
    D = D_MODEL
    w_in = w['w_in'][l]
    c_q, c_kv, c_kr = Q_LORA, Q_LORA + KV_LORA, MLA_IN
    c_rw, c_cv = MLA_IN + RWKV_IN, MLA_IN + RWKV_IN + 2 * CONV_DIM
    half = ROPE_DIM // 2
    swap = np.concatenate([np.arange(half, ROPE_DIM), np.arange(half)])
    wkr = w_in[:, c_kv:c_kr]
    q_up = w['q_up'][l].reshape(Q_LORA, MLA_HEADS, NOPE_DIM + ROPE_DIM)
    q_rope = q_up[:, :, NOPE_DIM:]
    wukq = jnp.zeros((KV_LORA, PAIR_KEYS), F32)
    for hd in range(MLA_HEADS):
        wukq = wukq.at[:, 128 * hd:128 * hd + NOPE_DIM].set(w['w_uk'][l][:, hd, :])
    zl = jnp.zeros((W_LORA, RWKV_DIM), F32)
    head = np.arange(RWKV_DIM) // RWKV_HEAD
    wr = jnp.zeros((D, ROUTER_LANES), F32)
    wr = wr.at[:, 0:N_EXPERTS].set(w['router_expert_w'][l]).at[:, N_EXPERTS:N_EXPERTS + N_GROUPS].set(
        w['router_group_w'][l])
    br = jnp.zeros((1, ROUTER_LANES), F32)
    br = br.at[0, 0:N_EXPERTS].set(w['router_expert_b'][l]).at[0, N_EXPERTS:N_EXPERTS + N_GROUPS].set(
        w['router_group_b'][l])
    r2 = lambda a: a.reshape(1, -1)
    pair_cols = lambda a: jnp.concatenate([a[j::MOE_EPS] for j in range(MOE_EPS)], axis=2)
    return dict(
        g1=r2(w['norm1_g'][l]), g2n=r2(w['norm2_g'][l]),
        wq=w_in[:, :c_q].astype(BF16), wkv=w_in[:, c_q:c_kv].astype(BF16),
        wkr=jnp.concatenate([jnp.tile(wkr, (1, 8)), jnp.tile(wkr[:, swap], (1, 8))], axis=1).astype(BF16),
        wrw=w_in[:, c_kr:c_rw].astype(BF16), wcv=w_in[:, c_rw:c_cv].astype(BF16), wg=w_in[:, c_cv:].astype(BF16),
        wkvt=w_in[:, c_q:c_kv].T.astype(BF16), kvgc=w['kv_norm_g'][l].reshape(-1, 1),
        qg=r2(w['q_norm_g'][l]), wqnt=q_up[:, :, :NOPE_DIM].reshape(Q_LORA, -1).T.astype(BF16),
        wukq=wukq.astype(BF16), kvg=r2(w['kv_norm_g'][l]),
        wqrt=q_rope.reshape(Q_LORA, -1).T.astype(BF16), wqrst=q_rope[:, :, swap].reshape(Q_LORA, -1).T.astype(BF16),
        wuvt=jnp.transpose(w['w_uv'][l], (1, 2, 0)).astype(BF16), proj_at=w['proj_a'][l].T.astype(BF16),
        mu=r2(w['rwkv_mu'][l]), w0=r2(w['rwkv_w0'][l]), a0=r2(w['rwkv_a0'][l]),
        w2p=jnp.concatenate([w['rwkv_w2'][l], zl], axis=0).astype(BF16),
        a2p=jnp.concatenate([zl, w['rwkv_a2'][l]], axis=0).astype(BF16),
        g2=w['rwkv_g2'][l].astype(BF16), k_k=r2(w['rwkv_k_k'][l]), k_a=r2(w['rwkv_k_a'][l]),
        r_k=r2(w['rwkv_r_k'][l]), ln_g=r2(w['rwkv_ln_g'][l]), ln_b=r2(w['rwkv_ln_b'][l]),
        ones_bd=jnp.asarray((head[:, None] == head[None, :]).astype(np.float32)).astype(BF16),
        proj_b=w['proj_b'][l].astype(BF16),
        conv_dw=w['conv_dw'][l], conv_dw_b=r2(w['conv_dw_b'][l]), conv_ln_g=r2(w['conv_ln_g'][l]),
        conv_ln_b=r2(w['conv_ln_b'][l]), proj_c=w['proj_c'][l].astype(BF16),
        w_out=w['w_out'][l].astype(BF16), wr=wr, br=br,
        wgu=jnp.concatenate([pair_cols(w['moe_w_gate'][l]), pair_cols(w['moe_w_up'][l])], axis=2).astype(BF16),
        wd=w['moe_w_down'][l].reshape(N_EXPERTS // MOE_EPS, MOE_EPS * D_EXPERT, D).astype(BF16),
    )


def _rope_tables(past, T):
    half = ROPE_DIM // 2
    inv = ROPE_THETA ** (-jnp.arange(half, dtype=F32) / half)
    ang = (past + jnp.arange(T)).astype(F32)[:, None] * inv[None, :]
    cos, sin = jnp.cos(ang), jnp.sin(ang)
    cos8 = jnp.tile(jnp.concatenate([cos, cos], axis=1), (1, 8))
    sin8 = jnp.tile(jnp.concatenate([-sin, sin], axis=1), (1, 8))
    return cos8, sin8, cos8.T, sin8.T


def _row_tile(T, cap):
    return min(T, cap)


def _trunk_layer(x, mod, cache_lat, cache_rope, shift_st, wkv_st, conv_st, lw, final_g, final):
    B, T, D = x.shape
    past = 0 if cache_lat is None else cache_lat.shape[1]
    tm = _row_tile(T, 512)
    lat, kr, keys, latt, qt, prw, glu = _in_proj(x, mod, lw, _rope_tables(past, T), tm)

    _, t_pad, _, s_pad, n_keys = _attn_tiles(T, past)
    row_pad = ((0, 0), (0, s_pad - n_keys), (0, 0))
    if past:
        lat_all = jnp.pad(jnp.concatenate([cache_lat, lat], axis=1), row_pad)
        rope_all = jnp.pad(jnp.concatenate([cache_rope, kr], axis=1), row_pad)
        key_tile = max(d for d in range(8, 2305, 8) if s_pad % d == 0)
        keys = _key_up(lat_all, jnp.tile(rope_all, (1, 1, 8)), lw['wukq'], key_tile)
        old_t = jnp.concatenate([jnp.swapaxes(cache_lat, 1, 2).astype(BF16),
                                 jnp.ones((B, VT_ROWS - KV_LORA, past), BF16)], axis=1)
        latt = jnp.concatenate([old_t, latt], axis=2)
    elif s_pad != n_keys:
        keys = jnp.pad(keys, row_pad)
    if s_pad != n_keys:
        latt = jnp.pad(latt, ((0, 0), (0, 0), (0, s_pad - n_keys)))
    if t_pad != T:
        qt = jnp.pad(qt, ((0, 0), (0, 0), (0, 0), (0, t_pad - T)))
    o_a = _attention(qt, keys, latt, lw['wuvt'], lw['proj_at'], past, T)[:, :T]

    L = WKV_CHUNK
    Tp = -(-T // L) * L
    prw_p = prw if Tp == T else jnp.pad(prw, ((0, 0), (0, Tp - T), (0, 0)))
    shift0 = jnp.zeros((B, 1, RWKV_IN), F32) if shift_st is None else shift_st[:, None, :]
    W = RWKV_DIM
    s0 = jnp.zeros((B, W, W), F32)
    if wkv_st is not None:
        for hd in range(RWKV_HEADS):
            o = hd * RWKV_HEAD
            s0 = s0.at[:, o:o + RWKV_HEAD, o:o + RWKV_HEAD].set(wkv_st[:, hd])
    yb, s_bd = _wkv(prw_p, shift0, s0, lw, L if Tp == T else T)
    yb = yb[:, :T]
    wkv_new = jnp.stack([s_bd[:, hd * RWKV_HEAD:(hd + 1) * RWKV_HEAD, hd * RWKV_HEAD:(hd + 1) * RWKV_HEAD]
                         for hd in range(RWKV_HEADS)], axis=1)

    pad_rows = CONV_HALO - (CONV_WIDTH - 1)
    st = jnp.zeros((B, CONV_WIDTH - 1, CONV_DIM), F32) if conv_st is None else conv_st
    x1, h2, comb, conv_tail = _merge(x, mod, o_a, yb, glu, jnp.pad(st, ((0, 0), (pad_rows, 0), (0, 0))), lw, tm)
    conv_new = conv_tail[:, pad_rows:]

    if T >= 1024:
        x2 = _moe(h2, comb, x1, mod[:, 5:6], lw, final_g, 1024, final)
    else:
        gt = jnp.broadcast_to(mod[:, 5:6], (B, T, D)).reshape(1, B * T, D)
        x2 = _moe(h2.reshape(1, B * T, D), comb.reshape(1, B * T, ROUTER_LANES), x1.reshape(1, B * T, D), gt, lw,
                  final_g, B * T, final).reshape(B, T, D)
    return x2, (lat, kr, prw[:, -1], wkv_new, conv_new)


def kernel(x_prompt, x_sample, c_prompt, c_sample, cache_kv_latent, cache_k_rope, state_rwkv_shift, state_rwkv_wkv, state_conv, ada_w, ada_b, norm1_g, norm2_g, w_in, q_norm_g, q_up, kv_norm_g, w_uk, w_uv, proj_a, rwkv_mu, rwkv_w0, rwkv_w2, rwkv_a0, rwkv_a2, rwkv_g2, rwkv_k_k, rwkv_k_a, rwkv_r_k, rwkv_ln_g, rwkv_ln_b, proj_b, conv_dw, conv_dw_b, conv_ln_g, conv_ln_b, proj_c, w_out, router_group_w, router_group_b, router_expert_w, router_expert_b, moe_w_gate, moe_w_up, moe_w_down, final_g):
    w = dict(norm1_g=norm1_g, norm2_g=norm2_g, w_in=w_in, q_norm_g=q_norm_g, q_up=q_up, kv_norm_g=kv_norm_g,
             w_uk=w_uk, w_uv=w_uv, proj_a=proj_a, rwkv_mu=rwkv_mu, rwkv_w0=rwkv_w0, rwkv_w2=rwkv_w2,
             rwkv_a0=rwkv_a0, rwkv_a2=rwkv_a2, rwkv_g2=rwkv_g2, rwkv_k_k=rwkv_k_k, rwkv_k_a=rwkv_k_a,
             rwkv_r_k=rwkv_r_k.reshape(rwkv_r_k.shape[0], -1), rwkv_ln_g=rwkv_ln_g, rwkv_ln_b=rwkv_ln_b,
             proj_b=proj_b, conv_dw=conv_dw, conv_dw_b=conv_dw_b, conv_ln_g=conv_ln_g, conv_ln_b=conv_ln_b,
             proj_c=proj_c, w_out=w_out, router_group_w=router_group_w, router_group_b=router_group_b,
             router_expert_w=router_expert_w, router_expert_b=router_expert_b, moe_w_gate=moe_w_gate,
             moe_w_up=moe_w_up, moe_w_down=moe_w_down)
    depth = ada_w.shape[0]
    bp = x_prompt.shape[0]
    D = D_MODEL
    mod_all = _ada_mod(jnp.concatenate([c_prompt, c_sample], axis=0), ada_w, ada_b)
    fg = final_g.reshape(1, D)
    hp, hs = x_prompt, x_sample
    new_p, new_s = [], []
    for l in range(depth):
        lw = _layer_weights(l, w)
        mod_p = mod_all[l, :bp].reshape(bp, N_MOD, D)
        mod_s = mod_all[l, bp:].reshape(-1, N_MOD, D)
        final = l == depth - 1
        hp, st_p = _trunk_layer(hp, mod_p, None, None, None, None, None, lw, fg, final)
        hs, st_s = _trunk_layer(hs, mod_s, cache_kv_latent[l], cache_k_rope[l], state_rwkv_shift[l],
                                state_rwkv_wkv[l], state_conv[l], lw, fg, final)
        new_p.append(st_p)
        new_s.append(st_s)
    stack = lambda states, i: jnp.stack([s[i] for s in states], axis=0)
    return ((hp, hs) + tuple(stack(new_p, i) for i in range(5)) + tuple(stack(new_s, i) for i in range(5)))
```

```python
import functools

import numpy as np
import jax
import jax.numpy as jnp
from jax import lax
from jax.experimental import pallas as pl
from jax.experimental.pallas import tpu as pltpu

F32 = jnp.float32
BF16 = jnp.bfloat16

D_MODEL = 1024
CHUNK = 64
NORM_EPS = 1e-6
MLA_HEADS = 8
Q_LORA = 384
KV_LORA = 256
NOPE_DIM = 64
ROPE_DIM = 32
V_HEAD = 64
ROPE_THETA = 10000.0
SM_SCALE = (NOPE_DIM + ROPE_DIM) ** -0.5
LOG2_E = 1.4426950408889634
RWKV_HEADS = 4
RWKV_HEAD = 64
RWKV_DIM = RWKV_HEADS * RWKV_HEAD
W_LORA = 64
A_LORA = 64
G_LORA = 128
RWKV_IN = 3 * RWKV_DIM + W_LORA + A_LORA + G_LORA
GN_EPS = 64e-5
CONV_DIM = 256
CONV_WIDTH = 31
LN_EPS = 1e-5
MLA_IN = Q_LORA + KV_LORA + ROPE_DIM
N_MOD = 6
N_GROUPS = 4
EXPERTS_PER_GROUP = 4
N_EXPERTS = 16
D_EXPERT = 256
WKV_CHUNK = 64
CONV_HALO = 32
ROUTER_LANES = 128
PAIR_KEYS = 4 * 256
VT_ROWS = KV_LORA + 16
MOE_EPS = 4
QK_AHEAD = 8
VMEM_LIMIT = 56 * 1024 * 1024


def _cp(sem):
    return pltpu.CompilerParams(dimension_semantics=sem, vmem_limit_bytes=VMEM_LIMIT)


def _full(a):
    nd = a.ndim
    return pl.BlockSpec(a.shape, lambda *_: (0,) * nd)


def _dot(a, b, dims=(((1,), (0,)), ((), ()))):
    return lax.dot_general(a.astype(BF16), b.astype(BF16), dims, preferred_element_type=F32)


def _split(a):
    hi = a.astype(BF16)
    lo = (a - hi.astype(F32)).astype(BF16)
    return hi, lo


def _dot_hl(a, b, dims=(((1,), (0,)), ((), ()))):
    hi, lo = _split(a)
    bb = b.astype(BF16)
    return (lax.dot_general(hi, bb, dims, preferred_element_type=F32)
            + lax.dot_general(lo, bb, dims, preferred_element_type=F32))


def _dot_split(a, b):
    m, n = a.shape[0], b.shape[1]
    ah, al = _split(a)
    bh, bl = _split(b)
    p = jnp.dot(jnp.concatenate([ah, al], axis=0), jnp.concatenate([bh, bl], axis=1), preferred_element_type=F32)
    return (p[0:m, 0:n] + p[0:m, n:2 * n]) + (p[m:2 * m, 0:n] + p[m:2 * m, n:2 * n])


def _sigmoid(x):
    return 0.5 * jnp.tanh(0.5 * x) + 0.5


def _rms(x, g):
    return x * lax.rsqrt(jnp.mean(x * x, axis=-1, keepdims=True) + NORM_EPS) * g


def _ada_kernel(c_ref, w_ref, b_ref, o_ref):
    c = c_ref[...]
    o_ref[0] = _dot(c * _sigmoid(c), w_ref[0]) + b_ref[0]


def _ada_mod(c_all, ada_w, ada_b):
    L, D, C = ada_w.shape
    R = c_all.shape[0]
    nb = C // D
    return pl.pallas_call(
        _ada_kernel,
        grid=(L, nb),
        in_specs=[pl.BlockSpec((R, D), lambda l, j: (0, 0)),
                  pl.BlockSpec((1, D, D), lambda l, j: (l, 0, j)),
                  pl.BlockSpec((1, 1, D), lambda l, j: (l, 0, j))],
        out_specs=pl.BlockSpec((1, R, D), lambda l, j: (l, 0, j)),
        out_shape=jax.ShapeDtypeStruct((L, R, C), F32),
        compiler_params=_cp(("parallel", "parallel")),
        name="ada_mod",
    )(c_all, ada_w, ada_b.reshape(L, 1, C))


_NT = (((1,), (1,)), ((), ()))


def _pair_keys(lat, rope8, wukq):
    lane = lax.broadcasted_iota(jnp.int32, rope8.shape, 1) & 127
    rope_slot = jnp.where((lane >= NOPE_DIM) & (lane < NOPE_DIM + ROPE_DIM), rope8, 0.0)
    return (_dot(lat, wukq) + jnp.concatenate([rope_slot] * (MLA_HEADS // 2), axis=1)).astype(BF16)


def _in_proj_kernel(x_ref, mod_ref, g1_ref, wq_ref, wkv_ref, wkvt_ref, wkr_ref, wrw_ref, wcv_ref, qg_ref, wqnt_ref,
                    wqrt_ref, wqrst_ref, wukq_ref, kvg_ref, kvgc_ref, cos_ref, sin_ref, cost_ref, sint_ref,
                    lat_ref, kr_ref, kq_ref, latt_ref, qt_ref, prw_ref, glu_ref):
    x = x_ref[0]
    mod = mod_ref[0]
    tm = x.shape[0]
    h = _rms(x, g1_ref[...]) * (1.0 + mod[1:2]) + mod[0:1]
    hb = h.astype(BF16)
    qn = _rms(_dot(hb, wq_ref[...]), qg_ref[...]).astype(BF16)
    qnt = _dot(wqnt_ref[...], qn, _NT)
    qrt = _dot(wqrt_ref[...], qn, _NT)
    qrst = _dot(wqrst_ref[...], qn, _NT)
    qrope = qrt * cost_ref[...] + qrst * sint_ref[...]
    zpad = jnp.zeros((128 - NOPE_DIM - ROPE_DIM, tm), F32)
    for hd in range(MLA_HEADS):
        qh = jnp.concatenate([qnt[NOPE_DIM * hd:NOPE_DIM * (hd + 1)], qrope[ROPE_DIM * hd:ROPE_DIM * (hd + 1)],
                              zpad], axis=0)
        qt_ref[0, hd] = (qh * (SM_SCALE * LOG2_E)).astype(BF16)
    lat = _rms(_dot(hb, wkv_ref[...]), kvg_ref[...])
    lat_ref[0] = lat
    if tm % 128 == 0:
        latt = lat.T
    else:
        pkvt = _dot(wkvt_ref[...], hb, _NT)
        latt = pkvt * lax.rsqrt(jnp.mean(pkvt * pkvt, axis=0, keepdims=True) + NORM_EPS) * kvgc_ref[...]
    latt_ref[0, 0:KV_LORA, :] = latt.astype(BF16)
    latt_ref[0, KV_LORA:VT_ROWS, :] = jnp.ones((VT_ROWS - KV_LORA, tm), BF16)
    pkr = _dot(hb, wkr_ref[...])
    krt = pkr[:, 0:256] * cos_ref[...] + pkr[:, 256:512] * sin_ref[...]
    kr_ref[0] = krt[:, 0:ROPE_DIM]
    kq_ref[0] = _pair_keys(lat, krt, wukq_ref[...])
    prw_ref[0] = _dot(hb, wrw_ref[...])
    pcv = _dot(hb, wcv_ref[...])
    glu_ref[0] = pcv[:, 0:CONV_DIM] * _sigmoid(pcv[:, CONV_DIM:2 * CONV_DIM])


def _in_proj(x, mod, lw, tabs, tm):
    B, T, D = x.shape
    nT = T // tm
    cos8, sin8, cos8t, sin8t = tabs
    row = lambda w: pl.BlockSpec((1, tm, w), lambda b, i: (b, i, 0))
    col = pl.BlockSpec((1, VT_ROWS, tm), lambda b, i: (b, 0, i))
    args = (lw['g1'], lw['wq'], lw['wkv'], lw['wkvt'], lw['wkr'], lw['wrw'], lw['wcv'], lw['qg'], lw['wqnt'],
            lw['wqrt'], lw['wqrst'], lw['wukq'], lw['kvg'], lw['kvgc'])
    return pl.pallas_call(
        _in_proj_kernel,
        grid=(B, nT),
        in_specs=[row(D), pl.BlockSpec((1, N_MOD, D), lambda b, i: (b, 0, 0))] + [_full(a) for a in args]
                 + [pl.BlockSpec((tm, 256), lambda b, i: (i, 0))] * 2
                 + [pl.BlockSpec((256, tm), lambda b, i: (0, i))] * 2,
        out_specs=[row(KV_LORA), row(ROPE_DIM), row(PAIR_KEYS), col,
                   pl.BlockSpec((1, MLA_HEADS, 128, tm), lambda b, i: (b, 0, 0, i)),
                   row(RWKV_IN), row(CONV_DIM)],
        out_shape=[jax.ShapeDtypeStruct((B, T, KV_LORA), F32),
                   jax.ShapeDtypeStruct((B, T, ROPE_DIM), F32),
                   jax.ShapeDtypeStruct((B, T, PAIR_KEYS), BF16),
                   jax.ShapeDtypeStruct((B, VT_ROWS, T), BF16),
                   jax.ShapeDtypeStruct((B, MLA_HEADS, 128, T), BF16),
                   jax.ShapeDtypeStruct((B, T, RWKV_IN), F32),
                   jax.ShapeDtypeStruct((B, T, CONV_DIM), F32)],
        compiler_params=_cp(("parallel", "parallel")),
        name="in_proj",
    )(x, mod, *args, cos8, sin8, cos8t, sin8t)


def _key_up_kernel(lat_ref, rope8_ref, wukq_ref, kq_ref):
    kq_ref[0] = _pair_keys(lat_ref[0], rope8_ref[0], wukq_ref[...])


def _key_up(lat, rope8, wukq, tm):
    B, S, _ = lat.shape
    row = lambda w: pl.BlockSpec((1, tm, w), lambda b, i: (b, i, 0))
    return pl.pallas_call(
        _key_up_kernel,
        grid=(B, S // tm),
        in_specs=[row(KV_LORA), row(256), _full(wukq)],
        out_specs=row(PAIR_KEYS),
        out_shape=jax.ShapeDtypeStruct((B, S, PAIR_KEYS), BF16),
        compiler_params=_cp(("parallel", "parallel")),
        name="key_up",
    )(lat, rope8, wukq)


def _attn_kernel(qi_ref, ki_ref, last_ref, qt_ref, k_ref, vt_ref, wuvt_ref, pa_ref, o_ref,
                 qcat, m_s, acc, *, tq, tk, past, n_keys):
    p = pl.program_id(1)
    qi = qi_ref[p]
    ki = ki_ref[p]
    H = MLA_HEADS
    NP = H // 2
    nh = tq // 128
    NC = NP * nh
    diag_aligned = tq == tk and past % tk == 0

    @pl.when(ki == 0)
    def _():
        qcat[...] = jnp.zeros(qcat.shape, BF16)
        for pr in range(NP):
            for hf in range(nh):
                pos = slice(128 * hf, 128 * (hf + 1))
                qcat[pr * nh + hf, 0:128, 0:128] = qt_ref[0, 2 * pr, :, pos]
                qcat[pr * nh + hf, 128:256, 128:256] = qt_ref[0, 2 * pr + 1, :, pos]
        m_s[...] = jnp.full(m_s.shape, -jnp.inf, F32)
        acc[...] = jnp.zeros(acc.shape, F32)

    def step(bias):
        def nkeys(ch):
            return 128 * (ch % nh + 1) if (bias is not None and diag_aligned) else tk

        scores = {}

        def qk(ch):
            pr = ch // nh
            scores[ch] = jnp.dot(k_ref[0, 0:nkeys(ch), 256 * pr:256 * (pr + 1)], qcat[ch],
                                 preferred_element_type=F32)

        for ch in range(min(QK_AHEAD, NC)):
            qk(ch)
        for ch in range(NC):
            if ch + QK_AHEAD < NC:
                qk(ch + QK_AHEAD)
            s = scores.pop(ch)
            if bias is not None:
                s = s + bias[ch % nh][0:nkeys(ch), :]
            m_prev = m_s[ch]
            m_new = jnp.maximum(m_prev, jnp.max(s, axis=0, keepdims=True))
            alpha = jnp.exp2(m_prev - m_new)
            pe = jnp.exp2(s - m_new)
            acc[ch] = alpha * acc[ch] + jnp.dot(vt_ref[0, :, 0:nkeys(ch)], pe.astype(BF16),
                                                preferred_element_type=F32)
            m_s[ch] = m_new

    is_last = last_ref[p] == 1

    @pl.when(jnp.logical_not(is_last))
    def _():
        step(None)

    @pl.when(is_last)
    def _():
        kpos = ki * tk + lax.broadcasted_iota(jnp.int32, (tk, 1), 0)
        biases = []
        for hf in range(nh):
            t = 128 * hf + lax.broadcasted_iota(jnp.int32, (1, 128), 1)
            limit = jnp.minimum((((past + qi * tq + t) >> 6) + 1) << 6, n_keys)
            bias = jnp.where(kpos < limit, 0.0, -jnp.inf)
            biases.append(jnp.concatenate([bias, bias], axis=1))
        step(biases)
        heads = []
        for pr in range(NP):
            o = [acc[pr * nh + hf, 0:KV_LORA, :] / acc[pr * nh + hf, KV_LORA:KV_LORA + 1, :]
                 for hf in range(nh)]
            for j in range(2):
                oh = jnp.concatenate([x[:, 128 * j:128 * (j + 1)] for x in o], axis=1)
                heads.append(_dot(wuvt_ref[2 * pr + j], oh))
        o_ref[0] = _dot(jnp.concatenate(heads, axis=0), pa_ref[...], (((0,), (0,)), ((), ())))


def _attn_tiles(T, past):
    n_keys = past + T
    tq = 512 if T % 512 == 0 else 128
    t_pad = -(-T // tq) * tq
    tk = -(-n_keys // 128) * 128 if n_keys <= 2304 else 512
    s_pad = -(-n_keys // tk) * tk
    return tq, t_pad, tk, s_pad, n_keys


def _attention(qt, keys, latt, wuvt, proj_a, past, T):
    B, H = qt.shape[:2]
    tq, t_pad, tk, s_pad, n_keys = _attn_tiles(T, past)
    assert keys.shape[1] == s_pad and latt.shape[2] == s_pad and qt.shape[3] == t_pad and past % CHUNK == 0
    qi, ki, last = [], [], []
    for i in range(t_pad // tq):
        lim = min(((past + (i + 1) * tq - 1) // CHUNK + 1) * CHUNK, n_keys)
        nk = -(-lim // tk)
        for j in range(nk):
            qi.append(i), ki.append(j), last.append(int(j == nk - 1))
    npairs = len(qi)
    tabs = [jnp.asarray(np.asarray(a, np.int32)) for a in (qi, ki, last)]
    nc = (H // 2) * (tq // 128)
    grid_spec = pltpu.PrefetchScalarGridSpec(
        num_scalar_prefetch=3,
        grid=(B, npairs),
        in_specs=[pl.BlockSpec((1, H, 128, tq), lambda b, p, qi, ki, la: (b, 0, 0, qi[p])),
                  pl.BlockSpec((1, tk, PAIR_KEYS), lambda b, p, qi, ki, la: (b, ki[p], 0)),
                  pl.BlockSpec((1, VT_ROWS, tk), lambda b, p, qi, ki, la: (b, 0, ki[p])),
                  pl.BlockSpec(wuvt.shape, lambda b, p, qi, ki, la: (0, 0, 0)),
                  pl.BlockSpec(proj_a.shape, lambda b, p, qi, ki, la: (0, 0))],
        out_specs=pl.BlockSpec((1, tq, D_MODEL), lambda b, p, qi, ki, la: (b, qi[p], 0)),
        scratch_shapes=[pltpu.VMEM((nc, 256, 256), BF16), pltpu.VMEM((nc, 1, 256), F32),
                        pltpu.VMEM((nc, VT_ROWS, 256), F32)])
    return pl.pallas_call(
        functools.partial(_attn_kernel, tq=tq, tk=tk, past=past, n_keys=n_keys),
        grid_spec=grid_spec,
        out_shape=jax.ShapeDtypeStruct((B, t_pad, D_MODEL), F32),
        compiler_params=_cp(("parallel", "arbitrary")),
        name="attention",
    )(*tabs, qt, keys, latt, wuvt, proj_a)


def _head_sum(x, ones_bd):
    return _dot_hl(x, ones_bd)


def _rwkv_features(pr, prev, wts):
    mu, w0, w2p, a0, a2p, g2, k_k, k_a, r_k, ones_bd = wts
    xs = pr + (prev - pr) * mu
    r = xs[:, 0:256]
    k = xs[:, 256:512]
    v = xs[:, 512:768]
    wa = xs[:, 768:896]
    gl = xs[:, 896:1024]
    nz = -(w0 + _dot(jnp.tanh(wa), w2p))
    softplus = jnp.maximum(nz, 0.0) + jnp.log(1.0 + jnp.exp(-jnp.abs(nz)))
    lw = -jnp.exp(-softplus - 0.5)
    a = _sigmoid(a0 + _dot(wa, a2p))
    g = _dot(_sigmoid(gl), g2)
    kk = k * k_k
    kk = kk * lax.rsqrt(_head_sum(kk * kk, ones_bd) + 1e-12)
    kf = k * (1.0 + (a - 1.0) * k_a)
    bonus = _head_sum(r * kf * r_k, ones_bd) * v
    return r, kf, kk * a, kk, v, lw, g, bonus


def _wkv_kernel(prw_ref, shift_ref, s0_ref, mu_ref, w0_ref, w2_ref, a0_ref, a2_ref, g2_ref, kk_ref, ka_ref, rk_ref,
                ones_ref, lng_ref, lnb_ref, yb_ref, sout_ref, S, carry, *, nb, G, n_valid):
    L = WKV_CHUNK
    W = RWKV_DIM

    @pl.when(pl.program_id(0) == 0)
    def _():
        S[...] = s0_ref[...]
        carry[...] = shift_ref[...]

    wts = tuple(ref[...] for ref in (mu_ref, w0_ref, w2_ref, a0_ref, a2_ref, g2_ref, kk_ref, ka_ref, rk_ref, ones_ref))
    ones_bd = wts[-1]
    ln_g = lng_ref[...]
    ln_b = lnb_ref[...]
    ri = lax.broadcasted_iota(jnp.int32, (W, W), 0)
    ci = lax.broadcasted_iota(jnp.int32, (W, W), 1)
    same = (ri >> 6) == (ci >> 6)
    strict = same & ((ci & 63) < (ri & 63))
    incl = same & ((ci & 63) <= (ri & 63))
    eye = (ri == ci).astype(F32)
    tri = (lax.broadcasted_iota(jnp.int32, (L, L), 1) <= lax.broadcasted_iota(jnp.int32, (L, L), 0)).astype(BF16)
    lane_head = lax.broadcasted_iota(jnp.int32, (L, W), 1) >> 6
    row = lax.broadcasted_iota(jnp.int32, (L, 1), 0)
    nt = (((1,), (1,)), ((), ()))
    tn = (((0,), (0,)), ((), ()))

    def stack(x):
        return jnp.concatenate([jnp.where(lane_head == hd, x, 0.0) for hd in range(RWKV_HEADS)], axis=0)

    def collapse(z):
        return z[0:L] + z[L:2 * L] + z[2 * L:3 * L] + z[3 * L:4 * L]

    def features(i):
        pr = prw_ref[i]
        prev = jnp.where(row == 0, carry[i], pltpu.roll(pr, 1, 0))
        carry[i] = pr[L - 1:L, :]
        f = _rwkv_features(pr, prev, wts)
        if n_valid < L:
            f = tuple(jnp.where(row < n_valid, x, 0.0) for x in f[:6]) + f[6:]
        return f

    def body(it, _):
        ids = [it * G + g for g in range(G)]
        each = lambda f, *xs: [f(*a) for a in zip(*xs)]
        r, kf, bb, kk, v, ll, gate, bonus = zip(*[features(i) for i in ids])
        s_prev = [S[i] for i in ids]
        split = each(_split, ll)
        c = each(lambda hl: jnp.dot(tri, hl[0], preferred_element_type=F32)
                 + jnp.dot(tri, hl[1], preferred_element_type=F32), split)
        c_last = each(lambda x: x[L - 1:L, :], c)
        e_neg = each(lambda x: jnp.exp(-x), c)
        e_end = each(lambda x, y: jnp.exp(y - x), c, c_last)
        rt = each(lambda x, y: stack(x * jnp.exp(y)), r, c)
        kkt = each(lambda x, y, z: stack(x * jnp.exp(y - z)), kk, c, ll)
        bt = each(lambda x, y: stack(x * y), bb, e_neg)
        kt = each(lambda x, y: stack(x * y), kf, e_neg)
        a_m = each(lambda x, y: jnp.where(strict, _dot(x, y, nt), 0.0), kkt, bt)
        b_m = each(lambda x, y: jnp.where(strict, _dot(x, y, nt), 0.0), kkt, kt)
        m1 = each(lambda x, y: jnp.where(incl, _dot(x, y, nt), 0.0), rt, bt)
        m2 = each(lambda x, y: jnp.where(incl, _dot(x, y, nt), 0.0), rt, kt)
        t_m = each(lambda x: eye - x, a_m)
        a_p = a_m
        for _ in range(5):
            a_p = each(lambda x: _dot(x, x), a_p)
            t_m = each(lambda x, y: x + _dot(x, y), t_m, a_p)
        vs = each(stack, v)
        w1 = each(lambda x, y: collapse(_dot(x, y)), t_m, kkt)
        bv = each(_dot, b_m, vs)
        w2 = each(lambda x, y: collapse(_dot(x, y)), t_m, bv)
        u = each(lambda x, y, z: -(_dot(x, y, nt) + z), w1, s_prev, w2)
        y0 = each(lambda x, y: _dot(collapse(x), y, nt), rt, s_prev)
        y1 = each(lambda a, b_, c_, d: collapse(_dot(a, stack(b_)) + _dot(c_, d)), m1, u, m2, vs)
        upd = each(lambda a, b_, c_, d, e: _dot(a, b_ * e, tn) + _dot(c_, d * e, tn), u, bb, v, kf, e_end)
        for g, i in enumerate(ids):
            S[i] = s_prev[g] * jnp.exp(c_last[g]) + jnp.where(same, upd[g], 0.0)
            y = y0[g] + y1[g]
            mean = _head_sum(y, ones_bd) * (1.0 / RWKV_HEAD)
            yc = y - mean
            var = _head_sum(yc * yc, ones_bd) * (1.0 / RWKV_HEAD)
            yb = (yc * lax.rsqrt(var + GN_EPS) * ln_g + ln_b + bonus[g]) * gate[g]
            yb_ref[i] = yb.astype(BF16)
        return 0

    lax.fori_loop(0, nb // G, body, 0)

    @pl.when(pl.program_id(0) == pl.num_programs(0) - 1)
    def _():
        sout_ref[...] = S[...]


def _wkv(prw, shift0, s0, lw, n_valid):
    B, T, _ = prw.shape
    L = WKV_CHUNK
    W = RWKV_DIM
    assert T % L == 0 and (n_valid == L or T == L)
    args = (lw['mu'], lw['w0'], lw['w2p'], lw['a0'], lw['a2p'], lw['g2'], lw['k_k'], lw['k_a'], lw['r_k'],
            lw['ones_bd'], lw['ln_g'], lw['ln_b'])
    st = pl.BlockSpec((B, W, W), lambda c: (0, 0, 0))
    return pl.pallas_call(
        functools.partial(_wkv_kernel, nb=B, G=8 if B % 8 == 0 else 1, n_valid=n_valid),
        grid=(T // L,),
        in_specs=[pl.BlockSpec((B, L, RWKV_IN), lambda c: (0, c, 0)), _full(shift0), st] + [_full(a) for a in args],
        out_specs=[pl.BlockSpec((B, L, W), lambda c: (0, c, 0)), st],
        out_shape=[jax.ShapeDtypeStruct((B, T, W), BF16), jax.ShapeDtypeStruct((B, W, W), F32)],
        scratch_shapes=[pltpu.VMEM((B, W, W), F32), pltpu.VMEM((B, 1, RWKV_IN), F32)],
        compiler_params=_cp(("arbitrary",)),
        name="wkv",
    )(prw, shift0, s0, *args)


def _conv_stage(ext, glu, st_ref, first, tm):
    @pl.when(first)
    def _():
        ext[0:CONV_HALO, :] = st_ref[0]
    ext[CONV_HALO:CONV_HALO + tm, :] = glu


def _conv_module(ext, dw_ref, dwb, lng, lnb, new_ref, tm):
    off = CONV_HALO - (CONV_WIDTH - 1)
    ext_v = ext[...]
    n = CONV_HALO + tm
    acc = jnp.zeros((tm, CONV_DIM), F32)
    for res in range(8):
        taps = [w for w in range(CONV_WIDTH) if (off + w) % 8 == res]
        if not taps:
            continue
        sh = ext_v if res == 0 else pltpu.roll(ext_v, n - res, 0)
        for w in taps:
            base = off + w - res
            acc = acc + sh[base:base + tm, :] * dw_ref[w:w + 1, :]
    y = acc + dwb
    mu = jnp.mean(y, axis=-1, keepdims=True)
    var = jnp.mean(jnp.square(y - mu), axis=-1, keepdims=True)
    y = (y - mu) * lax.rsqrt(var + LN_EPS) * lng + lnb
    tail = ext[tm:tm + CONV_HALO, :]
    new_ref[0] = tail
    ext[0:CONV_HALO, :] = tail
    return y * _sigmoid(y)


def _merge_kernel(x_ref, mod_ref, g1_ref, wg_ref, oa_ref, yb_ref, pb_ref, glu_ref, st_ref, dw_ref, dwb_ref, clg_ref,
                  clb_ref, pc_ref, wo_ref, g2_ref, wr_ref, br_ref, x1_ref, h2_ref, comb_ref, cnew_ref, ext):
    x = x_ref[0]
    mod = mod_ref[0]
    tm = x.shape[0]
    _conv_stage(ext, glu_ref[0], st_ref, pl.program_id(1) == 0, tm)
    yc = _conv_module(ext, dw_ref, dwb_ref[...], clg_ref[...], clb_ref[...], cnew_ref, tm)
    hb = (_rms(x, g1_ref[...]) * (1.0 + mod[1:2]) + mod[0:1]).astype(BF16)
    D = D_MODEL
    merged = _sigmoid(_dot(hb, wg_ref[:, 0:D])) * oa_ref[0]
    merged = merged + _sigmoid(_dot(hb, wg_ref[:, D:2 * D])) * _dot(yb_ref[0], pb_ref[...])
    merged = merged + _sigmoid(_dot(hb, wg_ref[:, 2 * D:3 * D])) * _dot(yc, pc_ref[...])
    x1 = x + mod[2:3] * _dot(merged, wo_ref[...])
    x1_ref[0] = x1
    h2 = _rms(x1, g2_ref[...]) * (1.0 + mod[4:5]) + mod[3:4]
    h2_ref[0] = h2.astype(BF16)
    logits = _dot_split(h2, wr_ref[...]) + br_ref[...]
    lane = lax.broadcasted_iota(jnp.int32, logits.shape, 1)
    neg = -jnp.inf
    is_g = (lane >= N_EXPERTS) & (lane < N_EXPERTS + N_GROUPS)
    gl = jnp.where(is_g, logits, neg)
    gmax = jnp.max(gl, axis=1, keepdims=True)
    gsum = jnp.sum(jnp.exp(gl - gmax), axis=1, keepdims=True)
    g_val = 1.0 / gsum
    g_idx = jnp.min(jnp.where(is_g & (gl == gmax), lane, 4 * ROUTER_LANES), axis=1, keepdims=True) - N_EXPERTS
    in_grp = (lane >= g_idx * EXPERTS_PER_GROUP) & (lane < (g_idx + 1) * EXPERTS_PER_GROUP)
    el = jnp.where(in_grp, logits, neg)
    v1 = jnp.max(el, axis=1, keepdims=True)
    i1 = jnp.min(jnp.where(in_grp & (el == v1), lane, 4 * ROUTER_LANES), axis=1, keepdims=True)
    el2 = jnp.where(lane == i1, neg, el)
    v2 = jnp.max(el2, axis=1, keepdims=True)
    i2 = jnp.min(jnp.where(in_grp & (lane != i1) & (el2 == v2), lane, 4 * ROUTER_LANES), axis=1, keepdims=True)
    e2 = jnp.exp(v2 - v1)
    w1 = g_val / (1.0 + e2)
    w2 = g_val * e2 / (1.0 + e2)
    comb_ref[0] = jnp.where(lane == i1, w1, 0.0) + jnp.where(lane == i2, w2, 0.0)


def _merge(x, mod, oa, yb, glu, st_pad, lw, tm):
    B, T, D = x.shape
    row = lambda w: pl.BlockSpec((1, tm, w), lambda b, i: (b, i, 0))
    modspec = pl.BlockSpec((1, N_MOD, D), lambda b, i: (b, 0, 0))
    halo = pl.BlockSpec((1, CONV_HALO, CONV_DIM), lambda b, i: (b, 0, 0))
    ins = [(x, row(D)), (mod, modspec), (lw['g1'], None), (lw['wg'], None), (oa, row(D)), (yb, row(256)),
           (lw['proj_b'], None), (glu, row(CONV_DIM)), (st_pad, halo), (lw['conv_dw'], None),
           (lw['conv_dw_b'], None), (lw['conv_ln_g'], None), (lw['conv_ln_b'], None), (lw['proj_c'], None),
           (lw['w_out'], None), (lw['g2n'], None), (lw['wr'], None), (lw['br'], None)]
    return pl.pallas_call(
        _merge_kernel,
        grid=(B, T // tm),
        in_specs=[s if s is not None else _full(a) for a, s in ins],
        out_specs=[row(D), row(D), row(ROUTER_LANES), halo],
        out_shape=[jax.ShapeDtypeStruct((B, T, D), F32), jax.ShapeDtypeStruct((B, T, D), BF16),
                   jax.ShapeDtypeStruct((B, T, ROUTER_LANES), F32),
                   jax.ShapeDtypeStruct((B, CONV_HALO, CONV_DIM), F32)],
        scratch_shapes=[pltpu.VMEM((CONV_HALO + tm, CONV_DIM), F32)],
        compiler_params=_cp(("parallel", "arbitrary")),
        name="merge",
    )(*[a for a, _ in ins])


def _moe_kernel(h2_ref, comb_ref, x1_ref, gt_ref, wgu_ref, wd_ref, fg_ref, o_ref, acc, *, final, rb):
    s = pl.program_id(2)
    tm = h2_ref.shape[1]
    nrb = tm // rb
    DE = D_EXPERT

    @pl.when(s == 0)
    def _():
        acc[...] = jnp.zeros_like(acc)

    wgu = wgu_ref[0]
    wd = wd_ref[0]
    gus = {}

    def up(i):
        gus[i] = _dot(h2_ref[0, i * rb:(i + 1) * rb, :], wgu)

    up(0)
    for i in range(nrb):
        if i + 1 < nrb:
            up(i + 1)
        gu = gus.pop(i)
        rows = slice(i * rb, (i + 1) * rb)
        comb = comb_ref[0, rows, :]
        lane = lax.broadcasted_iota(jnp.int32, comb.shape, 1)
        cws = [jnp.sum(jnp.where(lane == MOE_EPS * s + j, comb, 0.0), axis=1, keepdims=True) for j in range(MOE_EPS)]
        cw = jnp.concatenate([jnp.broadcast_to(c, (rb, DE)) for c in cws], axis=1)
        gate = gu[:, 0:MOE_EPS * DE]
        act = gate * _sigmoid(gate) * gu[:, MOE_EPS * DE:2 * MOE_EPS * DE] * cw
        acc[rows, :] += _dot(act, wd)

    @pl.when(s == N_EXPERTS // MOE_EPS - 1)
    def _():
        x2 = x1_ref[0] + gt_ref[0] * acc[...]
        o_ref[0] = _rms(x2, fg_ref[...]) if final else x2


def _moe(h2, comb, x1, gt, lw, final_g, tm, final):
    B, T, D = x1.shape
    gt_rows = gt.shape[1]
    gt_spec = (pl.BlockSpec((1, 1, D), lambda b, i, e: (b, 0, 0)) if gt_rows == 1
               else pl.BlockSpec((1, tm, D), lambda b, i, e: (b, i, 0)))
    row = lambda w: pl.BlockSpec((1, tm, w), lambda b, i, e: (b, i, 0))
    return pl.pallas_call(
        functools.partial(_moe_kernel, final=final, rb=min(tm, 256)),
        grid=(B, T // tm, N_EXPERTS // MOE_EPS),
        in_specs=[row(D), row(ROUTER_LANES), row(D), gt_spec,
                  pl.BlockSpec((1, D, 2 * MOE_EPS * D_EXPERT), lambda b, i, e: (e, 0, 0)),
                  pl.BlockSpec((1, MOE_EPS * D_EXPERT, D), lambda b, i, e: (e, 0, 0)),
                  pl.BlockSpec((1, D), lambda b, i, e: (0, 0))],
        out_specs=row(D),
        out_shape=jax.ShapeDtypeStruct((B, T, D), F32),
        scratch_shapes=[pltpu.VMEM((tm, D), F32)],
        compiler_params=_cp(("parallel", "parallel", "arbitrary")),
        name="moe",
    )(h2, comb, x1, gt, lw['wgu'], lw['wd'], final_g)


def _layer_weights(l, w):
    D = D_MODEL
    w_in = w['w_in'][l]
    c_q, c_kv, c_kr = Q_LORA, Q_LORA + KV_LORA, MLA_IN
    c_rw, c_cv = MLA_IN + RWKV_IN, MLA_IN + RWKV_IN + 2 * CONV_DIM
    half = ROPE_DIM // 2
    swap = np.concatenate([np.arange(half, ROPE_DIM), np.arange(half)])
    wkr = w_in[:, c_kv:c_kr]
    q_up = w['q_up'][l].reshape(Q_LORA, MLA_HEADS, NOPE_DIM + ROPE_DIM)
    q_rope = q_up[:, :, NOPE_DIM:]
    wukq = jnp.zeros((KV_LORA, PAIR_KEYS), F32)
    for hd in range(MLA_HEADS):
        wukq = wukq.at[:, 128 * hd:128 * hd + NOPE_DIM].set(w['w_uk'][l][:, hd, :])
    zl = jnp.zeros((W_LORA, RWKV_DIM), F32)
    head = np.arange(RWKV_DIM) // RWKV_HEAD
    wr = jnp.zeros((D, ROUTER_LANES), F32)
    wr = wr.at[:, 0:N_EXPERTS].set(w['router_expert_w'][l]).at[:, N_EXPERTS:N_EXPERTS + N_GROUPS].set(
        w['router_group_w'][l])
    br = jnp.zeros((1, ROUTER_LANES), F32)
    br = br.at[0, 0:N_EXPERTS].set(w['router_expert_b'][l]).at[0, N_EXPERTS:N_EXPERTS + N_GROUPS].set(
        w['router_group_b'][l])
    r2 = lambda a: a.reshape(1, -1)
    pair_cols = lambda a: jnp.concatenate([a[j::MOE_EPS] for j in range(MOE_EPS)], axis=2)
    return dict(
        g1=r2(w['norm1_g'][l]), g2n=r2(w['norm2_g'][l]),
        wq=w_in[:, :c_q].astype(BF16), wkv=w_in[:, c_q:c_kv].astype(BF16),
        wkr=jnp.concatenate([jnp.tile(wkr, (1, 8)), jnp.tile(wkr[:, swap], (1, 8))], axis=1).astype(BF16),
        wrw=w_in[:, c_kr:c_rw].astype(BF16), wcv=w_in[:, c_rw:c_cv].astype(BF16), wg=w_in[:, c_cv:].astype(BF16),
        wkvt=w_in[:, c_q:c_kv].T.astype(BF16), kvgc=w['kv_norm_g'][l].reshape(-1, 1),
        qg=r2(w['q_norm_g'][l]), wqnt=q_up[:, :, :NOPE_DIM].reshape(Q_LORA, -1).T.astype(BF16),
        wukq=wukq.astype(BF16), kvg=r2(w['kv_norm_g'][l]),
        wqrt=q_rope.reshape(Q_LORA, -1).T.astype(BF16), wqrst=q_rope[:, :, swap].reshape(Q_LORA, -1).T.astype(BF16),
        wuvt=jnp.transpose(w['w_uv'][l], (1, 2, 0)).astype(BF16), proj_a=w['proj_a'][l].astype(BF16),
        mu=r2(w['rwkv_mu'][l]), w0=r2(w['rwkv_w0'][l]), a0=r2(w['rwkv_a0'][l]),
        w2p=jnp.concatenate([w['rwkv_w2'][l], zl], axis=0).astype(BF16),
        a2p=jnp.concatenate([zl, w['rwkv_a2'][l]], axis=0).astype(BF16),
        g2=w['rwkv_g2'][l].astype(BF16), k_k=r2(w['rwkv_k_k'][l]), k_a=r2(w['rwkv_k_a'][l]),
        r_k=r2(w['rwkv_r_k'][l]), ln_g=r2(w['rwkv_ln_g'][l]), ln_b=r2(w['rwkv_ln_b'][l]),
        ones_bd=jnp.asarray((head[:, None] == head[None, :]).astype(np.float32)).astype(BF16),
        proj_b=w['proj_b'][l].astype(BF16),
        conv_dw=w['conv_dw'][l], conv_dw_b=r2(w['conv_dw_b'][l]), conv_ln_g=r2(w['conv_ln_g'][l]),
        conv_ln_b=r2(w['conv_ln_b'][l]), proj_c=w['proj_c'][l].astype(BF16),
        w_out=w['w_out'][l].astype(BF16), wr=wr, br=br,
        wgu=jnp.concatenate([pair_cols(w['moe_w_gate'][l]), pair_cols(w['moe_w_up'][l])], axis=2).astype(BF16),
        wd=w['moe_w_down'][l].reshape(N_EXPERTS // MOE_EPS, MOE_EPS * D_EXPERT, D).astype(BF16),
    )


def _rope_tables(past, T):
    half = ROPE_DIM // 2
    inv = ROPE_THETA ** (-jnp.arange(half, dtype=F32) / half)
    ang = (past + jnp.arange(T)).astype(F32)[:, None] * inv[None, :]
    cos, sin = jnp.cos(ang), jnp.sin(ang)
    cos8 = jnp.tile(jnp.concatenate([cos, cos], axis=1), (1, 8))
    sin8 = jnp.tile(jnp.concatenate([-sin, sin], axis=1), (1, 8))
    return cos8, sin8, cos8.T, sin8.T


def _row_tile(T, cap):
    return min(T, cap)


def _trunk_layer(x, mod, cache_lat, cache_rope, shift_st, wkv_st, conv_st, lw, final_g, final):
    B, T, D = x.shape
    past = 0 if cache_lat is None else cache_lat.shape[1]
    tm = _row_tile(T, 512)
    lat, kr, keys, latt, qt, prw, glu = _in_proj(x, mod, lw, _rope_tables(past, T), _row_tile(T, 1024))

    _, t_pad, _, s_pad, n_keys = _attn_tiles(T, past)
    row_pad = ((0, 0), (0, s_pad - n_keys), (0, 0))
    if past:
        lat_all = jnp.pad(jnp.concatenate([cache_lat, lat], axis=1), row_pad)
        rope_all = jnp.pad(jnp.concatenate([cache_rope, kr], axis=1), row_pad)
        key_tile = max(d for d in range(8, 2305, 8) if s_pad % d == 0)
        keys = _key_up(lat_all, jnp.tile(rope_all, (1, 1, 8)), lw['wukq'], key_tile)
        old_t = jnp.concatenate([jnp.swapaxes(cache_lat, 1, 2).astype(BF16),
                                 jnp.ones((B, VT_ROWS - KV_LORA, past), BF16)], axis=1)
        latt = jnp.concatenate([old_t, latt], axis=2)
    elif s_pad != n_keys:
        keys = jnp.pad(keys, row_pad)
    if s_pad != n_keys:
        latt = jnp.pad(latt, ((0, 0), (0, 0), (0, s_pad - n_keys)))
    if t_pad != T:
        qt = jnp.pad(qt, ((0, 0), (0, 0), (0, 0), (0, t_pad - T)))
    o_a = _attention(qt, keys, latt, lw['wuvt'], lw['proj_a'], past, T)[:, :T]

    L = WKV_CHUNK
    Tp = -(-T // L) * L
    prw_p = prw if Tp == T else jnp.pad(prw, ((0, 0), (0, Tp - T), (0, 0)))
    shift0 = jnp.zeros((B, 1, RWKV_IN), F32) if shift_st is None else shift_st[:, None, :]
    W = RWKV_DIM
    s0 = jnp.zeros((B, W, W), F32)
    if wkv_st is not None:
        for hd in range(RWKV_HEADS):
            o = hd * RWKV_HEAD
            s0 = s0.at[:, o:o + RWKV_HEAD, o:o + RWKV_HEAD].set(wkv_st[:, hd])
    yb, s_bd = _wkv(prw_p, shift0, s0, lw, L if Tp == T else T)
    yb = yb[:, :T]
    wkv_new = jnp.stack([s_bd[:, hd * RWKV_HEAD:(hd + 1) * RWKV_HEAD, hd * RWKV_HEAD:(hd + 1) * RWKV_HEAD]
                         for hd in range(RWKV_HEADS)], axis=1)

    pad_rows = CONV_HALO - (CONV_WIDTH - 1)
    st = jnp.zeros((B, CONV_WIDTH - 1, CONV_DIM), F32) if conv_st is None else conv_st
    x1, h2, comb, conv_tail = _merge(x, mod, o_a, yb, glu, jnp.pad(st, ((0, 0), (pad_rows, 0), (0, 0))), lw, tm)
    conv_new = conv_tail[:, pad_rows:]

    if T >= 1024:
        x2 = _moe(h2, comb, x1, mod[:, 5:6], lw, final_g, 1024, final)
    else:
        gt = jnp.broadcast_to(mod[:, 5:6], (B, T, D)).reshape(1, B * T, D)
        x2 = _moe(h2.reshape(1, B * T, D), comb.reshape(1, B * T, ROUTER_LANES), x1.reshape(1, B * T, D), gt, lw,
                  final_g, B * T, final).reshape(B, T, D)
    return x2, (lat, kr, prw[:, -1], wkv_new, conv_new)


def kernel(x_prompt, x_sample, c_prompt, c_sample, cache_kv_latent, cache_k_rope, state_rwkv_shift, state_rwkv_wkv, state_conv, ada_w, ada_b, norm1_g, norm2_g, w_in, q_norm_g, q_up, kv_norm_g, w_uk, w_uv, proj_a, rwkv_mu, rwkv_w0, rwkv_w2, rwkv_a0, rwkv_a2, rwkv_g2, rwkv_k_k, rwkv_k_a, rwkv_r_k, rwkv_ln_g, rwkv_ln_b, proj_b, conv_dw, conv_dw_b, conv_ln_g, conv_ln_b, proj_c, w_out, router_group_w, router_group_b, router_expert_w, router_expert_b, moe_w_gate, moe_w_up, moe_w_down, final_g):
    w = dict(norm1_g=norm1_g, norm2_g=norm2_g, w_in=w_in, q_norm_g=q_norm_g, q_up=q_up, kv_norm_g=kv_norm_g,
             w_uk=w_uk, w_uv=w_uv, proj_a=proj_a, rwkv_mu=rwkv_mu, rwkv_w0=rwkv_w0, rwkv_w2=rwkv_w2,
             rwkv_a0=rwkv_a0, rwkv_a2=rwkv_a2, rwkv_g2=rwkv_g2, rwkv_k_k=rwkv_k_k, rwkv_k_a=rwkv_k_a,
             rwkv_r_k=rwkv_r_k.reshape(rwkv_r_k.shape[0], -1), rwkv_ln_g=rwkv_ln_g, rwkv_ln_b=rwkv_ln_b,
             proj_b=proj_b, conv_dw=conv_dw, conv_dw_b=conv_dw_b, conv_ln_g=conv_ln_g, conv_ln_b=conv_ln_b,
             proj_c=proj_c, w_out=w_out, router_group_w=router_group_w, router_group_b=router_group_b,
             router_expert_w=router_expert_w, router_expert_b=router_expert_b, moe_w_gate=moe_w_gate,
             moe_w_up=moe_w_up, moe_w_down=moe_w_down)
    depth = ada_w.shape[0]
    bp = x_prompt.shape[0]
    D = D_MODEL
    mod_all = _ada_mod(jnp.concatenate([c_prompt, c_sample], axis=0), ada_w, ada_b)
    fg = final_g.reshape(1, D)
    hp, hs = x_prompt, x_sample
    new_p, new_s = [], []
    for l in range(depth):
        lw = _layer_weights(l, w)
        mod_p = mod_all[l, :bp].reshape(bp, N_MOD, D)
        mod_s = mod_all[l, bp:].reshape(-1, N_MOD, D)
        final = l == depth - 1
        hp, st_p = _trunk_layer(hp, mod_p, None, None, None, None, None, lw, fg, final)
        hs, st_s = _trunk_layer(hs, mod_s, cache_kv_latent[l], cache_k_rope[l], state_rwkv_shift[l],
                                state_rwkv_wkv[l], state_conv[l], lw, fg, final)
        new_p.append(st_p)
        new_s.append(st_s)
    stack = lambda states, i: jnp.stack([s[i] for s in states], axis=0)
    return ((hp, hs) + tuple(stack(new_p, i) for i in range(5)) + tuple(stack(new_s, i) for i in range(5)))
```

```python
import functools

import numpy as np
import jax
import jax.numpy as jnp
from jax import lax
from jax.experimental import pallas as pl
from jax.experimental.pallas import tpu as pltpu

F32 = jnp.float32
BF16 = jnp.bfloat16

D_MODEL = 1024
CHUNK = 64
NORM_EPS = 1e-6
MLA_HEADS = 8
Q_LORA = 384
KV_LORA = 256
NOPE_DIM = 64
ROPE_DIM = 32
V_HEAD = 64
ROPE_THETA = 10000.0
SM_SCALE = (NOPE_DIM + ROPE_DIM) ** -0.5
LOG2_E = 1.4426950408889634
RWKV_HEADS = 4
RWKV_HEAD = 64
RWKV_DIM = RWKV_HEADS * RWKV_HEAD
W_LORA = 64
A_LORA = 64
G_LORA = 128
RWKV_IN = 3 * RWKV_DIM + W_LORA + A_LORA + G_LORA
GN_EPS = 64e-5
CONV_DIM = 256
CONV_WIDTH = 31
LN_EPS = 1e-5
MLA_IN = Q_LORA + KV_LORA + ROPE_DIM
N_MOD = 6
N_GROUPS = 4
EXPERTS_PER_GROUP = 4
N_EXPERTS = 16
D_EXPERT = 256
WKV_CHUNK = 64
CONV_HALO = 32
ROUTER_LANES = 128
PAIR_KEYS = 4 * 256
VT_ROWS = KV_LORA + 16
MOE_EPS = 4
QK_AHEAD = 8
IN_PROJ_ROWS = 1024
MERGE_ROWS = 512
MOE_ROWS = 1024
MOE_ROW_BLOCK = 256
CHUNK_SHIFT = 6
HEAD_SHIFT = 6
assert 1 << CHUNK_SHIFT == CHUNK and 1 << HEAD_SHIFT == RWKV_HEAD
VMEM_LIMIT = 56 * 1024 * 1024


def _cp(sem):
    return pltpu.CompilerParams(dimension_semantics=sem, vmem_limit_bytes=VMEM_LIMIT)


def _full(a):
    nd = a.ndim
    return pl.BlockSpec(a.shape, lambda *_: (0,) * nd)


def _dot(a, b, dims=(((1,), (0,)), ((), ()))):
    return lax.dot_general(a.astype(BF16), b.astype(BF16), dims, preferred_element_type=F32)


def _split(a):
    hi = a.astype(BF16)
    lo = (a - hi.astype(F32)).astype(BF16)
    return hi, lo


def _dot_hl(a, b, dims=(((1,), (0,)), ((), ()))):
    hi, lo = _split(a)
    bb = b.astype(BF16)
    return (lax.dot_general(hi, bb, dims, preferred_element_type=F32)
            + lax.dot_general(lo, bb, dims, preferred_element_type=F32))


def _dot_split(a, b):
    m, n = a.shape[0], b.shape[1]
    ah, al = _split(a)
    bh, bl = _split(b)
    p = jnp.dot(jnp.concatenate([ah, al], axis=0), jnp.concatenate([bh, bl], axis=1), preferred_element_type=F32)
    return (p[0:m, 0:n] + p[0:m, n:2 * n]) + (p[m:2 * m, 0:n] + p[m:2 * m, n:2 * n])


def _sigmoid(x):
    return 0.5 * jnp.tanh(0.5 * x) + 0.5


def _rms(x, g):
    return x * lax.rsqrt(jnp.mean(x * x, axis=-1, keepdims=True) + NORM_EPS) * g


def _ada_kernel(c_ref, w_ref, b_ref, o_ref):
    c = c_ref[...]
    o_ref[0] = _dot(c * _sigmoid(c), w_ref[0]) + b_ref[0]


def _ada_mod(c_all, ada_w, ada_b):
    L, D, C = ada_w.shape
    R = c_all.shape[0]
    nb = C // D
    return pl.pallas_call(
        _ada_kernel,
        grid=(L, nb),
        in_specs=[pl.BlockSpec((R, D), lambda l, j: (0, 0)),
                  pl.BlockSpec((1, D, D), lambda l, j: (l, 0, j)),
                  pl.BlockSpec((1, 1, D), lambda l, j: (l, 0, j))],
        out_specs=pl.BlockSpec((1, R, D), lambda l, j: (l, 0, j)),
        out_shape=jax.ShapeDtypeStruct((L, R, C), F32),
        compiler_params=_cp(("parallel", "parallel")),
        name="ada_mod",
    )(c_all, ada_w, ada_b.reshape(L, 1, C))


_NT = (((1,), (1,)), ((), ()))


def _pair_keys(lat, rope8, wukq):
    lane = lax.broadcasted_iota(jnp.int32, rope8.shape, 1) & 127
    rope_slot = jnp.where((lane >= NOPE_DIM) & (lane < NOPE_DIM + ROPE_DIM), rope8, 0.0)
    return (_dot(lat, wukq) + jnp.concatenate([rope_slot] * (MLA_HEADS // 2), axis=1)).astype(BF16)


def _in_proj_kernel(x_ref, mod_ref, g1_ref, wq_ref, wkv_ref, wkvt_ref, wkr_ref, wrw_ref, wcv_ref, qg_ref, wqnt_ref,
                    wqrt_ref, wqrst_ref, wukq_ref, kvg_ref, kvgc_ref, cos_ref, sin_ref, cost_ref, sint_ref,
                    lat_ref, kr_ref, kq_ref, latt_ref, qt_ref, prw_ref, glu_ref):
    x = x_ref[0]
    mod = mod_ref[0]
    tm = x.shape[0]
    h = _rms(x, g1_ref[...]) * (1.0 + mod[1:2]) + mod[0:1]
    hb = h.astype(BF16)
    qn = _rms(_dot(hb, wq_ref[...]), qg_ref[...]).astype(BF16)
    qnt = _dot(wqnt_ref[...], qn, _NT)
    qrt = _dot(wqrt_ref[...], qn, _NT)
    qrst = _dot(wqrst_ref[...], qn, _NT)
    qrope = qrt * cost_ref[...] + qrst * sint_ref[...]
    zpad = jnp.zeros((128 - NOPE_DIM - ROPE_DIM, tm), F32)
    for hd in range(MLA_HEADS):
        qh = jnp.concatenate([qnt[NOPE_DIM * hd:NOPE_DIM * (hd + 1)], qrope[ROPE_DIM * hd:ROPE_DIM * (hd + 1)],
                              zpad], axis=0)
        qt_ref[0, hd] = (qh * (SM_SCALE * LOG2_E)).astype(BF16)
    lat = _rms(_dot(hb, wkv_ref[...]), kvg_ref[...])
    lat_ref[0] = lat
    if tm % 128 == 0:
        latt = lat.T
    else:
        pkvt = _dot(wkvt_ref[...], hb, _NT)
        latt = pkvt * lax.rsqrt(jnp.mean(pkvt * pkvt, axis=0, keepdims=True) + NORM_EPS) * kvgc_ref[...]
    latt_ref[0, 0:KV_LORA, :] = latt.astype(BF16)
    latt_ref[0, KV_LORA:VT_ROWS, :] = jnp.ones((VT_ROWS - KV_LORA, tm), BF16)
    pkr = _dot(hb, wkr_ref[...])
    krt = pkr[:, 0:256] * cos_ref[...] + pkr[:, 256:512] * sin_ref[...]
    kr_ref[0] = krt[:, 0:ROPE_DIM]
    kq_ref[0] = _pair_keys(lat, krt, wukq_ref[...])
    prw_ref[0] = _dot(hb, wrw_ref[...])
    pcv = _dot(hb, wcv_ref[...])
    glu_ref[0] = pcv[:, 0:CONV_DIM] * _sigmoid(pcv[:, CONV_DIM:2 * CONV_DIM])


def _in_proj(x, mod, lw, tabs, tm):
    B, T, D = x.shape
    nT = T // tm
    cos8, sin8, cos8t, sin8t = tabs
    row = lambda w: pl.BlockSpec((1, tm, w), lambda b, i: (b, i, 0))
    col = pl.BlockSpec((1, VT_ROWS, tm), lambda b, i: (b, 0, i))
    args = (lw['g1'], lw['wq'], lw['wkv'], lw['wkvt'], lw['wkr'], lw['wrw'], lw['wcv'], lw['qg'], lw['wqnt'],
            lw['wqrt'], lw['wqrst'], lw['wukq'], lw['kvg'], lw['kvgc'])
    return pl.pallas_call(
        _in_proj_kernel,
        grid=(B, nT),
        in_specs=[row(D), pl.BlockSpec((1, N_MOD, D), lambda b, i: (b, 0, 0))] + [_full(a) for a in args]
                 + [pl.BlockSpec((tm, 256), lambda b, i: (i, 0))] * 2
                 + [pl.BlockSpec((256, tm), lambda b, i: (0, i))] * 2,
        out_specs=[row(KV_LORA), row(ROPE_DIM), row(PAIR_KEYS), col,
                   pl.BlockSpec((1, MLA_HEADS, 128, tm), lambda b, i: (b, 0, 0, i)),
                   row(RWKV_IN), row(CONV_DIM)],
        out_shape=[jax.ShapeDtypeStruct((B, T, KV_LORA), F32),
                   jax.ShapeDtypeStruct((B, T, ROPE_DIM), F32),
                   jax.ShapeDtypeStruct((B, T, PAIR_KEYS), BF16),
                   jax.ShapeDtypeStruct((B, VT_ROWS, T), BF16),
                   jax.ShapeDtypeStruct((B, MLA_HEADS, 128, T), BF16),
                   jax.ShapeDtypeStruct((B, T, RWKV_IN), F32),
                   jax.ShapeDtypeStruct((B, T, CONV_DIM), F32)],
        compiler_params=_cp(("parallel", "parallel")),
        name="in_proj",
    )(x, mod, *args, cos8, sin8, cos8t, sin8t)


def _key_up_kernel(lat_ref, rope8_ref, wukq_ref, kq_ref):
    kq_ref[0] = _pair_keys(lat_ref[0], rope8_ref[0], wukq_ref[...])


def _key_up(lat, rope8, wukq, tm):
    B, S, _ = lat.shape
    row = lambda w: pl.BlockSpec((1, tm, w), lambda b, i: (b, i, 0))
    return pl.pallas_call(
        _key_up_kernel,
        grid=(B, S // tm),
        in_specs=[row(KV_LORA), row(256), _full(wukq)],
        out_specs=row(PAIR_KEYS),
        out_shape=jax.ShapeDtypeStruct((B, S, PAIR_KEYS), BF16),
        compiler_params=_cp(("parallel", "parallel")),
        name="key_up",
    )(lat, rope8, wukq)


def _attn_kernel(qi_ref, ki_ref, last_ref, qt_ref, k_ref, vt_ref, wuvt_ref, pa_ref, o_ref,
                 qcat, m_s, acc, *, tq, tk, past, n_keys):
    p = pl.program_id(1)
    qi = qi_ref[p]
    ki = ki_ref[p]
    H = MLA_HEADS
    NP = H // 2
    nh = tq // 128
    NC = NP * nh
    diag_aligned = tq == tk and past % tk == 0

    @pl.when(ki == 0)
    def _():
        qcat[...] = jnp.zeros(qcat.shape, BF16)
        for pr in range(NP):
            for hf in range(nh):
                pos = slice(128 * hf, 128 * (hf + 1))
                qcat[pr * nh + hf, 0:128, 0:128] = qt_ref[0, 2 * pr, :, pos]
                qcat[pr * nh + hf, 128:256, 128:256] = qt_ref[0, 2 * pr + 1, :, pos]
        m_s[...] = jnp.full(m_s.shape, -jnp.inf, F32)
        acc[...] = jnp.zeros(acc.shape, F32)

    def step(bias):
        def nkeys(ch):
            return 128 * (ch % nh + 1) if (bias is not None and diag_aligned) else tk

        scores = {}

        def qk(ch):
            pr = ch // nh
            scores[ch] = jnp.dot(k_ref[0, 0:nkeys(ch), 256 * pr:256 * (pr + 1)], qcat[ch],
                                 preferred_element_type=F32)

        for ch in range(min(QK_AHEAD, NC)):
            qk(ch)
        for ch in range(NC):
            if ch + QK_AHEAD < NC:
                qk(ch + QK_AHEAD)
            s = scores.pop(ch)
            if bias is not None:
                s = s + bias[ch % nh][0:nkeys(ch), :]
            m_prev = m_s[ch]
            m_new = jnp.maximum(m_prev, jnp.max(s, axis=0, keepdims=True))
            alpha = jnp.exp2(m_prev - m_new)
            pe = jnp.exp2(s - m_new)
            acc[ch] = alpha * acc[ch] + jnp.dot(vt_ref[0, :, 0:nkeys(ch)], pe.astype(BF16),
                                                preferred_element_type=F32)
            m_s[ch] = m_new

    is_last = last_ref[p] == 1

    @pl.when(jnp.logical_not(is_last))
    def _():
        step(None)

    @pl.when(is_last)
    def _():
        kpos = ki * tk + lax.broadcasted_iota(jnp.int32, (tk, 1), 0)
        biases = []
        for hf in range(nh):
            t = 128 * hf + lax.broadcasted_iota(jnp.int32, (1, 128), 1)
            limit = jnp.minimum((((past + qi * tq + t) >> CHUNK_SHIFT) + 1) << CHUNK_SHIFT, n_keys)
            bias = jnp.where(kpos < limit, 0.0, -jnp.inf)
            biases.append(jnp.concatenate([bias, bias], axis=1))
        step(biases)
        heads = []
        for pr in range(NP):
            o = [acc[pr * nh + hf, 0:KV_LORA, :] / acc[pr * nh + hf, KV_LORA:KV_LORA + 1, :]
                 for hf in range(nh)]
            for j in range(2):
                oh = jnp.concatenate([x[:, 128 * j:128 * (j + 1)] for x in o], axis=1)
                heads.append(_dot(wuvt_ref[2 * pr + j], oh))
        o_ref[0] = _dot(jnp.concatenate(heads, axis=0), pa_ref[...], (((0,), (0,)), ((), ())))


def _attn_tiles(T, past):
    n_keys = past + T
    tq = 512 if T % 512 == 0 else 128
    t_pad = -(-T // tq) * tq
    tk = -(-n_keys // 128) * 128 if n_keys <= 2304 else 512
    s_pad = -(-n_keys // tk) * tk
    return tq, t_pad, tk, s_pad, n_keys


def _attention(qt, keys, latt, wuvt, proj_a, past, T):
    B, H = qt.shape[:2]
    tq, t_pad, tk, s_pad, n_keys = _attn_tiles(T, past)
    assert keys.shape[1] == s_pad and latt.shape[2] == s_pad and qt.shape[3] == t_pad and past % CHUNK == 0
    qi, ki, last = [], [], []
    for i in range(t_pad // tq):
        lim = min(((past + (i + 1) * tq - 1) // CHUNK + 1) * CHUNK, n_keys)
        nk = -(-lim // tk)
        for j in range(nk):
            qi.append(i), ki.append(j), last.append(int(j == nk - 1))
    npairs = len(qi)
    tabs = [jnp.asarray(np.asarray(a, np.int32)) for a in (qi, ki, last)]
    nc = (H // 2) * (tq // 128)
    grid_spec = pltpu.PrefetchScalarGridSpec(
        num_scalar_prefetch=3,
        grid=(B, npairs),
        in_specs=[pl.BlockSpec((1, H, 128, tq), lambda b, p, qi, ki, la: (b, 0, 0, qi[p])),
                  pl.BlockSpec((1, tk, PAIR_KEYS), lambda b, p, qi, ki, la: (b, ki[p], 0)),
                  pl.BlockSpec((1, VT_ROWS, tk), lambda b, p, qi, ki, la: (b, 0, ki[p])),
                  pl.BlockSpec(wuvt.shape, lambda b, p, qi, ki, la: (0, 0, 0)),
                  pl.BlockSpec(proj_a.shape, lambda b, p, qi, ki, la: (0, 0))],
        out_specs=pl.BlockSpec((1, tq, D_MODEL), lambda b, p, qi, ki, la: (b, qi[p], 0)),
        scratch_shapes=[pltpu.VMEM((nc, 256, 256), BF16), pltpu.VMEM((nc, 1, 256), F32),
                        pltpu.VMEM((nc, VT_ROWS, 256), F32)])
    return pl.pallas_call(
        functools.partial(_attn_kernel, tq=tq, tk=tk, past=past, n_keys=n_keys),
        grid_spec=grid_spec,
        out_shape=jax.ShapeDtypeStruct((B, t_pad, D_MODEL), F32),
        compiler_params=_cp(("parallel", "arbitrary")),
        name="attention",
    )(*tabs, qt, keys, latt, wuvt, proj_a)


def _head_sum(x, ones_bd):
    return _dot_hl(x, ones_bd)


def _rwkv_features(pr, prev, wts):
    mu, w0, w2p, a0, a2p, g2, k_k, k_a, r_k, ones_bd = wts
    xs = pr + (prev - pr) * mu
    r = xs[:, 0:256]
    k = xs[:, 256:512]
    v = xs[:, 512:768]
    wa = xs[:, 768:896]
    gl = xs[:, 896:1024]
    nz = -(w0 + _dot(jnp.tanh(wa), w2p))
    softplus = jnp.maximum(nz, 0.0) + jnp.log(1.0 + jnp.exp(-jnp.abs(nz)))
    lw = -jnp.exp(-softplus - 0.5)
    a = _sigmoid(a0 + _dot(wa, a2p))
    g = _dot(_sigmoid(gl), g2)
    kk = k * k_k
    kk = kk * lax.rsqrt(_head_sum(kk * kk, ones_bd) + 1e-12)
    kf = k * (1.0 + (a - 1.0) * k_a)
    bonus = _head_sum(r * kf * r_k, ones_bd) * v
    return r, kf, kk * a, kk, v, lw, g, bonus


def _wkv_kernel(prw_ref, shift_ref, s0_ref, mu_ref, w0_ref, w2_ref, a0_ref, a2_ref, g2_ref, kk_ref, ka_ref, rk_ref,
                ones_ref, lng_ref, lnb_ref, yb_ref, sout_ref, S, carry, *, nb, G, n_valid):
    L = WKV_CHUNK
    W = RWKV_DIM

    @pl.when(pl.program_id(0) == 0)
    def _():
        S[...] = s0_ref[...]
        carry[...] = shift_ref[...]

    wts = tuple(ref[...] for ref in (mu_ref, w0_ref, w2_ref, a0_ref, a2_ref, g2_ref, kk_ref, ka_ref, rk_ref, ones_ref))
    ones_bd = wts[-1]
    ln_g = lng_ref[...]
    ln_b = lnb_ref[...]
    ri = lax.broadcasted_iota(jnp.int32, (W, W), 0)
    ci = lax.broadcasted_iota(jnp.int32, (W, W), 1)
    same = (ri >> HEAD_SHIFT) == (ci >> HEAD_SHIFT)
    in_head = RWKV_HEAD - 1
    strict = same & ((ci & in_head) < (ri & in_head))
    incl = same & ((ci & in_head) <= (ri & in_head))
    eye = (ri == ci).astype(F32)
    tri = (lax.broadcasted_iota(jnp.int32, (L, L), 1) <= lax.broadcasted_iota(jnp.int32, (L, L), 0)).astype(BF16)
    lane_head = lax.broadcasted_iota(jnp.int32, (L, W), 1) >> HEAD_SHIFT
    row = lax.broadcasted_iota(jnp.int32, (L, 1), 0)
    nt = (((1,), (1,)), ((), ()))
    tn = (((0,), (0,)), ((), ()))

    def stack(x):
        return jnp.concatenate([jnp.where(lane_head == hd, x, 0.0) for hd in range(RWKV_HEADS)], axis=0)

    def collapse(z):
        return z[0:L] + z[L:2 * L] + z[2 * L:3 * L] + z[3 * L:4 * L]

    def features(i):
        pr = prw_ref[i]
        prev = jnp.where(row == 0, carry[i], pltpu.roll(pr, 1, 0))
        carry[i] = pr[L - 1:L, :]
        f = _rwkv_features(pr, prev, wts)
        if n_valid < L:
            f = tuple(jnp.where(row < n_valid, x, 0.0) for x in f[:6]) + f[6:]
        return f

    def body(it, _):
        ids = [it * G + g for g in range(G)]
        each = lambda f, *xs: [f(*a) for a in zip(*xs)]
        r, kf, bb, kk, v, ll, gate, bonus = zip(*[features(i) for i in ids])
        s_prev = [S[i] for i in ids]
        split = each(_split, ll)
        c = each(lambda hl: jnp.dot(tri, hl[0], preferred_element_type=F32)
                 + jnp.dot(tri, hl[1], preferred_element_type=F32), split)
        c_last = each(lambda x: x[L - 1:L, :], c)
        e_neg = each(lambda x: jnp.exp(-x), c)
        e_end = each(lambda x, y: jnp.exp(y - x), c, c_last)
        rt = each(lambda x, y: stack(x * jnp.exp(y)), r, c)
        kkt = each(lambda x, y, z: stack(x * jnp.exp(y - z)), kk, c, ll)
        bt = each(lambda x, y: stack(x * y), bb, e_neg)
        kt = each(lambda x, y: stack(x * y), kf, e_neg)
        a_m = each(lambda x, y: jnp.where(strict, _dot(x, y, nt), 0.0), kkt, bt)
        b_m = each(lambda x, y: jnp.where(strict, _dot(x, y, nt), 0.0), kkt, kt)
        m1 = each(lambda x, y: jnp.where(incl, _dot(x, y, nt), 0.0), rt, bt)
        m2 = each(lambda x, y: jnp.where(incl, _dot(x, y, nt), 0.0), rt, kt)
        t_m = each(lambda x: eye - x, a_m)
        a_p = a_m
        for _ in range(5):
            a_p = each(lambda x: _dot(x, x), a_p)
            t_m = each(lambda x, y: x + _dot(x, y), t_m, a_p)
        vs = each(stack, v)
        w1 = each(lambda x, y: collapse(_dot(x, y)), t_m, kkt)
        bv = each(_dot, b_m, vs)
        w2 = each(lambda x, y: collapse(_dot(x, y)), t_m, bv)
        u = each(lambda x, y, z: -(_dot(x, y, nt) + z), w1, s_prev, w2)
        y0 = each(lambda x, y: _dot(collapse(x), y, nt), rt, s_prev)
        y1 = each(lambda a, b_, c_, d: collapse(_dot(a, stack(b_)) + _dot(c_, d)), m1, u, m2, vs)
        upd = each(lambda a, b_, c_, d, e: _dot(a, b_ * e, tn) + _dot(c_, d * e, tn), u, bb, v, kf, e_end)
        for g, i in enumerate(ids):
            S[i] = s_prev[g] * jnp.exp(c_last[g]) + jnp.where(same, upd[g], 0.0)
            y = y0[g] + y1[g]
            mean = _head_sum(y, ones_bd) * (1.0 / RWKV_HEAD)
            yc = y - mean
            var = _head_sum(yc * yc, ones_bd) * (1.0 / RWKV_HEAD)
            yb = (yc * lax.rsqrt(var + GN_EPS) * ln_g + ln_b + bonus[g]) * gate[g]
            yb_ref[i] = yb.astype(BF16)
        return 0

    lax.fori_loop(0, nb // G, body, 0)

    @pl.when(pl.program_id(0) == pl.num_programs(0) - 1)
    def _():
        sout_ref[...] = S[...]


def _wkv(prw, shift0, s0, lw, n_valid):
    B, T, _ = prw.shape
    L = WKV_CHUNK
    W = RWKV_DIM
    assert T % L == 0 and (n_valid == L or T == L)
    args = (lw['mu'], lw['w0'], lw['w2p'], lw['a0'], lw['a2p'], lw['g2'], lw['k_k'], lw['k_a'], lw['r_k'],
            lw['ones_bd'], lw['ln_g'], lw['ln_b'])
    st = pl.BlockSpec((B, W, W), lambda c: (0, 0, 0))
    return pl.pallas_call(
        functools.partial(_wkv_kernel, nb=B, G=8 if B % 8 == 0 else 1, n_valid=n_valid),
        grid=(T // L,),
        in_specs=[pl.BlockSpec((B, L, RWKV_IN), lambda c: (0, c, 0)), _full(shift0), st] + [_full(a) for a in args],
        out_specs=[pl.BlockSpec((B, L, W), lambda c: (0, c, 0)), st],
        out_shape=[jax.ShapeDtypeStruct((B, T, W), BF16), jax.ShapeDtypeStruct((B, W, W), F32)],
        scratch_shapes=[pltpu.VMEM((B, W, W), F32), pltpu.VMEM((B, 1, RWKV_IN), F32)],
        compiler_params=_cp(("arbitrary",)),
        name="wkv",
    )(prw, shift0, s0, *args)


def _conv_stage(ext, glu, st_ref, first, tm):
    @pl.when(first)
    def _():
        ext[0:CONV_HALO, :] = st_ref[0]
    ext[CONV_HALO:CONV_HALO + tm, :] = glu


def _conv_module(ext, dw_ref, dwb, lng, lnb, new_ref, tm):
    off = CONV_HALO - (CONV_WIDTH - 1)
    ext_v = ext[...]
    n = CONV_HALO + tm
    acc = jnp.zeros((tm, CONV_DIM), F32)
    for res in range(8):
        taps = [w for w in range(CONV_WIDTH) if (off + w) % 8 == res]
        if not taps:
            continue
        sh = ext_v if res == 0 else pltpu.roll(ext_v, n - res, 0)
        for w in taps:
            base = off + w - res
            acc = acc + sh[base:base + tm, :] * dw_ref[w:w + 1, :]
    y = acc + dwb
    mu = jnp.mean(y, axis=-1, keepdims=True)
    var = jnp.mean(jnp.square(y - mu), axis=-1, keepdims=True)
    y = (y - mu) * lax.rsqrt(var + LN_EPS) * lng + lnb
    tail = ext[tm:tm + CONV_HALO, :]
    new_ref[0] = tail
    ext[0:CONV_HALO, :] = tail
    return y * _sigmoid(y)


def _merge_kernel(x_ref, mod_ref, g1_ref, wg_ref, oa_ref, yb_ref, pb_ref, glu_ref, st_ref, dw_ref, dwb_ref, clg_ref,
                  clb_ref, pc_ref, wo_ref, g2_ref, wr_ref, br_ref, x1_ref, h2_ref, comb_ref, cnew_ref, ext):
    x = x_ref[0]
    mod = mod_ref[0]
    tm = x.shape[0]
    _conv_stage(ext, glu_ref[0], st_ref, pl.program_id(1) == 0, tm)
    yc = _conv_module(ext, dw_ref, dwb_ref[...], clg_ref[...], clb_ref[...], cnew_ref, tm)
    hb = (_rms(x, g1_ref[...]) * (1.0 + mod[1:2]) + mod[0:1]).astype(BF16)
    D = D_MODEL
    merged = _sigmoid(_dot(hb, wg_ref[:, 0:D])) * oa_ref[0]
    merged = merged + _sigmoid(_dot(hb, wg_ref[:, D:2 * D])) * _dot(yb_ref[0], pb_ref[...])
    merged = merged + _sigmoid(_dot(hb, wg_ref[:, 2 * D:3 * D])) * _dot(yc, pc_ref[...])
    x1 = x + mod[2:3] * _dot(merged, wo_ref[...])
    x1_ref[0] = x1
    h2 = _rms(x1, g2_ref[...]) * (1.0 + mod[4:5]) + mod[3:4]
    h2_ref[0] = h2.astype(BF16)
    logits = _dot_split(h2, wr_ref[...]) + br_ref[...]
    lane = lax.broadcasted_iota(jnp.int32, logits.shape, 1)
    neg = -jnp.inf
    is_g = (lane >= N_EXPERTS) & (lane < N_EXPERTS + N_GROUPS)
    gl = jnp.where(is_g, logits, neg)
    gmax = jnp.max(gl, axis=1, keepdims=True)
    gsum = jnp.sum(jnp.exp(gl - gmax), axis=1, keepdims=True)
    g_val = 1.0 / gsum
    g_idx = jnp.min(jnp.where(is_g & (gl == gmax), lane, 4 * ROUTER_LANES), axis=1, keepdims=True) - N_EXPERTS
    in_grp = (lane >= g_idx * EXPERTS_PER_GROUP) & (lane < (g_idx + 1) * EXPERTS_PER_GROUP)
    el = jnp.where(in_grp, logits, neg)
    v1 = jnp.max(el, axis=1, keepdims=True)
    i1 = jnp.min(jnp.where(in_grp & (el == v1), lane, 4 * ROUTER_LANES), axis=1, keepdims=True)
    el2 = jnp.where(lane == i1, neg, el)
    v2 = jnp.max(el2, axis=1, keepdims=True)
    i2 = jnp.min(jnp.where(in_grp & (lane != i1) & (el2 == v2), lane, 4 * ROUTER_LANES), axis=1, keepdims=True)
    e2 = jnp.exp(v2 - v1)
    w1 = g_val / (1.0 + e2)
    w2 = g_val * e2 / (1.0 + e2)
    comb_ref[0] = jnp.where(lane == i1, w1, 0.0) + jnp.where(lane == i2, w2, 0.0)


def _merge(x, mod, oa, yb, glu, st_pad, lw, tm):
    B, T, D = x.shape
    row = lambda w: pl.BlockSpec((1, tm, w), lambda b, i: (b, i, 0))
    modspec = pl.BlockSpec((1, N_MOD, D), lambda b, i: (b, 0, 0))
    halo = pl.BlockSpec((1, CONV_HALO, CONV_DIM), lambda b, i: (b, 0, 0))
    ins = [(x, row(D)), (mod, modspec), (lw['g1'], None), (lw['wg'], None), (oa, row(D)), (yb, row(256)),
           (lw['proj_b'], None), (glu, row(CONV_DIM)), (st_pad, halo), (lw['conv_dw'], None),
           (lw['conv_dw_b'], None), (lw['conv_ln_g'], None), (lw['conv_ln_b'], None), (lw['proj_c'], None),
           (lw['w_out'], None), (lw['g2n'], None), (lw['wr'], None), (lw['br'], None)]
    return pl.pallas_call(
        _merge_kernel,
        grid=(B, T // tm),
        in_specs=[s if s is not None else _full(a) for a, s in ins],
        out_specs=[row(D), row(D), row(ROUTER_LANES), halo],
        out_shape=[jax.ShapeDtypeStruct((B, T, D), F32), jax.ShapeDtypeStruct((B, T, D), BF16),
                   jax.ShapeDtypeStruct((B, T, ROUTER_LANES), F32),
                   jax.ShapeDtypeStruct((B, CONV_HALO, CONV_DIM), F32)],
        scratch_shapes=[pltpu.VMEM((CONV_HALO + tm, CONV_DIM), F32)],
        compiler_params=_cp(("parallel", "arbitrary")),
        name="merge",
    )(*[a for a, _ in ins])


def _moe_kernel(h2_ref, comb_ref, x1_ref, gt_ref, wgu_ref, wd_ref, fg_ref, o_ref, acc, *, final, rb):
    s = pl.program_id(2)
    tm = h2_ref.shape[1]
    nrb = tm // rb
    DE = D_EXPERT

    @pl.when(s == 0)
    def _():
        acc[...] = jnp.zeros_like(acc)

    wgu = wgu_ref[0]
    wd = wd_ref[0]
    gus = {}

    def up(i):
        gus[i] = _dot(h2_ref[0, i * rb:(i + 1) * rb, :], wgu)

    up(0)
    for i in range(nrb):
        if i + 1 < nrb:
            up(i + 1)
        gu = gus.pop(i)
        rows = slice(i * rb, (i + 1) * rb)
        comb = comb_ref[0, rows, :]
        lane = lax.broadcasted_iota(jnp.int32, comb.shape, 1)
        cws = [jnp.sum(jnp.where(lane == MOE_EPS * s + j, comb, 0.0), axis=1, keepdims=True) for j in range(MOE_EPS)]
        cw = jnp.concatenate([jnp.broadcast_to(c, (rb, DE)) for c in cws], axis=1)
        gate = gu[:, 0:MOE_EPS * DE]
        act = gate * _sigmoid(gate) * gu[:, MOE_EPS * DE:2 * MOE_EPS * DE] * cw
        acc[rows, :] += _dot(act, wd)

    @pl.when(s == N_EXPERTS // MOE_EPS - 1)
    def _():
        x2 = x1_ref[0] + gt_ref[0] * acc[...]
        o_ref[0] = _rms(x2, fg_ref[...]) if final else x2


def _moe(h2, comb, x1, gt, lw, final_g, tm, final):
    B, T, D = x1.shape
    gt_rows = gt.shape[1]
    gt_spec = (pl.BlockSpec((1, 1, D), lambda b, i, e: (b, 0, 0)) if gt_rows == 1
               else pl.BlockSpec((1, tm, D), lambda b, i, e: (b, i, 0)))
    row = lambda w: pl.BlockSpec((1, tm, w), lambda b, i, e: (b, i, 0))
    return pl.pallas_call(
        functools.partial(_moe_kernel, final=final, rb=min(tm, MOE_ROW_BLOCK)),
        grid=(B, T // tm, N_EXPERTS // MOE_EPS),
        in_specs=[row(D), row(ROUTER_LANES), row(D), gt_spec,
                  pl.BlockSpec((1, D, 2 * MOE_EPS * D_EXPERT), lambda b, i, e: (e, 0, 0)),
                  pl.BlockSpec((1, MOE_EPS * D_EXPERT, D), lambda b, i, e: (e, 0, 0)),
                  pl.BlockSpec((1, D), lambda b, i, e: (0, 0))],
        out_specs=row(D),
        out_shape=jax.ShapeDtypeStruct((B, T, D), F32),
        scratch_shapes=[pltpu.VMEM((tm, D), F32)],
        compiler_params=_cp(("parallel", "parallel", "arbitrary")),
        name="moe",
    )(h2, comb, x1, gt, lw['wgu'], lw['wd'], final_g)


def _layer_weights(l, w):
    D = D_MODEL
    w_in = w['w_in'][l]
    c_q, c_kv, c_kr = Q_LORA, Q_LORA + KV_LORA, MLA_IN
    c_rw, c_cv = MLA_IN + RWKV_IN, MLA_IN + RWKV_IN + 2 * CONV_DIM
    half = ROPE_DIM // 2
    swap = np.concatenate([np.arange(half, ROPE_DIM), np.arange(half)])
    wkr = w_in[:, c_kv:c_kr]
    q_up = w['q_up'][l].reshape(Q_LORA, MLA_HEADS, NOPE_DIM + ROPE_DIM)
    q_rope = q_up[:, :, NOPE_DIM:]
    wukq = jnp.zeros((KV_LORA, PAIR_KEYS), F32)
    for hd in range(MLA_HEADS):
        wukq = wukq.at[:, 128 * hd:128 * hd + NOPE_DIM].set(w['w_uk'][l][:, hd, :])
    zl = jnp.zeros((W_LORA, RWKV_DIM), F32)
    head = np.arange(RWKV_DIM) // RWKV_HEAD
    wr = jnp.zeros((D, ROUTER_LANES), F32)
    wr = wr.at[:, 0:N_EXPERTS].set(w['router_expert_w'][l]).at[:, N_EXPERTS:N_EXPERTS + N_GROUPS].set(
        w['router_group_w'][l])
    br = jnp.zeros((1, ROUTER_LANES), F32)
    br = br.at[0, 0:N_EXPERTS].set(w['router_expert_b'][l]).at[0, N_EXPERTS:N_EXPERTS + N_GROUPS].set(
        w['router_group_b'][l])
    r2 = lambda a: a.reshape(1, -1)
    pair_cols = lambda a: jnp.concatenate([a[j::MOE_EPS] for j in range(MOE_EPS)], axis=2)
    return dict(
        g1=r2(w['norm1_g'][l]), g2n=r2(w['norm2_g'][l]),
        wq=w_in[:, :c_q].astype(BF16), wkv=w_in[:, c_q:c_kv].astype(BF16),
        wkr=jnp.concatenate([jnp.tile(wkr, (1, 8)), jnp.tile(wkr[:, swap], (1, 8))], axis=1).astype(BF16),
        wrw=w_in[:, c_kr:c_rw].astype(BF16), wcv=w_in[:, c_rw:c_cv].astype(BF16), wg=w_in[:, c_cv:].astype(BF16),
        wkvt=w_in[:, c_q:c_kv].T.astype(BF16), kvgc=w['kv_norm_g'][l].reshape(-1, 1),
        qg=r2(w['q_norm_g'][l]), wqnt=q_up[:, :, :NOPE_DIM].reshape(Q_LORA, -1).T.astype(BF16),
        wukq=wukq.astype(BF16), kvg=r2(w['kv_norm_g'][l]),
        wqrt=q_rope.reshape(Q_LORA, -1).T.astype(BF16), wqrst=q_rope[:, :, swap].reshape(Q_LORA, -1).T.astype(BF16),
        wuvt=jnp.transpose(w['w_uv'][l], (1, 2, 0)).astype(BF16), proj_a=w['proj_a'][l].astype(BF16),
        mu=r2(w['rwkv_mu'][l]), w0=r2(w['rwkv_w0'][l]), a0=r2(w['rwkv_a0'][l]),
        w2p=jnp.concatenate([w['rwkv_w2'][l], zl], axis=0).astype(BF16),
        a2p=jnp.concatenate([zl, w['rwkv_a2'][l]], axis=0).astype(BF16),
        g2=w['rwkv_g2'][l].astype(BF16), k_k=r2(w['rwkv_k_k'][l]), k_a=r2(w['rwkv_k_a'][l]),
        r_k=r2(w['rwkv_r_k'][l]), ln_g=r2(w['rwkv_ln_g'][l]), ln_b=r2(w['rwkv_ln_b'][l]),
        ones_bd=jnp.asarray((head[:, None] == head[None, :]).astype(np.float32)).astype(BF16),
        proj_b=w['proj_b'][l].astype(BF16),
        conv_dw=w['conv_dw'][l], conv_dw_b=r2(w['conv_dw_b'][l]), conv_ln_g=r2(w['conv_ln_g'][l]),
        conv_ln_b=r2(w['conv_ln_b'][l]), proj_c=w['proj_c'][l].astype(BF16),
        w_out=w['w_out'][l].astype(BF16), wr=wr, br=br,
        wgu=jnp.concatenate([pair_cols(w['moe_w_gate'][l]), pair_cols(w['moe_w_up'][l])], axis=2).astype(BF16),
        wd=w['moe_w_down'][l].reshape(N_EXPERTS // MOE_EPS, MOE_EPS * D_EXPERT, D).astype(BF16),
    )


def _rope_tables(past, T):
    half = ROPE_DIM // 2
    inv = ROPE_THETA ** (-jnp.arange(half, dtype=F32) / half)
    ang = (past + jnp.arange(T)).astype(F32)[:, None] * inv[None, :]
    cos, sin = jnp.cos(ang), jnp.sin(ang)
    cos8 = jnp.tile(jnp.concatenate([cos, cos], axis=1), (1, 8))
    sin8 = jnp.tile(jnp.concatenate([-sin, sin], axis=1), (1, 8))
    return cos8, sin8, cos8.T, sin8.T


def _row_tile(T, cap):
    return min(T, cap)


def _trunk_layer(x, mod, cache_lat, cache_rope, shift_st, wkv_st, conv_st, lw, final_g, final):
    B, T, D = x.shape
    past = 0 if cache_lat is None else cache_lat.shape[1]
    tm = _row_tile(T, MERGE_ROWS)
    lat, kr, keys, latt, qt, prw, glu = _in_proj(x, mod, lw, _rope_tables(past, T), _row_tile(T, IN_PROJ_ROWS))

    _, t_pad, _, s_pad, n_keys = _attn_tiles(T, past)
    row_pad = ((0, 0), (0, s_pad - n_keys), (0, 0))
    if past:
        lat_all = jnp.pad(jnp.concatenate([cache_lat, lat], axis=1), row_pad)
        rope_all = jnp.pad(jnp.concatenate([cache_rope, kr], axis=1), row_pad)
        key_tile = max(d for d in range(8, 2305, 8) if s_pad % d == 0)
        keys = _key_up(lat_all, jnp.tile(rope_all, (1, 1, 8)), lw['wukq'], key_tile)
        old_t = jnp.concatenate([jnp.swapaxes(cache_lat, 1, 2).astype(BF16),
                                 jnp.ones((B, VT_ROWS - KV_LORA, past), BF16)], axis=1)
        latt = jnp.concatenate([old_t, latt], axis=2)
    elif s_pad != n_keys:
        keys = jnp.pad(keys, row_pad)
    if s_pad != n_keys:
        latt = jnp.pad(latt, ((0, 0), (0, 0), (0, s_pad - n_keys)))
    if t_pad != T:
        qt = jnp.pad(qt, ((0, 0), (0, 0), (0, 0), (0, t_pad - T)))
    o_a = _attention(qt, keys, latt, lw['wuvt'], lw['proj_a'], past, T)[:, :T]

    L = WKV_CHUNK
    Tp = -(-T // L) * L
    prw_p = prw if Tp == T else jnp.pad(prw, ((0, 0), (0, Tp - T), (0, 0)))
    shift0 = jnp.zeros((B, 1, RWKV_IN), F32) if shift_st is None else shift_st[:, None, :]
    W = RWKV_DIM
    s0 = jnp.zeros((B, W, W), F32)
    if wkv_st is not None:
        for hd in range(RWKV_HEADS):
            o = hd * RWKV_HEAD
            s0 = s0.at[:, o:o + RWKV_HEAD, o:o + RWKV_HEAD].set(wkv_st[:, hd])
    yb, s_bd = _wkv(prw_p, shift0, s0, lw, L if Tp == T else T)
    yb = yb[:, :T]
    wkv_new = jnp.stack([s_bd[:, hd * RWKV_HEAD:(hd + 1) * RWKV_HEAD, hd * RWKV_HEAD:(hd + 1) * RWKV_HEAD]
                         for hd in range(RWKV_HEADS)], axis=1)

    pad_rows = CONV_HALO - (CONV_WIDTH - 1)
    st = jnp.zeros((B, CONV_WIDTH - 1, CONV_DIM), F32) if conv_st is None else conv_st
    x1, h2, comb, conv_tail = _merge(x, mod, o_a, yb, glu, jnp.pad(st, ((0, 0), (pad_rows, 0), (0, 0))), lw, tm)
    conv_new = conv_tail[:, pad_rows:]

    if T >= MOE_ROWS:
        x2 = _moe(h2, comb, x1, mod[:, 5:6], lw, final_g, MOE_ROWS, final)
    else:
        gt = jnp.broadcast_to(mod[:, 5:6], (B, T, D)).reshape(1, B * T, D)
        x2 = _moe(h2.reshape(1, B * T, D), comb.reshape(1, B * T, ROUTER_LANES), x1.reshape(1, B * T, D), gt, lw,
                  final_g, B * T, final).reshape(B, T, D)
    return x2, (lat, kr, prw[:, -1], wkv_new, conv_new)


def kernel(x_prompt, x_sample, c_prompt, c_sample, cache_kv_latent, cache_k_rope, state_rwkv_shift, state_rwkv_wkv, state_conv, ada_w, ada_b, norm1_g, norm2_g, w_in, q_norm_g, q_up, kv_norm_g, w_uk, w_uv, proj_a, rwkv_mu, rwkv_w0, rwkv_w2, rwkv_a0, rwkv_a2, rwkv_g2, rwkv_k_k, rwkv_k_a, rwkv_r_k, rwkv_ln_g, rwkv_ln_b, proj_b, conv_dw, conv_dw_b, conv_ln_g, conv_ln_b, proj_c, w_out, router_group_w, router_group_b, router_expert_w, router_expert_b, moe_w_gate, moe_w_up, moe_w_down, final_g):
    w = dict(norm1_g=norm1_g, norm2_g=norm2_g, w_in=w_in, q_norm_g=q_norm_g, q_up=q_up, kv_norm_g=kv_norm_g,
             w_uk=w_uk, w_uv=w_uv, proj_a=proj_a, rwkv_mu=rwkv_mu, rwkv_w0=rwkv_w0, rwkv_w2=rwkv_w2,
             rwkv_a0=rwkv_a0, rwkv_a2=rwkv_a2, rwkv_g2=rwkv_g2, rwkv_k_k=rwkv_k_k, rwkv_k_a=rwkv_k_a,
             rwkv_r_k=rwkv_r_k.reshape(rwkv_r_k.shape[0], -1), rwkv_ln_g=rwkv_ln_g, rwkv_ln_b=rwkv_ln_b,
             proj_b=proj_b, conv_dw=conv_dw, conv_dw_b=conv_dw_b, conv_ln_g=conv_ln_g, conv_ln_b=conv_ln_b,
             proj_c=proj_c, w_out=w_out, router_group_w=router_group_w, router_group_b=router_group_b,
             router_expert_w=router_expert_w, router_expert_b=router_expert_b, moe_w_gate=moe_w_gate,
             moe_w_up=moe_w_up, moe_w_down=moe_w_down)
    depth = ada_w.shape[0]
    bp = x_prompt.shape[0]
    D = D_MODEL
    mod_all = _ada_mod(jnp.concatenate([c_prompt, c_sample], axis=0), ada_w, ada_b)
    fg = final_g.reshape(1, D)
    hp, hs = x_prompt, x_sample
    new_p, new_s = [], []
    for l in range(depth):
        lw = _layer_weights(l, w)
        mod_p = mod_all[l, :bp].reshape(bp, N_MOD, D)
        mod_s = mod_all[l, bp:].reshape(-1, N_MOD, D)
        final = l == depth - 1
        hp, st_p = _trunk_layer(hp, mod_p, None, None, None, None, None, lw, fg, final)
        hs, st_s = _trunk_layer(hs, mod_s, cache_kv_latent[l], cache_k_rope[l], state_rwkv_shift[l],
                                state_rwkv_wkv[l], state_conv[l], lw, fg, final)
        new_p.append(st_p)
        new_s.append(st_s)
    stack = lambda states, i: jnp.stack([s[i] for s in states], axis=0)
    return ((hp, hs) + tuple(stack(new_p, i) for i in range(5)) + tuple(stack(new_s, i) for i in range(5)))
```

```python
import functools

import numpy as np
import jax
import jax.numpy as jnp
from jax import lax
from jax.experimental import pallas as pl
from jax.experimental.pallas import tpu as pltpu

F32 = jnp.float32
BF16 = jnp.bfloat16

D_MODEL = 1024
CHUNK = 64
NORM_EPS = 1e-6
MLA_HEADS = 8
Q_LORA = 384
KV_LORA = 256
NOPE_DIM = 64
ROPE_DIM = 32
V_HEAD = 64
ROPE_THETA = 10000.0
SM_SCALE = (NOPE_DIM + ROPE_DIM) ** -0.5
LOG2_E = 1.4426950408889634
RWKV_HEADS = 4
RWKV_HEAD = 64
RWKV_DIM = RWKV_HEADS * RWKV_HEAD
W_LORA = 64
A_LORA = 64
G_LORA = 128
RWKV_IN = 3 * RWKV_DIM + W_LORA + A_LORA + G_LORA
GN_EPS = 64e-5
CONV_DIM = 256
CONV_WIDTH = 31
LN_EPS = 1e-5
MLA_IN = Q_LORA + KV_LORA + ROPE_DIM
N_MOD = 6
N_GROUPS = 4
EXPERTS_PER_GROUP = 4
N_EXPERTS = 16
D_EXPERT = 256
WKV_CHUNK = 64
CONV_HALO = 32
ROUTER_LANES = 128
PAIR_KEYS = 4 * 256
VT_ROWS = KV_LORA + 16
MOE_EPS = 4
QK_AHEAD = 8
IN_PROJ_ROWS = 1024
MERGE_ROWS = 512
MOE_ROWS = 1024
MOE_ROW_BLOCK = 256
CHUNK_SHIFT = 6
HEAD_SHIFT = 6
assert 1 << CHUNK_SHIFT == CHUNK and 1 << HEAD_SHIFT == RWKV_HEAD
VMEM_LIMIT = 56 * 1024 * 1024


def _cp(sem):
    return pltpu.CompilerParams(dimension_semantics=sem, vmem_limit_bytes=VMEM_LIMIT)


def _full(a):
    nd = a.ndim
    return pl.BlockSpec(a.shape, lambda *_: (0,) * nd)


def _dot(a, b, dims=(((1,), (0,)), ((), ()))):
    return lax.dot_general(a.astype(BF16), b.astype(BF16), dims, preferred_element_type=F32)


def _split(a):
    hi = a.astype(BF16)
    lo = (a - hi.astype(F32)).astype(BF16)
    return hi, lo


def _dot_hl(a, b, dims=(((1,), (0,)), ((), ()))):
    hi, lo = _split(a)
    bb = b.astype(BF16)
    return (lax.dot_general(hi, bb, dims, preferred_element_type=F32)
            + lax.dot_general(lo, bb, dims, preferred_element_type=F32))


def _dot_split(a, b):
    m, n = a.shape[0], b.shape[1]
    ah, al = _split(a)
    bh, bl = _split(b)
    p = jnp.dot(jnp.concatenate([ah, al], axis=0), jnp.concatenate([bh, bl], axis=1), preferred_element_type=F32)
    return (p[0:m, 0:n] + p[0:m, n:2 * n]) + (p[m:2 * m, 0:n] + p[m:2 * m, n:2 * n])


def _dot_split_nt(a, b):
    n, m = a.shape[0], b.shape[0]
    ah, al = _split(a)
    bh, bl = _split(b)
    p = lax.dot_general(jnp.concatenate([ah, al], axis=0), jnp.concatenate([bh, bl], axis=0),
                        (((1,), (1,)), ((), ())), preferred_element_type=F32)
    return (p[0:n, 0:m] + p[0:n, m:2 * m]) + (p[n:2 * n, 0:m] + p[n:2 * n, m:2 * m])


def _sigmoid(x):
    return 0.5 * jnp.tanh(0.5 * x) + 0.5


def _rms(x, g):
    return x * lax.rsqrt(jnp.mean(x * x, axis=-1, keepdims=True) + NORM_EPS) * g


def _ada_kernel(c_ref, w_ref, b_ref, o_ref):
    c = c_ref[...]
    o_ref[0] = _dot(c * _sigmoid(c), w_ref[0]) + b_ref[0]


def _ada_mod(c_all, ada_w, ada_b):
    L, D, C = ada_w.shape
    R = c_all.shape[0]
    nb = C // D
    return pl.pallas_call(
        _ada_kernel,
        grid=(L, nb),
        in_specs=[pl.BlockSpec((R, D), lambda l, j: (0, 0)),
                  pl.BlockSpec((1, D, D), lambda l, j: (l, 0, j)),
                  pl.BlockSpec((1, 1, D), lambda l, j: (l, 0, j))],
        out_specs=pl.BlockSpec((1, R, D), lambda l, j: (l, 0, j)),
        out_shape=jax.ShapeDtypeStruct((L, R, C), F32),
        compiler_params=_cp(("parallel", "parallel")),
        name="ada_mod",
    )(c_all, ada_w, ada_b.reshape(L, 1, C))


_NT = (((1,), (1,)), ((), ()))


def _pair_keys(lat, rope8, wukq):
    lane = lax.broadcasted_iota(jnp.int32, rope8.shape, 1) & 127
    rope_slot = jnp.where((lane >= NOPE_DIM) & (lane < NOPE_DIM + ROPE_DIM), rope8, 0.0)
    return (_dot(lat, wukq) + jnp.concatenate([rope_slot] * (MLA_HEADS // 2), axis=1)).astype(BF16)


def _in_proj_kernel(x_ref, mod_ref, g1_ref, wq_ref, wkv_ref, wkvt_ref, wkr_ref, wrw_ref, wcv_ref, qg_ref, wqnt_ref,
                    wqrt_ref, wqrst_ref, wukq_ref, kvg_ref, kvgc_ref, cos_ref, sin_ref, cost_ref, sint_ref,
                    lat_ref, kr_ref, kq_ref, latt_ref, qt_ref, prw_ref, glu_ref):
    x = x_ref[0]
    mod = mod_ref[0]
    tm = x.shape[0]
    h = _rms(x, g1_ref[...]) * (1.0 + mod[1:2]) + mod[0:1]
    hb = h.astype(BF16)
    qn = _rms(_dot(hb, wq_ref[...]), qg_ref[...]).astype(BF16)
    qnt = _dot(wqnt_ref[...], qn, _NT)
    qrt = _dot(wqrt_ref[...], qn, _NT)
    qrst = _dot(wqrst_ref[...], qn, _NT)
    qrope = qrt * cost_ref[...] + qrst * sint_ref[...]
    zpad = jnp.zeros((128 - NOPE_DIM - ROPE_DIM, tm), F32)
    for hd in range(MLA_HEADS):
        qh = jnp.concatenate([qnt[NOPE_DIM * hd:NOPE_DIM * (hd + 1)], qrope[ROPE_DIM * hd:ROPE_DIM * (hd + 1)],
                              zpad], axis=0)
        qt_ref[0, hd] = (qh * (SM_SCALE * LOG2_E)).astype(BF16)
    lat = _rms(_dot(hb, wkv_ref[...]), kvg_ref[...])
    lat_ref[0] = lat
    if tm % 128 == 0:
        latt = lat.T
    else:
        pkvt = _dot(wkvt_ref[...], hb, _NT)
        latt = pkvt * lax.rsqrt(jnp.mean(pkvt * pkvt, axis=0, keepdims=True) + NORM_EPS) * kvgc_ref[...]
    latt_ref[0, 0:KV_LORA, :] = latt.astype(BF16)
    latt_ref[0, KV_LORA:VT_ROWS, :] = jnp.ones((VT_ROWS - KV_LORA, tm), BF16)
    pkr = _dot(hb, wkr_ref[...])
    krt = pkr[:, 0:256] * cos_ref[...] + pkr[:, 256:512] * sin_ref[...]
    kr_ref[0] = krt[:, 0:ROPE_DIM]
    kq_ref[0] = _pair_keys(lat, krt, wukq_ref[...])
    prw_ref[0] = _dot(hb, wrw_ref[...])
    pcv = _dot(hb, wcv_ref[...])
    glu_ref[0] = pcv[:, 0:CONV_DIM] * _sigmoid(pcv[:, CONV_DIM:2 * CONV_DIM])


def _in_proj(x, mod, lw, tabs, tm):
    B, T, D = x.shape
    nT = T // tm
    cos8, sin8, cos8t, sin8t = tabs
    row = lambda w: pl.BlockSpec((1, tm, w), lambda b, i: (b, i, 0))
    col = pl.BlockSpec((1, VT_ROWS, tm), lambda b, i: (b, 0, i))
    args = (lw['g1'], lw['wq'], lw['wkv'], lw['wkvt'], lw['wkr'], lw['wrw'], lw['wcv'], lw['qg'], lw['wqnt'],
            lw['wqrt'], lw['wqrst'], lw['wukq'], lw['kvg'], lw['kvgc'])
    return pl.pallas_call(
        _in_proj_kernel,
        grid=(B, nT),
        in_specs=[row(D), pl.BlockSpec((1, N_MOD, D), lambda b, i: (b, 0, 0))] + [_full(a) for a in args]
                 + [pl.BlockSpec((tm, 256), lambda b, i: (i, 0))] * 2
                 + [pl.BlockSpec((256, tm), lambda b, i: (0, i))] * 2,
        out_specs=[row(KV_LORA), row(ROPE_DIM), row(PAIR_KEYS), col,
                   pl.BlockSpec((1, MLA_HEADS, 128, tm), lambda b, i: (b, 0, 0, i)),
                   row(RWKV_IN), row(CONV_DIM)],
        out_shape=[jax.ShapeDtypeStruct((B, T, KV_LORA), F32),
                   jax.ShapeDtypeStruct((B, T, ROPE_DIM), F32),
                   jax.ShapeDtypeStruct((B, T, PAIR_KEYS), BF16),
                   jax.ShapeDtypeStruct((B, VT_ROWS, T), BF16),
                   jax.ShapeDtypeStruct((B, MLA_HEADS, 128, T), BF16),
                   jax.ShapeDtypeStruct((B, T, RWKV_IN), F32),
                   jax.ShapeDtypeStruct((B, T, CONV_DIM), F32)],
        compiler_params=_cp(("parallel", "parallel")),
        name="in_proj",
    )(x, mod, *args, cos8, sin8, cos8t, sin8t)


def _key_up_kernel(lat_ref, rope8_ref, wukq_ref, kq_ref):
    kq_ref[0] = _pair_keys(lat_ref[0], rope8_ref[0], wukq_ref[...])


def _key_up(lat, rope8, wukq, tm):
    B, S, _ = lat.shape
    row = lambda w: pl.BlockSpec((1, tm, w), lambda b, i: (b, i, 0))
    return pl.pallas_call(
        _key_up_kernel,
        grid=(B, S // tm),
        in_specs=[row(KV_LORA), row(256), _full(wukq)],
        out_specs=row(PAIR_KEYS),
        out_shape=jax.ShapeDtypeStruct((B, S, PAIR_KEYS), BF16),
        compiler_params=_cp(("parallel", "parallel")),
        name="key_up",
    )(lat, rope8, wukq)


def _attn_kernel(qi_ref, ki_ref, last_ref, qt_ref, k_ref, vt_ref, wuvt_ref, pa_ref, o_ref,
                 qcat, m_s, acc, *, tq, tk, past, n_keys):
    p = pl.program_id(1)
    qi = qi_ref[p]
    ki = ki_ref[p]
    H = MLA_HEADS
    NP = H // 2
    nh = tq // 128
    NC = NP * nh
    diag_aligned = tq == tk and past % tk == 0

    @pl.when(ki == 0)
    def _():
        qcat[...] = jnp.zeros(qcat.shape, BF16)
        for pr in range(NP):
            for hf in range(nh):
                pos = slice(128 * hf, 128 * (hf + 1))
                qcat[pr * nh + hf, 0:128, 0:128] = qt_ref[0, 2 * pr, :, pos]
                qcat[pr * nh + hf, 128:256, 128:256] = qt_ref[0, 2 * pr + 1, :, pos]
        m_s[...] = jnp.full(m_s.shape, -jnp.inf, F32)
        acc[...] = jnp.zeros(acc.shape, F32)

    def step(bias):
        def nkeys(ch):
            return 128 * (ch % nh + 1) if (bias is not None and diag_aligned) else tk

        scores = {}

        def qk(ch):
            pr = ch // nh
            scores[ch] = jnp.dot(k_ref[0, 0:nkeys(ch), 256 * pr:256 * (pr + 1)], qcat[ch],
                                 preferred_element_type=F32)

        for ch in range(min(QK_AHEAD, NC)):
            qk(ch)
        for ch in range(NC):
            if ch + QK_AHEAD < NC:
                qk(ch + QK_AHEAD)
            s = scores.pop(ch)
            if bias is not None:
                s = s + bias[ch % nh][0:nkeys(ch), :]
            m_prev = m_s[ch]
            m_new = jnp.maximum(m_prev, jnp.max(s, axis=0, keepdims=True))
            alpha = jnp.exp2(m_prev - m_new)
            pe = jnp.exp2(s - m_new)
            acc[ch] = alpha * acc[ch] + jnp.dot(vt_ref[0, :, 0:nkeys(ch)], pe.astype(BF16),
                                                preferred_element_type=F32)
            m_s[ch] = m_new

    is_last = last_ref[p] == 1

    @pl.when(jnp.logical_not(is_last))
    def _():
        step(None)

    @pl.when(is_last)
    def _():
        kpos = ki * tk + lax.broadcasted_iota(jnp.int32, (tk, 1), 0)
        biases = []
        for hf in range(nh):
            t = 128 * hf + lax.broadcasted_iota(jnp.int32, (1, 128), 1)
            limit = jnp.minimum((((past + qi * tq + t) >> CHUNK_SHIFT) + 1) << CHUNK_SHIFT, n_keys)
            bias = jnp.where(kpos < limit, 0.0, -jnp.inf)
            biases.append(jnp.concatenate([bias, bias], axis=1))
        step(biases)
        heads = []
        for pr in range(NP):
            o = [acc[pr * nh + hf, 0:KV_LORA, :] / acc[pr * nh + hf, KV_LORA:KV_LORA + 1, :]
                 for hf in range(nh)]
            for j in range(2):
                oh = jnp.concatenate([x[:, 128 * j:128 * (j + 1)] for x in o], axis=1)
                heads.append(_dot(wuvt_ref[2 * pr + j], oh))
        o_ref[0] = _dot(jnp.concatenate(heads, axis=0), pa_ref[...], (((0,), (0,)), ((), ())))


def _attn_tiles(T, past):
    n_keys = past + T
    tq = 512 if T % 512 == 0 else 128
    t_pad = -(-T // tq) * tq
    tk = -(-n_keys // 128) * 128 if n_keys <= 2304 else 512
    s_pad = -(-n_keys // tk) * tk
    return tq, t_pad, tk, s_pad, n_keys


def _attention(qt, keys, latt, wuvt, proj_a, past, T):
    B, H = qt.shape[:2]
    tq, t_pad, tk, s_pad, n_keys = _attn_tiles(T, past)
    assert keys.shape[1] == s_pad and latt.shape[2] == s_pad and qt.shape[3] == t_pad and past % CHUNK == 0
    qi, ki, last = [], [], []
    for i in range(t_pad // tq):
        lim = min(((past + (i + 1) * tq - 1) // CHUNK + 1) * CHUNK, n_keys)
        nk = -(-lim // tk)
        for j in range(nk):
            qi.append(i), ki.append(j), last.append(int(j == nk - 1))
    npairs = len(qi)
    tabs = [jnp.asarray(np.asarray(a, np.int32)) for a in (qi, ki, last)]
    nc = (H // 2) * (tq // 128)
    grid_spec = pltpu.PrefetchScalarGridSpec(
        num_scalar_prefetch=3,
        grid=(B, npairs),
        in_specs=[pl.BlockSpec((1, H, 128, tq), lambda b, p, qi, ki, la: (b, 0, 0, qi[p])),
                  pl.BlockSpec((1, tk, PAIR_KEYS), lambda b, p, qi, ki, la: (b, ki[p], 0)),
                  pl.BlockSpec((1, VT_ROWS, tk), lambda b, p, qi, ki, la: (b, 0, ki[p])),
                  pl.BlockSpec(wuvt.shape, lambda b, p, qi, ki, la: (0, 0, 0)),
                  pl.BlockSpec(proj_a.shape, lambda b, p, qi, ki, la: (0, 0))],
        out_specs=pl.BlockSpec((1, tq, D_MODEL), lambda b, p, qi, ki, la: (b, qi[p], 0)),
        scratch_shapes=[pltpu.VMEM((nc, 256, 256), BF16), pltpu.VMEM((nc, 1, 256), F32),
                        pltpu.VMEM((nc, VT_ROWS, 256), F32)])
    return pl.pallas_call(
        functools.partial(_attn_kernel, tq=tq, tk=tk, past=past, n_keys=n_keys),
        grid_spec=grid_spec,
        out_shape=jax.ShapeDtypeStruct((B, t_pad, D_MODEL), F32),
        compiler_params=_cp(("parallel", "arbitrary")),
        name="attention",
    )(*tabs, qt, keys, latt, wuvt, proj_a)


def _head_sum(x, ones_bd):
    return _dot_hl(x, ones_bd)


def _rwkv_features(pr, prev, wts):
    mu, w0, w2p, a0, a2p, g2, k_k, k_a, r_k, ones_bd = wts
    xs = pr + (prev - pr) * mu
    r = xs[:, 0:256]
    k = xs[:, 256:512]
    v = xs[:, 512:768]
    wa = xs[:, 768:896]
    gl = xs[:, 896:1024]
    nz = -(w0 + _dot(jnp.tanh(wa), w2p))
    softplus = jnp.maximum(nz, 0.0) + jnp.log(1.0 + jnp.exp(-jnp.abs(nz)))
    lw = -jnp.exp(-softplus - 0.5)
    a = _sigmoid(a0 + _dot(wa, a2p))
    g = _dot(_sigmoid(gl), g2)
    kk = k * k_k
    kk = kk * lax.rsqrt(_head_sum(kk * kk, ones_bd) + 1e-12)
    kf = k * (1.0 + (a - 1.0) * k_a)
    bonus = _head_sum(r * kf * r_k, ones_bd) * v
    return r, kf, kk * a, kk, v, lw, g, bonus


def _wkv_kernel(prw_ref, shift_ref, s0_ref, mu_ref, w0_ref, w2_ref, a0_ref, a2_ref, g2_ref, kk_ref, ka_ref, rk_ref,
                ones_ref, lng_ref, lnb_ref, yb_ref, sout_ref, S, carry, *, nb, G, n_valid):
    L = WKV_CHUNK
    W = RWKV_DIM

    @pl.when(pl.program_id(0) == 0)
    def _():
        S[...] = s0_ref[...]
        carry[...] = shift_ref[...]

    wts = tuple(ref[...] for ref in (mu_ref, w0_ref, w2_ref, a0_ref, a2_ref, g2_ref, kk_ref, ka_ref, rk_ref, ones_ref))
    ones_bd = wts[-1]
    ln_g = lng_ref[...]
    ln_b = lnb_ref[...]
    ri = lax.broadcasted_iota(jnp.int32, (W, W), 0)
    ci = lax.broadcasted_iota(jnp.int32, (W, W), 1)
    same = (ri >> HEAD_SHIFT) == (ci >> HEAD_SHIFT)
    in_head = RWKV_HEAD - 1
    strict = same & ((ci & in_head) < (ri & in_head))
    incl = same & ((ci & in_head) <= (ri & in_head))
    eye = (ri == ci).astype(F32)
    tri = (lax.broadcasted_iota(jnp.int32, (L, L), 1) <= lax.broadcasted_iota(jnp.int32, (L, L), 0)).astype(BF16)
    lane_head = lax.broadcasted_iota(jnp.int32, (L, W), 1) >> HEAD_SHIFT
    row = lax.broadcasted_iota(jnp.int32, (L, 1), 0)
    nt = (((1,), (1,)), ((), ()))
    tn = (((0,), (0,)), ((), ()))

    def stack(x):
        return jnp.concatenate([jnp.where(lane_head == hd, x, 0.0) for hd in range(RWKV_HEADS)], axis=0)

    def collapse(z):
        return z[0:L] + z[L:2 * L] + z[2 * L:3 * L] + z[3 * L:4 * L]

    def features(i):
        pr = prw_ref[i]
        prev = jnp.where(row == 0, carry[i], pltpu.roll(pr, 1, 0))
        carry[i] = pr[L - 1:L, :]
        f = _rwkv_features(pr, prev, wts)
        if n_valid < L:
            f = tuple(jnp.where(row < n_valid, x, 0.0) for x in f[:6]) + f[6:]
        return f

    def body(it, _):
        ids = [it * G + g for g in range(G)]
        each = lambda f, *xs: [f(*a) for a in zip(*xs)]
        r, kf, bb, kk, v, ll, gate, bonus = zip(*[features(i) for i in ids])
        s_prev = [S[i] for i in ids]
        split = each(_split, ll)
        c = each(lambda hl: jnp.dot(tri, hl[0], preferred_element_type=F32)
                 + jnp.dot(tri, hl[1], preferred_element_type=F32), split)
        c_last = each(lambda x: x[L - 1:L, :], c)
        e_neg = each(lambda x: jnp.exp(-x), c)
        e_end = each(lambda x, y: jnp.exp(y - x), c, c_last)
        rt = each(lambda x, y: stack(x * jnp.exp(y)), r, c)
        kkt = each(lambda x, y, z: stack(x * jnp.exp(y - z)), kk, c, ll)
        bt = each(lambda x, y: stack(x * y), bb, e_neg)
        kt = each(lambda x, y: stack(x * y), kf, e_neg)
        a_m = each(lambda x, y: jnp.where(strict, _dot(x, y, nt), 0.0), kkt, bt)
        b_m = each(lambda x, y: jnp.where(strict, _dot(x, y, nt), 0.0), kkt, kt)
        m1 = each(lambda x, y: jnp.where(incl, _dot(x, y, nt), 0.0), rt, bt)
        m2 = each(lambda x, y: jnp.where(incl, _dot(x, y, nt), 0.0), rt, kt)
        t_m = each(lambda x: eye - x, a_m)
        a_p = a_m
        for _ in range(5):
            a_p = each(lambda x: _dot(x, x), a_p)
            t_m = each(lambda x, y: x + _dot(x, y), t_m, a_p)
        vs = each(stack, v)
        w1 = each(lambda x, y: collapse(_dot(x, y)), t_m, kkt)
        bv = each(_dot, b_m, vs)
        w2 = each(lambda x, y: collapse(_dot(x, y)), t_m, bv)
        u = each(lambda x, y, z: -(_dot(x, y, nt) + z), w1, s_prev, w2)
        y0 = each(lambda x, y: _dot(collapse(x), y, nt), rt, s_prev)
        y1 = each(lambda a, b_, c_, d: collapse(_dot(a, stack(b_)) + _dot(c_, d)), m1, u, m2, vs)
        upd = each(lambda a, b_, c_, d, e: _dot(a, b_ * e, tn) + _dot(c_, d * e, tn), u, bb, v, kf, e_end)
        for g, i in enumerate(ids):
            S[i] = s_prev[g] * jnp.exp(c_last[g]) + jnp.where(same, upd[g], 0.0)
            y = y0[g] + y1[g]
            mean = _head_sum(y, ones_bd) * (1.0 / RWKV_HEAD)
            yc = y - mean
            var = _head_sum(yc * yc, ones_bd) * (1.0 / RWKV_HEAD)
            yb = (yc * lax.rsqrt(var + GN_EPS) * ln_g + ln_b + bonus[g]) * gate[g]
            yb_ref[i] = yb.astype(BF16)
        return 0

    lax.fori_loop(0, nb // G, body, 0)

    @pl.when(pl.program_id(0) == pl.num_programs(0) - 1)
    def _():
        sout_ref[...] = S[...]


def _wkv(prw, shift0, s0, lw, n_valid):
    B, T, _ = prw.shape
    L = WKV_CHUNK
    W = RWKV_DIM
    assert T % L == 0 and (n_valid == L or T == L)
    args = (lw['mu'], lw['w0'], lw['w2p'], lw['a0'], lw['a2p'], lw['g2'], lw['k_k'], lw['k_a'], lw['r_k'],
            lw['ones_bd'], lw['ln_g'], lw['ln_b'])
    st = pl.BlockSpec((B, W, W), lambda c: (0, 0, 0))
    return pl.pallas_call(
        functools.partial(_wkv_kernel, nb=B, G=8 if B % 8 == 0 else 1, n_valid=n_valid),
        grid=(T // L,),
        in_specs=[pl.BlockSpec((B, L, RWKV_IN), lambda c: (0, c, 0)), _full(shift0), st] + [_full(a) for a in args],
        out_specs=[pl.BlockSpec((B, L, W), lambda c: (0, c, 0)), st],
        out_shape=[jax.ShapeDtypeStruct((B, T, W), BF16), jax.ShapeDtypeStruct((B, W, W), F32)],
        scratch_shapes=[pltpu.VMEM((B, W, W), F32), pltpu.VMEM((B, 1, RWKV_IN), F32)],
        compiler_params=_cp(("arbitrary",)),
        name="wkv",
    )(prw, shift0, s0, *args)


def _conv_stage(ext, glu, st_ref, first, tm):
    @pl.when(first)
    def _():
        ext[0:CONV_HALO, :] = st_ref[0]
    ext[CONV_HALO:CONV_HALO + tm, :] = glu


def _conv_module(ext, dw_ref, dwb, lng, lnb, new_ref, tm):
    off = CONV_HALO - (CONV_WIDTH - 1)
    ext_v = ext[...]
    n = CONV_HALO + tm
    acc = jnp.zeros((tm, CONV_DIM), F32)
    for res in range(8):
        taps = [w for w in range(CONV_WIDTH) if (off + w) % 8 == res]
        if not taps:
            continue
        sh = ext_v if res == 0 else pltpu.roll(ext_v, n - res, 0)
        for w in taps:
            base = off + w - res
            acc = acc + sh[base:base + tm, :] * dw_ref[w:w + 1, :]
    y = acc + dwb
    mu = jnp.mean(y, axis=-1, keepdims=True)
    var = jnp.mean(jnp.square(y - mu), axis=-1, keepdims=True)
    y = (y - mu) * lax.rsqrt(var + LN_EPS) * lng + lnb
    tail = ext[tm:tm + CONV_HALO, :]
    new_ref[0] = tail
    ext[0:CONV_HALO, :] = tail
    return y * _sigmoid(y)


def _route(logits, ax):
    idx = lax.broadcasted_iota(jnp.int32, logits.shape, ax)
    big = 4 * ROUTER_LANES
    neg = -jnp.inf
    rmax = lambda x: jnp.max(x, axis=ax, keepdims=True)
    rmin = lambda x: jnp.min(x, axis=ax, keepdims=True)
    is_g = (idx >= N_EXPERTS) & (idx < N_EXPERTS + N_GROUPS)
    gl = jnp.where(is_g, logits, neg)
    gmax = rmax(gl)
    g_val = 1.0 / jnp.sum(jnp.exp(gl - gmax), axis=ax, keepdims=True)
    g_idx = rmin(jnp.where(is_g & (gl == gmax), idx, big)) - N_EXPERTS
    in_grp = (idx >= g_idx * EXPERTS_PER_GROUP) & (idx < (g_idx + 1) * EXPERTS_PER_GROUP)
    el = jnp.where(in_grp, logits, neg)
    v1 = rmax(el)
    i1 = rmin(jnp.where(in_grp & (el == v1), idx, big))
    el2 = jnp.where(idx == i1, neg, el)
    v2 = rmax(el2)
    i2 = rmin(jnp.where(in_grp & (idx != i1) & (el2 == v2), idx, big))
    e2 = jnp.exp(v2 - v1)
    w1 = g_val / (1.0 + e2)
    w2 = g_val * e2 / (1.0 + e2)
    return jnp.where(idx == i1, w1, 0.0) + jnp.where(idx == i2, w2, 0.0)


def _merge_kernel(x_ref, mod_ref, g1_ref, wg_ref, oa_ref, yb_ref, pb_ref, glu_ref, st_ref, dw_ref, dwb_ref, clg_ref,
                  clb_ref, pc_ref, wo_ref, g2_ref, wr_ref, br_ref, wrt_ref, brt_ref, x1_ref, h2_ref, comb_ref, cnew_ref,
                  ext):
    x = x_ref[0]
    mod = mod_ref[0]
    tm = x.shape[0]
    _conv_stage(ext, glu_ref[0], st_ref, pl.program_id(1) == 0, tm)
    yc = _conv_module(ext, dw_ref, dwb_ref[...], clg_ref[...], clb_ref[...], cnew_ref, tm)
    hb = (_rms(x, g1_ref[...]) * (1.0 + mod[1:2]) + mod[0:1]).astype(BF16)
    D = D_MODEL
    merged = _sigmoid(_dot(hb, wg_ref[:, 0:D])) * oa_ref[0]
    merged = merged + _sigmoid(_dot(hb, wg_ref[:, D:2 * D])) * _dot(yb_ref[0], pb_ref[...])
    merged = merged + _sigmoid(_dot(hb, wg_ref[:, 2 * D:3 * D])) * _dot(yc, pc_ref[...])
    x1 = x + mod[2:3] * _dot(merged, wo_ref[...])
    x1_ref[0] = x1
    h2 = _rms(x1, g2_ref[...]) * (1.0 + mod[4:5]) + mod[3:4]
    h2_ref[0] = h2.astype(BF16)
    if tm % 128 == 0:
        comb_ref[0] = _route(_dot_split_nt(wrt_ref[...], h2) + brt_ref[...], 0).T
    else:
        comb_ref[0] = _route(_dot_split(h2, wr_ref[...]) + br_ref[...], 1)


def _merge(x, mod, oa, yb, glu, st_pad, lw, tm):
    B, T, D = x.shape
    row = lambda w: pl.BlockSpec((1, tm, w), lambda b, i: (b, i, 0))
    modspec = pl.BlockSpec((1, N_MOD, D), lambda b, i: (b, 0, 0))
    halo = pl.BlockSpec((1, CONV_HALO, CONV_DIM), lambda b, i: (b, 0, 0))
    ins = [(x, row(D)), (mod, modspec), (lw['g1'], None), (lw['wg'], None), (oa, row(D)), (yb, row(256)),
           (lw['proj_b'], None), (glu, row(CONV_DIM)), (st_pad, halo), (lw['conv_dw'], None),
           (lw['conv_dw_b'], None), (lw['conv_ln_g'], None), (lw['conv_ln_b'], None), (lw['proj_c'], None),
           (lw['w_out'], None), (lw['g2n'], None), (lw['wr'], None), (lw['br'], None), (lw['wrt'], None),
           (lw['brt'], None)]
    return pl.pallas_call(
        _merge_kernel,
        grid=(B, T // tm),
        in_specs=[s if s is not None else _full(a) for a, s in ins],
        out_specs=[row(D), row(D), row(ROUTER_LANES), halo],
        out_shape=[jax.ShapeDtypeStruct((B, T, D), F32), jax.ShapeDtypeStruct((B, T, D), BF16),
                   jax.ShapeDtypeStruct((B, T, ROUTER_LANES), F32),
                   jax.ShapeDtypeStruct((B, CONV_HALO, CONV_DIM), F32)],
        scratch_shapes=[pltpu.VMEM((CONV_HALO + tm, CONV_DIM), F32)],
        compiler_params=_cp(("parallel", "arbitrary")),
        name="merge",
    )(*[a for a, _ in ins])


def _moe_kernel(h2_ref, comb_ref, x1_ref, gt_ref, wgu_ref, wd_ref, fg_ref, o_ref, acc, *, final, rb):
    s = pl.program_id(2)
    tm = h2_ref.shape[1]
    nrb = tm // rb
    DE = D_EXPERT

    @pl.when(s == 0)
    def _():
        acc[...] = jnp.zeros_like(acc)

    wgu = wgu_ref[0]
    wd = wd_ref[0]
    gus = {}

    def up(i):
        gus[i] = _dot(h2_ref[0, i * rb:(i + 1) * rb, :], wgu)

    up(0)
    for i in range(nrb):
        if i + 1 < nrb:
            up(i + 1)
        gu = gus.pop(i)
        rows = slice(i * rb, (i + 1) * rb)
        comb = comb_ref[0, rows, :]
        lane = lax.broadcasted_iota(jnp.int32, comb.shape, 1)
        cws = [jnp.sum(jnp.where(lane == MOE_EPS * s + j, comb, 0.0), axis=1, keepdims=True) for j in range(MOE_EPS)]
        cw = jnp.concatenate([jnp.broadcast_to(c, (rb, DE)) for c in cws], axis=1)
        gate = gu[:, 0:MOE_EPS * DE]
        act = gate * _sigmoid(gate) * gu[:, MOE_EPS * DE:2 * MOE_EPS * DE] * cw
        acc[rows, :] += _dot(act, wd)

    @pl.when(s == N_EXPERTS // MOE_EPS - 1)
    def _():
        x2 = x1_ref[0] + gt_ref[0] * acc[...]
        o_ref[0] = _rms(x2, fg_ref[...]) if final else x2


def _moe(h2, comb, x1, gt, lw, final_g, tm, final):
    B, T, D = x1.shape
    gt_rows = gt.shape[1]
    gt_spec = (pl.BlockSpec((1, 1, D), lambda b, i, e: (b, 0, 0)) if gt_rows == 1
               else pl.BlockSpec((1, tm, D), lambda b, i, e: (b, i, 0)))
    row = lambda w: pl.BlockSpec((1, tm, w), lambda b, i, e: (b, i, 0))
    return pl.pallas_call(
        functools.partial(_moe_kernel, final=final, rb=min(tm, MOE_ROW_BLOCK)),
        grid=(B, T // tm, N_EXPERTS // MOE_EPS),
        in_specs=[row(D), row(ROUTER_LANES), row(D), gt_spec,
                  pl.BlockSpec((1, D, 2 * MOE_EPS * D_EXPERT), lambda b, i, e: (e, 0, 0)),
                  pl.BlockSpec((1, MOE_EPS * D_EXPERT, D), lambda b, i, e: (e, 0, 0)),
                  pl.BlockSpec((1, D), lambda b, i, e: (0, 0))],
        out_specs=row(D),
        out_shape=jax.ShapeDtypeStruct((B, T, D), F32),
        scratch_shapes=[pltpu.VMEM((tm, D), F32)],
        compiler_params=_cp(("parallel", "parallel", "arbitrary")),
        name="moe",
    )(h2, comb, x1, gt, lw['wgu'], lw['wd'], final_g)


def _layer_weights(l, w):
    D = D_MODEL
    w_in = w['w_in'][l]
    c_q, c_kv, c_kr = Q_LORA, Q_LORA + KV_LORA, MLA_IN
    c_rw, c_cv = MLA_IN + RWKV_IN, MLA_IN + RWKV_IN + 2 * CONV_DIM
    half = ROPE_DIM // 2
    swap = np.concatenate([np.arange(half, ROPE_DIM), np.arange(half)])
    wkr = w_in[:, c_kv:c_kr]
    q_up = w['q_up'][l].reshape(Q_LORA, MLA_HEADS, NOPE_DIM + ROPE_DIM)
    q_rope = q_up[:, :, NOPE_DIM:]
    wukq = jnp.zeros((KV_LORA, PAIR_KEYS), F32)
    for hd in range(MLA_HEADS):
        wukq = wukq.at[:, 128 * hd:128 * hd + NOPE_DIM].set(w['w_uk'][l][:, hd, :])
    zl = jnp.zeros((W_LORA, RWKV_DIM), F32)
    head = np.arange(RWKV_DIM) // RWKV_HEAD
    wr = jnp.zeros((D, ROUTER_LANES), F32)
    wr = wr.at[:, 0:N_EXPERTS].set(w['router_expert_w'][l]).at[:, N_EXPERTS:N_EXPERTS + N_GROUPS].set(
        w['router_group_w'][l])
    br = jnp.zeros((1, ROUTER_LANES), F32)
    br = br.at[0, 0:N_EXPERTS].set(w['router_expert_b'][l]).at[0, N_EXPERTS:N_EXPERTS + N_GROUPS].set(
        w['router_group_b'][l])
    r2 = lambda a: a.reshape(1, -1)
    pair_cols = lambda a: jnp.concatenate([a[j::MOE_EPS] for j in range(MOE_EPS)], axis=2)
    return dict(
        g1=r2(w['norm1_g'][l]), g2n=r2(w['norm2_g'][l]),
        wq=w_in[:, :c_q].astype(BF16), wkv=w_in[:, c_q:c_kv].astype(BF16),
        wkr=jnp.concatenate([jnp.tile(wkr, (1, 8)), jnp.tile(wkr[:, swap], (1, 8))], axis=1).astype(BF16),
        wrw=w_in[:, c_kr:c_rw].astype(BF16), wcv=w_in[:, c_rw:c_cv].astype(BF16), wg=w_in[:, c_cv:].astype(BF16),
        wkvt=w_in[:, c_q:c_kv].T.astype(BF16), kvgc=w['kv_norm_g'][l].reshape(-1, 1),
        qg=r2(w['q_norm_g'][l]), wqnt=q_up[:, :, :NOPE_DIM].reshape(Q_LORA, -1).T.astype(BF16),
        wukq=wukq.astype(BF16), kvg=r2(w['kv_norm_g'][l]),
        wqrt=q_rope.reshape(Q_LORA, -1).T.astype(BF16), wqrst=q_rope[:, :, swap].reshape(Q_LORA, -1).T.astype(BF16),
        wuvt=jnp.transpose(w['w_uv'][l], (1, 2, 0)).astype(BF16), proj_a=w['proj_a'][l].astype(BF16),
        mu=r2(w['rwkv_mu'][l]), w0=r2(w['rwkv_w0'][l]), a0=r2(w['rwkv_a0'][l]),
        w2p=jnp.concatenate([w['rwkv_w2'][l], zl], axis=0).astype(BF16),
        a2p=jnp.concatenate([zl, w['rwkv_a2'][l]], axis=0).astype(BF16),
        g2=w['rwkv_g2'][l].astype(BF16), k_k=r2(w['rwkv_k_k'][l]), k_a=r2(w['rwkv_k_a'][l]),
        r_k=r2(w['rwkv_r_k'][l]), ln_g=r2(w['rwkv_ln_g'][l]), ln_b=r2(w['rwkv_ln_b'][l]),
        ones_bd=jnp.asarray((head[:, None] == head[None, :]).astype(np.float32)).astype(BF16),
        proj_b=w['proj_b'][l].astype(BF16),
        conv_dw=w['conv_dw'][l], conv_dw_b=r2(w['conv_dw_b'][l]), conv_ln_g=r2(w['conv_ln_g'][l]),
        conv_ln_b=r2(w['conv_ln_b'][l]), proj_c=w['proj_c'][l].astype(BF16),
        w_out=w['w_out'][l].astype(BF16), wr=wr, br=br, wrt=wr.T, brt=br.T,
        wgu=jnp.concatenate([pair_cols(w['moe_w_gate'][l]), pair_cols(w['moe_w_up'][l])], axis=2).astype(BF16),
        wd=w['moe_w_down'][l].reshape(N_EXPERTS // MOE_EPS, MOE_EPS * D_EXPERT, D).astype(BF16),
    )


def _rope_tables(past, T):
    half = ROPE_DIM // 2
    inv = ROPE_THETA ** (-jnp.arange(half, dtype=F32) / half)
    ang = (past + jnp.arange(T)).astype(F32)[:, None] * inv[None, :]
    cos, sin = jnp.cos(ang), jnp.sin(ang)
    cos8 = jnp.tile(jnp.concatenate([cos, cos], axis=1), (1, 8))
    sin8 = jnp.tile(jnp.concatenate([-sin, sin], axis=1), (1, 8))
    return cos8, sin8, cos8.T, sin8.T


def _row_tile(T, cap):
    return min(T, cap)


def _trunk_layer(x, mod, cache_lat, cache_rope, shift_st, wkv_st, conv_st, lw, final_g, final):
    B, T, D = x.shape
    past = 0 if cache_lat is None else cache_lat.shape[1]
    tm = _row_tile(T, MERGE_ROWS)
    lat, kr, keys, latt, qt, prw, glu = _in_proj(x, mod, lw, _rope_tables(past, T), _row_tile(T, IN_PROJ_ROWS))

    _, t_pad, _, s_pad, n_keys = _attn_tiles(T, past)
    row_pad = ((0, 0), (0, s_pad - n_keys), (0, 0))
    if past:
        lat_all = jnp.pad(jnp.concatenate([cache_lat, lat], axis=1), row_pad)
        rope_all = jnp.pad(jnp.concatenate([cache_rope, kr], axis=1), row_pad)
        key_tile = max(d for d in range(8, 2305, 8) if s_pad % d == 0)
        keys = _key_up(lat_all, jnp.tile(rope_all, (1, 1, 8)), lw['wukq'], key_tile)
        old_t = jnp.concatenate([jnp.swapaxes(cache_lat, 1, 2).astype(BF16),
                                 jnp.ones((B, VT_ROWS - KV_LORA, past), BF16)], axis=1)
        latt = jnp.concatenate([old_t, latt], axis=2)
    elif s_pad != n_keys:
        keys = jnp.pad(keys, row_pad)
    if s_pad != n_keys:
        latt = jnp.pad(latt, ((0, 0), (0, 0), (0, s_pad - n_keys)))
    if t_pad != T:
        qt = jnp.pad(qt, ((0, 0), (0, 0), (0, 0), (0, t_pad - T)))
    o_a = _attention(qt, keys, latt, lw['wuvt'], lw['proj_a'], past, T)[:, :T]

    L = WKV_CHUNK
    Tp = -(-T // L) * L
    prw_p = prw if Tp == T else jnp.pad(prw, ((0, 0), (0, Tp - T), (0, 0)))
    shift0 = jnp.zeros((B, 1, RWKV_IN), F32) if shift_st is None else shift_st[:, None, :]
    W = RWKV_DIM
    s0 = jnp.zeros((B, W, W), F32)
    if wkv_st is not None:
        for hd in range(RWKV_HEADS):
            o = hd * RWKV_HEAD
            s0 = s0.at[:, o:o + RWKV_HEAD, o:o + RWKV_HEAD].set(wkv_st[:, hd])
    yb, s_bd = _wkv(prw_p, shift0, s0, lw, L if Tp == T else T)
    yb = yb[:, :T]
    wkv_new = jnp.stack([s_bd[:, hd * RWKV_HEAD:(hd + 1) * RWKV_HEAD, hd * RWKV_HEAD:(hd + 1) * RWKV_HEAD]
                         for hd in range(RWKV_HEADS)], axis=1)

    pad_rows = CONV_HALO - (CONV_WIDTH - 1)
    st = jnp.zeros((B, CONV_WIDTH - 1, CONV_DIM), F32) if conv_st is None else conv_st
    x1, h2, comb, conv_tail = _merge(x, mod, o_a, yb, glu, jnp.pad(st, ((0, 0), (pad_rows, 0), (0, 0))), lw, tm)
    conv_new = conv_tail[:, pad_rows:]

    if T >= MOE_ROWS:
        x2 = _moe(h2, comb, x1, mod[:, 5:6], lw, final_g, MOE_ROWS, final)
    else:
        gt = jnp.broadcast_to(mod[:, 5:6], (B, T, D)).reshape(1, B * T, D)
        x2 = _moe(h2.reshape(1, B * T, D), comb.reshape(1, B * T, ROUTER_LANES), x1.reshape(1, B * T, D), gt, lw,
                  final_g, B * T, final).reshape(B, T, D)
    return x2, (lat, kr, prw[:, -1], wkv_new, conv_new)


def kernel(x_prompt, x_sample, c_prompt, c_sample, cache_kv_latent, cache_k_rope, state_rwkv_shift, state_rwkv_wkv, state_conv, ada_w, ada_b, norm1_g, norm2_g, w_in, q_norm_g, q_up, kv_norm_g, w_uk, w_uv, proj_a, rwkv_mu, rwkv_w0, rwkv_w2, rwkv_a0, rwkv_a2, rwkv_g2, rwkv_k_k, rwkv_k_a, rwkv_r_k, rwkv_ln_g, rwkv_ln_b, proj_b, conv_dw, conv_dw_b, conv_ln_g, conv_ln_b, proj_c, w_out, router_group_w, router_group_b, router_expert_w, router_expert_b, moe_w_gate, moe_w_up, moe_w_down, final_g):
    w = dict(norm1_g=norm1_g, norm2_g=norm2_g, w_in=w_in, q_norm_g=q_norm_g, q_up=q_up, kv_norm_g=kv_norm_g,
             w_uk=w_uk, w_uv=w_uv, proj_a=proj_a, rwkv_mu=rwkv_mu, rwkv_w0=rwkv_w0, rwkv_w2=rwkv_w2,
             rwkv_a0=rwkv_a0, rwkv_a2=rwkv_a2, rwkv_g2=rwkv_g2, rwkv_k_k=rwkv_k_k, rwkv_k_a=rwkv_k_a,
             rwkv_r_k=rwkv_r_k.reshape(rwkv_r_k.shape[0], -1), rwkv_ln_g=rwkv_ln_g, rwkv_ln_b=rwkv_ln_b,
             proj_b=proj_b, conv_dw=conv_dw, conv_dw_b=conv_dw_b, conv_ln_g=conv_ln_g, conv_ln_b=conv_ln_b,
             proj_c=proj_c, w_out=w_out, router_group_w=router_group_w, router_group_b=router_group_b,
             router_expert_w=router_expert_w, router_expert_b=router_expert_b, moe_w_gate=moe_w_gate,
             moe_w_up=moe_w_up, moe_w_down=moe_w_down)
    depth = ada_w.shape[0]
    bp = x_prompt.shape[0]
    D = D_MODEL
    mod_all = _ada_mod(jnp.concatenate([c_prompt, c_sample], axis=0), ada_w, ada_b)
    fg = final_g.reshape(1, D)
    hp, hs = x_prompt, x_sample
    new_p, new_s = [], []
    for l in range(depth):
        lw = _layer_weights(l, w)
        mod_p = mod_all[l, :bp].reshape(bp, N_MOD, D)
        mod_s = mod_all[l, bp:].reshape(-1, N_MOD, D)
        final = l == depth - 1
        hp, st_p = _trunk_layer(hp, mod_p, None, None, None, None, None, lw, fg, final)
        hs, st_s = _trunk_layer(hs, mod_s, cache_kv_latent[l], cache_k_rope[l], state_rwkv_shift[l],
                                state_rwkv_wkv[l], state_conv[l], lw, fg, final)
        new_p.append(st_p)
        new_s.append(st_s)
    stack = lambda states, i: jnp.stack([s[i] for s in states], axis=0)
    return ((hp, hs) + tuple(stack(new_p, i) for i in range(5)) + tuple(stack(new_s, i) for i in range(5)))
```

```python
import functools

import numpy as np
import jax
import jax.numpy as jnp
from jax import lax
from jax.experimental import pallas as pl
from jax.experimental.pallas import tpu as pltpu

F32 = jnp.float32
BF16 = jnp.bfloat16

D_MODEL = 1024
CHUNK = 64
NORM_EPS = 1e-6
MLA_HEADS = 8
Q_LORA = 384
KV_LORA = 256
NOPE_DIM = 64
ROPE_DIM = 32
V_HEAD = 64
ROPE_THETA = 10000.0
SM_SCALE = (NOPE_DIM + ROPE_DIM) ** -0.5
LOG2_E = 1.4426950408889634
RWKV_HEADS = 4
RWKV_HEAD = 64
RWKV_DIM = RWKV_HEADS * RWKV_HEAD
W_LORA = 64
A_LORA = 64
G_LORA = 128
RWKV_IN = 3 * RWKV_DIM + W_LORA + A_LORA + G_LORA
GN_EPS = 64e-5
CONV_DIM = 256
CONV_WIDTH = 31
LN_EPS = 1e-5
MLA_IN = Q_LORA + KV_LORA + ROPE_DIM
N_MOD = 6
N_GROUPS = 4
EXPERTS_PER_GROUP = 4
N_EXPERTS = 16
D_EXPERT = 256
WKV_CHUNK = 64
CONV_HALO = 32
ROUTER_LANES = 128
PAIR_KEYS = 4 * 256
VT_ROWS = KV_LORA + 16
MOE_EPS = 4
QK_AHEAD = 8
IN_PROJ_ROWS = 1024
MERGE_ROWS = 512
MOE_ROWS = 1024
MOE_ROW_BLOCK = 256
CHUNK_SHIFT = 6
HEAD_SHIFT = 6
assert 1 << CHUNK_SHIFT == CHUNK and 1 << HEAD_SHIFT == RWKV_HEAD
VMEM_LIMIT = 56 * 1024 * 1024


def _cp(sem):
    return pltpu.CompilerParams(dimension_semantics=sem, vmem_limit_bytes=VMEM_LIMIT)


def _full(a):
    nd = a.ndim
    return pl.BlockSpec(a.shape, lambda *_: (0,) * nd)


def _dot(a, b, dims=(((1,), (0,)), ((), ()))):
    return lax.dot_general(a.astype(BF16), b.astype(BF16), dims, preferred_element_type=F32)


def _split(a):
    hi = a.astype(BF16)
    lo = (a - hi.astype(F32)).astype(BF16)
    return hi, lo


def _dot_hl(a, b, dims=(((1,), (0,)), ((), ()))):
    hi, lo = _split(a)
    bb = b.astype(BF16)
    return (lax.dot_general(hi, bb, dims, preferred_element_type=F32)
            + lax.dot_general(lo, bb, dims, preferred_element_type=F32))


def _dot_split(a, b):
    m, n = a.shape[0], b.shape[1]
    ah, al = _split(a)
    bh, bl = _split(b)
    p = jnp.dot(jnp.concatenate([ah, al], axis=0), jnp.concatenate([bh, bl], axis=1), preferred_element_type=F32)
    return (p[0:m, 0:n] + p[0:m, n:2 * n]) + (p[m:2 * m, 0:n] + p[m:2 * m, n:2 * n])


def _dot_split_nt(a, b):
    n, m = a.shape[0], b.shape[0]
    ah, al = _split(a)
    bh, bl = _split(b)
    p = lax.dot_general(jnp.concatenate([ah, al], axis=0), jnp.concatenate([bh, bl], axis=0),
                        (((1,), (1,)), ((), ())), preferred_element_type=F32)
    return (p[0:n, 0:m] + p[0:n, m:2 * m]) + (p[n:2 * n, 0:m] + p[n:2 * n, m:2 * m])


def _sigmoid(x):
    return 0.5 * jnp.tanh(0.5 * x) + 0.5


def _rms(x, g):
    return x * lax.rsqrt(jnp.mean(x * x, axis=-1, keepdims=True) + NORM_EPS) * g


def _ada_kernel(c_ref, w_ref, b_ref, o_ref):
    c = c_ref[...]
    o_ref[0] = _dot(c * _sigmoid(c), w_ref[0]) + b_ref[0]


def _ada_mod(c_all, ada_w, ada_b):
    L, D, C = ada_w.shape
    R = c_all.shape[0]
    nb = C // D
    return pl.pallas_call(
        _ada_kernel,
        grid=(L, nb),
        in_specs=[pl.BlockSpec((R, D), lambda l, j: (0, 0)),
                  pl.BlockSpec((1, D, D), lambda l, j: (l, 0, j)),
                  pl.BlockSpec((1, 1, D), lambda l, j: (l, 0, j))],
        out_specs=pl.BlockSpec((1, R, D), lambda l, j: (l, 0, j)),
        out_shape=jax.ShapeDtypeStruct((L, R, C), F32),
        compiler_params=_cp(("parallel", "parallel")),
        name="ada_mod",
    )(c_all, ada_w, ada_b.reshape(L, 1, C))


_NT = (((1,), (1,)), ((), ()))


def _pair_keys(lat, rope8, wukq):
    lane = lax.broadcasted_iota(jnp.int32, rope8.shape, 1) & 127
    rope_slot = jnp.where((lane >= NOPE_DIM) & (lane < NOPE_DIM + ROPE_DIM), rope8, 0.0)
    return (_dot(lat, wukq) + jnp.concatenate([rope_slot] * (MLA_HEADS // 2), axis=1)).astype(BF16)


def _in_proj_kernel(x_ref, mod_ref, g1_ref, wq_ref, wkv_ref, wkvt_ref, wkr_ref, wrw_ref, wcv_ref, qg_ref, wqnt_ref,
                    wqrt_ref, wqrst_ref, wukq_ref, kvg_ref, kvgc_ref, cos_ref, sin_ref, cost_ref, sint_ref,
                    lat_ref, kr_ref, kq_ref, latt_ref, qt_ref, prw_ref, glu_ref):
    x = x_ref[0]
    mod = mod_ref[0]
    tm = x.shape[0]
    h = _rms(x, g1_ref[...]) * (1.0 + mod[1:2]) + mod[0:1]
    hb = h.astype(BF16)
    qn = _rms(_dot(hb, wq_ref[...]), qg_ref[...]).astype(BF16)
    qnt = _dot(wqnt_ref[...], qn, _NT)
    qrt = _dot(wqrt_ref[...], qn, _NT)
    qrst = _dot(wqrst_ref[...], qn, _NT)
    qrope = qrt * cost_ref[...] + qrst * sint_ref[...]
    zpad = jnp.zeros((128 - NOPE_DIM - ROPE_DIM, tm), F32)
    for hd in range(MLA_HEADS):
        qh = jnp.concatenate([qnt[NOPE_DIM * hd:NOPE_DIM * (hd + 1)], qrope[ROPE_DIM * hd:ROPE_DIM * (hd + 1)],
                              zpad], axis=0)
        qt_ref[0, hd] = (qh * (SM_SCALE * LOG2_E)).astype(BF16)
    lat = _rms(_dot(hb, wkv_ref[...]), kvg_ref[...])
    lat_ref[0] = lat
    if tm % 128 == 0:
        latt = lat.T
    else:
        pkvt = _dot(wkvt_ref[...], hb, _NT)
        latt = pkvt * lax.rsqrt(jnp.mean(pkvt * pkvt, axis=0, keepdims=True) + NORM_EPS) * kvgc_ref[...]
    latt_ref[0, 0:KV_LORA, :] = latt.astype(BF16)
    latt_ref[0, KV_LORA:VT_ROWS, :] = jnp.ones((VT_ROWS - KV_LORA, tm), BF16)
    pkr = _dot(hb, wkr_ref[...])
    krt = pkr[:, 0:256] * cos_ref[...] + pkr[:, 256:512] * sin_ref[...]
    kr_ref[0] = krt[:, 0:ROPE_DIM]
    kq_ref[0] = _pair_keys(lat, krt, wukq_ref[...])
    prw_ref[0] = _dot(hb, wrw_ref[...])
    pcv = _dot(hb, wcv_ref[...])
    glu_ref[0] = pcv[:, 0:CONV_DIM] * _sigmoid(pcv[:, CONV_DIM:2 * CONV_DIM])


def _in_proj(x, mod, lw, tabs, tm):
    B, T, D = x.shape
    nT = T // tm
    cos8, sin8, cos8t, sin8t = tabs
    row = lambda w: pl.BlockSpec((1, tm, w), lambda b, i: (b, i, 0))
    col = pl.BlockSpec((1, VT_ROWS, tm), lambda b, i: (b, 0, i))
    args = (lw['g1'], lw['wq'], lw['wkv'], lw['wkvt'], lw['wkr'], lw['wrw'], lw['wcv'], lw['qg'], lw['wqnt'],
            lw['wqrt'], lw['wqrst'], lw['wukq'], lw['kvg'], lw['kvgc'])
    return pl.pallas_call(
        _in_proj_kernel,
        grid=(B, nT),
        in_specs=[row(D), pl.BlockSpec((1, N_MOD, D), lambda b, i: (b, 0, 0))] + [_full(a) for a in args]
                 + [pl.BlockSpec((tm, 256), lambda b, i: (i, 0))] * 2
                 + [pl.BlockSpec((256, tm), lambda b, i: (0, i))] * 2,
        out_specs=[row(KV_LORA), row(ROPE_DIM), row(PAIR_KEYS), col,
                   pl.BlockSpec((1, MLA_HEADS, 128, tm), lambda b, i: (b, 0, 0, i)),
                   row(RWKV_IN), row(CONV_DIM)],
        out_shape=[jax.ShapeDtypeStruct((B, T, KV_LORA), F32),
                   jax.ShapeDtypeStruct((B, T, ROPE_DIM), F32),
                   jax.ShapeDtypeStruct((B, T, PAIR_KEYS), BF16),
                   jax.ShapeDtypeStruct((B, VT_ROWS, T), BF16),
                   jax.ShapeDtypeStruct((B, MLA_HEADS, 128, T), BF16),
                   jax.ShapeDtypeStruct((B, T, RWKV_IN), F32),
                   jax.ShapeDtypeStruct((B, T, CONV_DIM), F32)],
        compiler_params=_cp(("parallel", "parallel")),
        name="in_proj",
    )(x, mod, *args, cos8, sin8, cos8t, sin8t)


def _key_up_kernel(lat_ref, rope8_ref, wukq_ref, kq_ref):
    kq_ref[0] = _pair_keys(lat_ref[0], rope8_ref[0], wukq_ref[...])


def _key_up(lat, rope8, wukq, tm):
    B, S, _ = lat.shape
    row = lambda w: pl.BlockSpec((1, tm, w), lambda b, i: (b, i, 0))
    return pl.pallas_call(
        _key_up_kernel,
        grid=(B, S // tm),
        in_specs=[row(KV_LORA), row(256), _full(wukq)],
        out_specs=row(PAIR_KEYS),
        out_shape=jax.ShapeDtypeStruct((B, S, PAIR_KEYS), BF16),
        compiler_params=_cp(("parallel", "parallel")),
        name="key_up",
    )(lat, rope8, wukq)


def _attn_kernel(qi_ref, ki_ref, last_ref, qt_ref, k_ref, vt_ref, wuvt_ref, pa_ref, o_ref,
                 qcat, m_s, acc, *, tq, tk, past, n_keys):
    p = pl.program_id(1)
    qi = qi_ref[p]
    ki = ki_ref[p]
    H = MLA_HEADS
    NP = H // 2
    nh = tq // 128
    NC = NP * nh
    diag_aligned = tq == tk and past % tk == 0

    @pl.when(ki == 0)
    def _():
        qcat[...] = jnp.zeros(qcat.shape, BF16)
        for pr in range(NP):
            for hf in range(nh):
                pos = slice(128 * hf, 128 * (hf + 1))
                qcat[pr * nh + hf, 0:128, 0:128] = qt_ref[0, 2 * pr, :, pos]
                qcat[pr * nh + hf, 128:256, 128:256] = qt_ref[0, 2 * pr + 1, :, pos]
        m_s[...] = jnp.full(m_s.shape, -jnp.inf, F32)
        acc[...] = jnp.zeros(acc.shape, F32)

    def step(bias):
        def nkeys(ch):
            return 128 * (ch % nh + 1) if (bias is not None and diag_aligned) else tk

        scores = {}

        def qk(ch):
            pr = ch // nh
            scores[ch] = jnp.dot(k_ref[0, 0:nkeys(ch), 256 * pr:256 * (pr + 1)], qcat[ch],
                                 preferred_element_type=F32)

        for ch in range(min(QK_AHEAD, NC)):
            qk(ch)
        for ch in range(NC):
            if ch + QK_AHEAD < NC:
                qk(ch + QK_AHEAD)
            s = scores.pop(ch)
            if bias is not None:
                s = s + bias[ch % nh][0:nkeys(ch), :]
            m_prev = m_s[ch]
            m_new = jnp.maximum(m_prev, jnp.max(s, axis=0, keepdims=True))
            alpha = jnp.exp2(m_prev - m_new)
            pe = jnp.exp2(s - m_new)
            acc[ch] = alpha * acc[ch] + jnp.dot(vt_ref[0, :, 0:nkeys(ch)], pe.astype(BF16),
                                                preferred_element_type=F32)
            m_s[ch] = m_new

    is_last = last_ref[p] == 1

    @pl.when(jnp.logical_not(is_last))
    def _():
        step(None)

    @pl.when(is_last)
    def _():
        kpos = ki * tk + lax.broadcasted_iota(jnp.int32, (tk, 1), 0)
        biases = []
        for hf in range(nh):
            t = 128 * hf + lax.broadcasted_iota(jnp.int32, (1, 128), 1)
            limit = jnp.minimum((((past + qi * tq + t) >> CHUNK_SHIFT) + 1) << CHUNK_SHIFT, n_keys)
            bias = jnp.where(kpos < limit, 0.0, -jnp.inf)
            biases.append(jnp.concatenate([bias, bias], axis=1))
        step(biases)
        heads = []
        for pr in range(NP):
            o = [acc[pr * nh + hf, 0:KV_LORA, :] / acc[pr * nh + hf, KV_LORA:KV_LORA + 1, :]
                 for hf in range(nh)]
            for j in range(2):
                oh = jnp.concatenate([x[:, 128 * j:128 * (j + 1)] for x in o], axis=1)
                heads.append(_dot(wuvt_ref[2 * pr + j], oh))
        o_ref[0] = _dot(jnp.concatenate(heads, axis=0), pa_ref[...], (((0,), (0,)), ((), ())))


def _attn_tiles(T, past):
    n_keys = past + T
    tq = 512 if T % 512 == 0 else 128
    t_pad = -(-T // tq) * tq
    tk = -(-n_keys // 128) * 128 if n_keys <= 2304 else 512
    s_pad = -(-n_keys // tk) * tk
    return tq, t_pad, tk, s_pad, n_keys


def _attention(qt, keys, latt, wuvt, proj_a, past, T):
    B, H = qt.shape[:2]
    tq, t_pad, tk, s_pad, n_keys = _attn_tiles(T, past)
    assert keys.shape[1] == s_pad and latt.shape[2] == s_pad and qt.shape[3] == t_pad and past % CHUNK == 0
    qi, ki, last = [], [], []
    for i in range(t_pad // tq):
        lim = min(((past + (i + 1) * tq - 1) // CHUNK + 1) * CHUNK, n_keys)
        nk = -(-lim // tk)
        for j in range(nk):
            qi.append(i), ki.append(j), last.append(int(j == nk - 1))
    npairs = len(qi)
    tabs = [jnp.asarray(np.asarray(a, np.int32)) for a in (qi, ki, last)]
    nc = (H // 2) * (tq // 128)
    grid_spec = pltpu.PrefetchScalarGridSpec(
        num_scalar_prefetch=3,
        grid=(B, npairs),
        in_specs=[pl.BlockSpec((1, H, 128, tq), lambda b, p, qi, ki, la: (b, 0, 0, qi[p])),
                  pl.BlockSpec((1, tk, PAIR_KEYS), lambda b, p, qi, ki, la: (b, ki[p], 0)),
                  pl.BlockSpec((1, VT_ROWS, tk), lambda b, p, qi, ki, la: (b, 0, ki[p])),
                  pl.BlockSpec(wuvt.shape, lambda b, p, qi, ki, la: (0, 0, 0)),
                  pl.BlockSpec(proj_a.shape, lambda b, p, qi, ki, la: (0, 0))],
        out_specs=pl.BlockSpec((1, tq, D_MODEL), lambda b, p, qi, ki, la: (b, qi[p], 0)),
        scratch_shapes=[pltpu.VMEM((nc, 256, 256), BF16), pltpu.VMEM((nc, 1, 256), F32),
                        pltpu.VMEM((nc, VT_ROWS, 256), F32)])
    return pl.pallas_call(
        functools.partial(_attn_kernel, tq=tq, tk=tk, past=past, n_keys=n_keys),
        grid_spec=grid_spec,
        out_shape=jax.ShapeDtypeStruct((B, t_pad, D_MODEL), F32),
        compiler_params=_cp(("parallel", "arbitrary")),
        name="attention",
    )(*tabs, qt, keys, latt, wuvt, proj_a)


def _head_sum(x, ones_bd):
    return _dot_hl(x, ones_bd)


def _rwkv_features(pr, prev, wts):
    mu, w0, w2p, a0, a2p, g2, k_k, k_a, r_k, ones_bd = wts
    xs = pr + (prev - pr) * mu
    r = xs[:, 0:256]
    k = xs[:, 256:512]
    v = xs[:, 512:768]
    wa = xs[:, 768:896]
    gl = xs[:, 896:1024]
    nz = -(w0 + _dot(jnp.tanh(wa), w2p))
    softplus = jnp.maximum(nz, 0.0) + jnp.log(1.0 + jnp.exp(-jnp.abs(nz)))
    lw = -jnp.exp(-softplus - 0.5)
    a = _sigmoid(a0 + _dot(wa, a2p))
    g = _dot(_sigmoid(gl), g2)
    kk = k * k_k
    kk = kk * lax.rsqrt(_head_sum(kk * kk, ones_bd) + 1e-12)
    kf = k * (1.0 + (a - 1.0) * k_a)
    bonus = _head_sum(r * kf * r_k, ones_bd) * v
    return r, kf, kk * a, kk, v, lw, g, bonus


def _wkv_kernel(prw_ref, shift_ref, s0_ref, mu_ref, w0_ref, w2_ref, a0_ref, a2_ref, g2_ref, kk_ref, ka_ref, rk_ref,
                ones_ref, lng_ref, lnb_ref, yb_ref, sout_ref, S, carry, *, nb, G, n_valid):
    L = WKV_CHUNK
    W = RWKV_DIM

    @pl.when(pl.program_id(0) == 0)
    def _():
        S[...] = s0_ref[...]
        carry[...] = shift_ref[...]

    wts = tuple(ref[...] for ref in (mu_ref, w0_ref, w2_ref, a0_ref, a2_ref, g2_ref, kk_ref, ka_ref, rk_ref, ones_ref))
    ones_bd = wts[-1]
    ln_g = lng_ref[...]
    ln_b = lnb_ref[...]
    ri = lax.broadcasted_iota(jnp.int32, (W, W), 0)
    ci = lax.broadcasted_iota(jnp.int32, (W, W), 1)
    same = (ri >> HEAD_SHIFT) == (ci >> HEAD_SHIFT)
    in_head = RWKV_HEAD - 1
    strict = same & ((ci & in_head) < (ri & in_head))
    incl = same & ((ci & in_head) <= (ri & in_head))
    eye = (ri == ci).astype(F32)
    tri = (lax.broadcasted_iota(jnp.int32, (L, L), 1) <= lax.broadcasted_iota(jnp.int32, (L, L), 0)).astype(BF16)
    lane_head = lax.broadcasted_iota(jnp.int32, (L, W), 1) >> HEAD_SHIFT
    row = lax.broadcasted_iota(jnp.int32, (L, 1), 0)
    nt = (((1,), (1,)), ((), ()))
    tn = (((0,), (0,)), ((), ()))

    def stack(x):
        return jnp.concatenate([jnp.where(lane_head == hd, x, 0.0) for hd in range(RWKV_HEADS)], axis=0)

    def collapse(z):
        return z[0:L] + z[L:2 * L] + z[2 * L:3 * L] + z[3 * L:4 * L]

    def features(i):
        pr = prw_ref[i]
        prev = jnp.where(row == 0, carry[i], pltpu.roll(pr, 1, 0))
        carry[i] = pr[L - 1:L, :]
        f = _rwkv_features(pr, prev, wts)
        if n_valid < L:
            f = tuple(jnp.where(row < n_valid, x, 0.0) for x in f[:6]) + f[6:]
        return f

    def body(it, _):
        ids = [it * G + g for g in range(G)]
        each = lambda f, *xs: [f(*a) for a in zip(*xs)]
        r, kf, bb, kk, v, ll, gate, bonus = zip(*[features(i) for i in ids])
        s_prev = [S[i] for i in ids]
        split = each(_split, ll)
        c = each(lambda hl: jnp.dot(tri, hl[0], preferred_element_type=F32)
                 + jnp.dot(tri, hl[1], preferred_element_type=F32), split)
        c_last = each(lambda x: x[L - 1:L, :], c)
        e_neg = each(lambda x: jnp.exp(-x), c)
        e_end = each(lambda x, y: jnp.exp(y - x), c, c_last)
        rt = each(lambda x, y: stack(x * jnp.exp(y)), r, c)
        kkt = each(lambda x, y, z: stack(x * jnp.exp(y - z)), kk, c, ll)
        bt = each(lambda x, y: stack(x * y), bb, e_neg)
        kt = each(lambda x, y: stack(x * y), kf, e_neg)
        a_m = each(lambda x, y: jnp.where(strict, _dot(x, y, nt), 0.0), kkt, bt)
        b_m = each(lambda x, y: jnp.where(strict, _dot(x, y, nt), 0.0), kkt, kt)
        m1 = each(lambda x, y: jnp.where(incl, _dot(x, y, nt), 0.0), rt, bt)
        m2 = each(lambda x, y: jnp.where(incl, _dot(x, y, nt), 0.0), rt, kt)
        t_m = each(lambda x: eye - x, a_m)
        a_p = a_m
        for _ in range(5):
            a_p = each(lambda x: _dot(x, x), a_p)
            t_m = each(lambda x, y: x + _dot(x, y), t_m, a_p)
        vs = each(stack, v)
        w1 = each(lambda x, y: collapse(_dot(x, y)), t_m, kkt)
        bv = each(_dot, b_m, vs)
        w2 = each(lambda x, y: collapse(_dot(x, y)), t_m, bv)
        u = each(lambda x, y, z: -(_dot(x, y, nt) + z), w1, s_prev, w2)
        y0 = each(lambda x, y: _dot(collapse(x), y, nt), rt, s_prev)
        y1 = each(lambda a, b_, c_, d: collapse(_dot(a, stack(b_)) + _dot(c_, d)), m1, u, m2, vs)
        upd = each(lambda a, b_, c_, d, e: _dot(a, b_ * e, tn) + _dot(c_, d * e, tn), u, bb, v, kf, e_end)
        for g, i in enumerate(ids):
            S[i] = s_prev[g] * jnp.exp(c_last[g]) + jnp.where(same, upd[g], 0.0)
            y = y0[g] + y1[g]
            mean = _head_sum(y, ones_bd) * (1.0 / RWKV_HEAD)
            yc = y - mean
            var = _head_sum(yc * yc, ones_bd) * (1.0 / RWKV_HEAD)
            yb = (yc * lax.rsqrt(var + GN_EPS) * ln_g + ln_b + bonus[g]) * gate[g]
            yb_ref[i] = yb.astype(BF16)
        return 0

    lax.fori_loop(0, nb // G, body, 0)

    @pl.when(pl.program_id(0) == pl.num_programs(0) - 1)
    def _():
        sout_ref[...] = S[...]


def _wkv(prw, shift0, s0, lw, n_valid):
    B, T, _ = prw.shape
    L = WKV_CHUNK
    W = RWKV_DIM
    assert T % L == 0 and (n_valid == L or T == L)
    args = (lw['mu'], lw['w0'], lw['w2p'], lw['a0'], lw['a2p'], lw['g2'], lw['k_k'], lw['k_a'], lw['r_k'],
            lw['ones_bd'], lw['ln_g'], lw['ln_b'])
    st = pl.BlockSpec((B, W, W), lambda c: (0, 0, 0))
    return pl.pallas_call(
        functools.partial(_wkv_kernel, nb=B, G=8 if B % 8 == 0 else 1, n_valid=n_valid),
        grid=(T // L,),
        in_specs=[pl.BlockSpec((B, L, RWKV_IN), lambda c: (0, c, 0)), _full(shift0), st] + [_full(a) for a in args],
        out_specs=[pl.BlockSpec((B, L, W), lambda c: (0, c, 0)), st],
        out_shape=[jax.ShapeDtypeStruct((B, T, W), BF16), jax.ShapeDtypeStruct((B, W, W), F32)],
        scratch_shapes=[pltpu.VMEM((B, W, W), F32), pltpu.VMEM((B, 1, RWKV_IN), F32)],
        compiler_params=_cp(("arbitrary",)),
        name="wkv",
    )(prw, shift0, s0, *args)


def _conv_stage(ext, glu, st_ref, first, tm):
    @pl.when(first)
    def _():
        ext[0:CONV_HALO, :] = st_ref[0]
    ext[CONV_HALO:CONV_HALO + tm, :] = glu


def _conv_module(ext, dw_ref, dwb, lng, lnb, new_ref, tm):
    off = CONV_HALO - (CONV_WIDTH - 1)
    ext_v = ext[...]
    n = CONV_HALO + tm
    acc = jnp.zeros((tm, CONV_DIM), F32)
    for res in range(8):
        taps = [w for w in range(CONV_WIDTH) if (off + w) % 8 == res]
        if not taps:
            continue
        sh = ext_v if res == 0 else pltpu.roll(ext_v, n - res, 0)
        for w in taps:
            base = off + w - res
            acc = acc + sh[base:base + tm, :] * dw_ref[w:w + 1, :]
    y = acc + dwb
    mu = jnp.mean(y, axis=-1, keepdims=True)
    var = jnp.mean(jnp.square(y - mu), axis=-1, keepdims=True)
    y = (y - mu) * lax.rsqrt(var + LN_EPS) * lng + lnb
    tail = ext[tm:tm + CONV_HALO, :]
    new_ref[0] = tail
    ext[0:CONV_HALO, :] = tail
    return y * _sigmoid(y)


def _route(logits, ax):
    idx = lax.broadcasted_iota(jnp.int32, logits.shape, ax)
    big = 4 * ROUTER_LANES
    neg = -jnp.inf
    rmax = lambda x: jnp.max(x, axis=ax, keepdims=True)
    rmin = lambda x: jnp.min(x, axis=ax, keepdims=True)
    is_g = (idx >= N_EXPERTS) & (idx < N_EXPERTS + N_GROUPS)
    gl = jnp.where(is_g, logits, neg)
    gmax = rmax(gl)
    g_val = 1.0 / jnp.sum(jnp.exp(gl - gmax), axis=ax, keepdims=True)
    g_idx = rmin(jnp.where(is_g & (gl == gmax), idx, big)) - N_EXPERTS
    in_grp = (idx >= g_idx * EXPERTS_PER_GROUP) & (idx < (g_idx + 1) * EXPERTS_PER_GROUP)
    el = jnp.where(in_grp, logits, neg)
    v1 = rmax(el)
    i1 = rmin(jnp.where(in_grp & (el == v1), idx, big))
    el2 = jnp.where(idx == i1, neg, el)
    v2 = rmax(el2)
    i2 = rmin(jnp.where(in_grp & (idx != i1) & (el2 == v2), idx, big))
    e2 = jnp.exp(v2 - v1)
    w1 = g_val / (1.0 + e2)
    w2 = g_val * e2 / (1.0 + e2)
    return jnp.where(idx == i1, w1, 0.0) + jnp.where(idx == i2, w2, 0.0)


def _merge_kernel(x_ref, mod_ref, g1_ref, wg_ref, oa_ref, yb_ref, pb_ref, glu_ref, st_ref, dw_ref, dwb_ref, clg_ref,
                  clb_ref, pc_ref, wo_ref, g2_ref, wr_ref, br_ref, wrt_ref, brt_ref, x1_ref, h2_ref, comb_ref, cnew_ref,
                  ext):
    x = x_ref[0]
    mod = mod_ref[0]
    tm = x.shape[0]
    _conv_stage(ext, glu_ref[0], st_ref, pl.program_id(1) == 0, tm)
    yc = _conv_module(ext, dw_ref, dwb_ref[...], clg_ref[...], clb_ref[...], cnew_ref, tm)
    hb = (_rms(x, g1_ref[...]) * (1.0 + mod[1:2]) + mod[0:1]).astype(BF16)
    D = D_MODEL
    merged = _sigmoid(_dot(hb, wg_ref[:, 0:D])) * oa_ref[0]
    merged = merged + _sigmoid(_dot(hb, wg_ref[:, D:2 * D])) * _dot(yb_ref[0], pb_ref[...])
    merged = merged + _sigmoid(_dot(hb, wg_ref[:, 2 * D:3 * D])) * _dot(yc, pc_ref[...])
    x1 = x + mod[2:3] * _dot(merged, wo_ref[...])
    x1_ref[0] = x1
    h2 = _rms(x1, g2_ref[...]) * (1.0 + mod[4:5]) + mod[3:4]
    h2_ref[0] = h2.astype(BF16)
    if tm % 128 == 0:
        comb_ref[0] = _route(_dot_split_nt(wrt_ref[...], h2) + brt_ref[...], 0).T
    else:
        comb_ref[0] = _route(_dot_split(h2, wr_ref[...]) + br_ref[...], 1)


def _merge(x, mod, oa, yb, glu, st_pad, lw, tm):
    B, T, D = x.shape
    row = lambda w: pl.BlockSpec((1, tm, w), lambda b, i: (b, i, 0))
    modspec = pl.BlockSpec((1, N_MOD, D), lambda b, i: (b, 0, 0))
    halo = pl.BlockSpec((1, CONV_HALO, CONV_DIM), lambda b, i: (b, 0, 0))
    ins = [(x, row(D)), (mod, modspec), (lw['g1'], None), (lw['wg'], None), (oa, row(D)), (yb, row(256)),
           (lw['proj_b'], None), (glu, row(CONV_DIM)), (st_pad, halo), (lw['conv_dw'], None),
           (lw['conv_dw_b'], None), (lw['conv_ln_g'], None), (lw['conv_ln_b'], None), (lw['proj_c'], None),
           (lw['w_out'], None), (lw['g2n'], None), (lw['wr'], None), (lw['br'], None), (lw['wrt'], None),
           (lw['brt'], None)]
    return pl.pallas_call(
        _merge_kernel,
        grid=(B, T // tm),
        in_specs=[s if s is not None else _full(a) for a, s in ins],
        out_specs=[row(D), row(D), row(ROUTER_LANES), halo],
        out_shape=[jax.ShapeDtypeStruct((B, T, D), F32), jax.ShapeDtypeStruct((B, T, D), BF16),
                   jax.ShapeDtypeStruct((B, T, ROUTER_LANES), F32),
                   jax.ShapeDtypeStruct((B, CONV_HALO, CONV_DIM), F32)],
        scratch_shapes=[pltpu.VMEM((CONV_HALO + tm, CONV_DIM), F32)],
        compiler_params=_cp(("parallel", "arbitrary")),
        name="merge",
    )(*[a for a, _ in ins])


def _moe_kernel(h2_ref, comb_ref, x1_ref, gt_ref, wgu_ref, wd_ref, fg_ref, o_ref, *, final, rb):
    s = pl.program_id(2)
    tm = h2_ref.shape[1]
    nrb = tm // rb
    DE = D_EXPERT

    @pl.when(s == 0)
    def _():
        o_ref[...] = jnp.zeros(o_ref.shape, F32)

    wgu = wgu_ref[0]
    wd = wd_ref[0]
    gus = {}

    def up(i):
        gus[i] = _dot(h2_ref[0, i * rb:(i + 1) * rb, :], wgu)

    up(0)
    for i in range(nrb):
        if i + 1 < nrb:
            up(i + 1)
        gu = gus.pop(i)
        rows = slice(i * rb, (i + 1) * rb)
        comb = comb_ref[0, rows, :]
        lane = lax.broadcasted_iota(jnp.int32, comb.shape, 1)
        cws = [jnp.sum(jnp.where(lane == MOE_EPS * s + j, comb, 0.0), axis=1, keepdims=True) for j in range(MOE_EPS)]
        cw = jnp.concatenate([jnp.broadcast_to(c, (rb, DE)) for c in cws], axis=1)
        gate = gu[:, 0:MOE_EPS * DE]
        act = gate * _sigmoid(gate) * gu[:, MOE_EPS * DE:2 * MOE_EPS * DE] * cw
        o_ref[0, rows, :] += _dot(act, wd)

    @pl.when(s == N_EXPERTS // MOE_EPS - 1)
    def _():
        x2 = x1_ref[0] + gt_ref[0] * o_ref[0]
        o_ref[0] = _rms(x2, fg_ref[...]) if final else x2


def _moe(h2, comb, x1, gt, lw, final_g, tm, final):
    B, T, D = x1.shape
    gt_rows = gt.shape[1]
    gt_spec = (pl.BlockSpec((1, 1, D), lambda b, i, e: (b, 0, 0)) if gt_rows == 1
               else pl.BlockSpec((1, tm, D), lambda b, i, e: (b, i, 0)))
    row = lambda w: pl.BlockSpec((1, tm, w), lambda b, i, e: (b, i, 0))
    return pl.pallas_call(
        functools.partial(_moe_kernel, final=final, rb=min(tm, MOE_ROW_BLOCK)),
        grid=(B, T // tm, N_EXPERTS // MOE_EPS),
        in_specs=[row(D), row(ROUTER_LANES), row(D), gt_spec,
                  pl.BlockSpec((1, D, 2 * MOE_EPS * D_EXPERT), lambda b, i, e: (e, 0, 0)),
                  pl.BlockSpec((1, MOE_EPS * D_EXPERT, D), lambda b, i, e: (e, 0, 0)),
                  pl.BlockSpec((1, D), lambda b, i, e: (0, 0))],
        out_specs=row(D),
        out_shape=jax.ShapeDtypeStruct((B, T, D), F32),
        compiler_params=_cp(("parallel", "parallel", "arbitrary")),
        name="moe",
    )(h2, comb, x1, gt, lw['wgu'], lw['wd'], final_g)


def _layer_weights(l, w):
    D = D_MODEL
    w_in = w['w_in'][l]
    c_q, c_kv, c_kr = Q_LORA, Q_LORA + KV_LORA, MLA_IN
    c_rw, c_cv = MLA_IN + RWKV_IN, MLA_IN + RWKV_IN + 2 * CONV_DIM
    half = ROPE_DIM // 2
    swap = np.concatenate([np.arange(half, ROPE_DIM), np.arange(half)])
    wkr = w_in[:, c_kv:c_kr]
    q_up = w['q_up'][l].reshape(Q_LORA, MLA_HEADS, NOPE_DIM + ROPE_DIM)
    q_rope = q_up[:, :, NOPE_DIM:]
    wukq = jnp.zeros((KV_LORA, PAIR_KEYS), F32)
    for hd in range(MLA_HEADS):
        wukq = wukq.at[:, 128 * hd:128 * hd + NOPE_DIM].set(w['w_uk'][l][:, hd, :])
    zl = jnp.zeros((W_LORA, RWKV_DIM), F32)
    head = np.arange(RWKV_DIM) // RWKV_HEAD
    wr = jnp.zeros((D, ROUTER_LANES), F32)
    wr = wr.at[:, 0:N_EXPERTS].set(w['router_expert_w'][l]).at[:, N_EXPERTS:N_EXPERTS + N_GROUPS].set(
        w['router_group_w'][l])
    br = jnp.zeros((1, ROUTER_LANES), F32)
    br = br.at[0, 0:N_EXPERTS].set(w['router_expert_b'][l]).at[0, N_EXPERTS:N_EXPERTS + N_GROUPS].set(
        w['router_group_b'][l])
    r2 = lambda a: a.reshape(1, -1)
    pair_cols = lambda a: jnp.concatenate([a[j::MOE_EPS] for j in range(MOE_EPS)], axis=2)
    return dict(
        g1=r2(w['norm1_g'][l]), g2n=r2(w['norm2_g'][l]),
        wq=w_in[:, :c_q].astype(BF16), wkv=w_in[:, c_q:c_kv].astype(BF16),
        wkr=jnp.concatenate([jnp.tile(wkr, (1, 8)), jnp.tile(wkr[:, swap], (1, 8))], axis=1).astype(BF16),
        wrw=w_in[:, c_kr:c_rw].astype(BF16), wcv=w_in[:, c_rw:c_cv].astype(BF16), wg=w_in[:, c_cv:].astype(BF16),
        wkvt=w_in[:, c_q:c_kv].T.astype(BF16), kvgc=w['kv_norm_g'][l].reshape(-1, 1),
        qg=r2(w['q_norm_g'][l]), wqnt=q_up[:, :, :NOPE_DIM].reshape(Q_LORA, -1).T.astype(BF16),
        wukq=wukq.astype(BF16), kvg=r2(w['kv_norm_g'][l]),
        wqrt=q_rope.reshape(Q_LORA, -1).T.astype(BF16), wqrst=q_rope[:, :, swap].reshape(Q_LORA, -1).T.astype(BF16),
        wuvt=jnp.transpose(w['w_uv'][l], (1, 2, 0)).astype(BF16), proj_a=w['proj_a'][l].astype(BF16),
        mu=r2(w['rwkv_mu'][l]), w0=r2(w['rwkv_w0'][l]), a0=r2(w['rwkv_a0'][l]),
        w2p=jnp.concatenate([w['rwkv_w2'][l], zl], axis=0).astype(BF16),
        a2p=jnp.concatenate([zl, w['rwkv_a2'][l]], axis=0).astype(BF16),
        g2=w['rwkv_g2'][l].astype(BF16), k_k=r2(w['rwkv_k_k'][l]), k_a=r2(w['rwkv_k_a'][l]),
        r_k=r2(w['rwkv_r_k'][l]), ln_g=r2(w['rwkv_ln_g'][l]), ln_b=r2(w['rwkv_ln_b'][l]),
        ones_bd=jnp.asarray((head[:, None] == head[None, :]).astype(np.float32)).astype(BF16),
        proj_b=w['proj_b'][l].astype(BF16),
        conv_dw=w['conv_dw'][l], conv_dw_b=r2(w['conv_dw_b'][l]), conv_ln_g=r2(w['conv_ln_g'][l]),
        conv_ln_b=r2(w['conv_ln_b'][l]), proj_c=w['proj_c'][l].astype(BF16),
        w_out=w['w_out'][l].astype(BF16), wr=wr, br=br, wrt=wr.T, brt=br.T,
        wgu=jnp.concatenate([pair_cols(w['moe_w_gate'][l]), pair_cols(w['moe_w_up'][l])], axis=2).astype(BF16),
        wd=w['moe_w_down'][l].reshape(N_EXPERTS // MOE_EPS, MOE_EPS * D_EXPERT, D).astype(BF16),
    )


def _rope_tables(past, T):
    half = ROPE_DIM // 2
    inv = ROPE_THETA ** (-jnp.arange(half, dtype=F32) / half)
    ang = (past + jnp.arange(T)).astype(F32)[:, None] * inv[None, :]
    cos, sin = jnp.cos(ang), jnp.sin(ang)
    cos8 = jnp.tile(jnp.concatenate([cos, cos], axis=1), (1, 8))
    sin8 = jnp.tile(jnp.concatenate([-sin, sin], axis=1), (1, 8))
    return cos8, sin8, cos8.T, sin8.T


def _row_tile(T, cap):
    return min(T, cap)


def _trunk_layer(x, mod, cache_lat, cache_rope, shift_st, wkv_st, conv_st, lw, final_g, final):
    B, T, D = x.shape
    past = 0 if cache_lat is None else cache_lat.shape[1]
    tm = _row_tile(T, MERGE_ROWS)
    lat, kr, keys, latt, qt, prw, glu = _in_proj(x, mod, lw, _rope_tables(past, T), _row_tile(T, IN_PROJ_ROWS))

    _, t_pad, _, s_pad, n_keys = _attn_tiles(T, past)
    row_pad = ((0, 0), (0, s_pad - n_keys), (0, 0))
    if past:
        lat_all = jnp.pad(jnp.concatenate([cache_lat, lat], axis=1), row_pad)
        rope_all = jnp.pad(jnp.concatenate([cache_rope, kr], axis=1), row_pad)
        key_tile = max(d for d in range(8, 2305, 8) if s_pad % d == 0)
        keys = _key_up(lat_all, jnp.tile(rope_all, (1, 1, 8)), lw['wukq'], key_tile)
        old_t = jnp.concatenate([jnp.swapaxes(cache_lat, 1, 2).astype(BF16),
                                 jnp.ones((B, VT_ROWS - KV_LORA, past), BF16)], axis=1)
        latt = jnp.concatenate([old_t, latt], axis=2)
    elif s_pad != n_keys:
        keys = jnp.pad(keys, row_pad)
    if s_pad != n_keys:
        latt = jnp.pad(latt, ((0, 0), (0, 0), (0, s_pad - n_keys)))
    if t_pad != T:
        qt = jnp.pad(qt, ((0, 0), (0, 0), (0, 0), (0, t_pad - T)))
    o_a = _attention(qt, keys, latt, lw['wuvt'], lw['proj_a'], past, T)[:, :T]

    L = WKV_CHUNK
    Tp = -(-T // L) * L
    prw_p = prw if Tp == T else jnp.pad(prw, ((0, 0), (0, Tp - T), (0, 0)))
    shift0 = jnp.zeros((B, 1, RWKV_IN), F32) if shift_st is None else shift_st[:, None, :]
    W = RWKV_DIM
    s0 = jnp.zeros((B, W, W), F32)
    if wkv_st is not None:
        for hd in range(RWKV_HEADS):
            o = hd * RWKV_HEAD
            s0 = s0.at[:, o:o + RWKV_HEAD, o:o + RWKV_HEAD].set(wkv_st[:, hd])
    yb, s_bd = _wkv(prw_p, shift0, s0, lw, L if Tp == T else T)
    yb = yb[:, :T]
    wkv_new = jnp.stack([s_bd[:, hd * RWKV_HEAD:(hd + 1) * RWKV_HEAD, hd * RWKV_HEAD:(hd + 1) * RWKV_HEAD]
                         for hd in range(RWKV_HEADS)], axis=1)

    pad_rows = CONV_HALO - (CONV_WIDTH - 1)
    st = jnp.zeros((B, CONV_WIDTH - 1, CONV_DIM), F32) if conv_st is None else conv_st
    x1, h2, comb, conv_tail = _merge(x, mod, o_a, yb, glu, jnp.pad(st, ((0, 0), (pad_rows, 0), (0, 0))), lw, tm)
    conv_new = conv_tail[:, pad_rows:]

    if T >= MOE_ROWS:
        x2 = _moe(h2, comb, x1, mod[:, 5:6], lw, final_g, MOE_ROWS, final)
    else:
        gt = jnp.broadcast_to(mod[:, 5:6], (B, T, D)).reshape(1, B * T, D)
        x2 = _moe(h2.reshape(1, B * T, D), comb.reshape(1, B * T, ROUTER_LANES), x1.reshape(1, B * T, D), gt, lw,
                  final_g, B * T, final).reshape(B, T, D)
    return x2, (lat, kr, prw[:, -1], wkv_new, conv_new)


def kernel(x_prompt, x_sample, c_prompt, c_sample, cache_kv_latent, cache_k_rope, state_rwkv_shift, state_rwkv_wkv, state_conv, ada_w, ada_b, norm1_g, norm2_g, w_in, q_norm_g, q_up, kv_norm_g, w_uk, w_uv, proj_a, rwkv_mu, rwkv_w0, rwkv_w2, rwkv_a0, rwkv_a2, rwkv_g2, rwkv_k_k, rwkv_k_a, rwkv_r_k, rwkv_ln_g, rwkv_ln_b, proj_b, conv_dw, conv_dw_b, conv_ln_g, conv_ln_b, proj_c, w_out, router_group_w, router_group_b, router_expert_w, router_expert_b, moe_w_gate, moe_w_up, moe_w_down, final_g):
    w = dict(norm1_g=norm1_g, norm2_g=norm2_g, w_in=w_in, q_norm_g=q_norm_g, q_up=q_up, kv_norm_g=kv_norm_g,
             w_uk=w_uk, w_uv=w_uv, proj_a=proj_a, rwkv_mu=rwkv_mu, rwkv_w0=rwkv_w0, rwkv_w2=rwkv_w2,
             rwkv_a0=rwkv_a0, rwkv_a2=rwkv_a2, rwkv_g2=rwkv_g2, rwkv_k_k=rwkv_k_k, rwkv_k_a=rwkv_k_a,
             rwkv_r_k=rwkv_r_k.reshape(rwkv_r_k.shape[0], -1), rwkv_ln_g=rwkv_ln_g, rwkv_ln_b=rwkv_ln_b,
             proj_b=proj_b, conv_dw=conv_dw, conv_dw_b=conv_dw_b, conv_ln_g=conv_ln_g, conv_ln_b=conv_ln_b,
             proj_c=proj_c, w_out=w_out, router_group_w=router_group_w, router_group_b=router_group_b,
             router_expert_w=router_expert_w, router_expert_b=router_expert_b, moe_w_gate=moe_w_gate,
             moe_w_up=moe_w_up, moe_w_down=moe_w_down)
    depth = ada_w.shape[0]
    bp = x_prompt.shape[0]
    D = D_MODEL
    mod_all = _ada_mod(jnp.concatenate([c_prompt, c_sample], axis=0), ada_w, ada_b)
    fg = final_g.reshape(1, D)
    hp, hs = x_prompt, x_sample
    new_p, new_s = [], []
    for l in range(depth):
        lw = _layer_weights(l, w)
        mod_p = mod_all[l, :bp].reshape(bp, N_MOD, D)
        mod_s = mod_all[l, bp:].reshape(-1, N_MOD, D)
        final = l == depth - 1
        hp, st_p = _trunk_layer(hp, mod_p, None, None, None, None, None, lw, fg, final)
        hs, st_s = _trunk_layer(hs, mod_s, cache_kv_latent[l], cache_k_rope[l], state_rwkv_shift[l],
                                state_rwkv_wkv[l], state_conv[l], lw, fg, final)
        new_p.append(st_p)
        new_s.append(st_s)
    stack = lambda states, i: jnp.stack([s[i] for s in states], axis=0)
    return ((hp, hs) + tuple(stack(new_p, i) for i in range(5)) + tuple(stack(new_s, i) for i in range(5)))
```
